```python
import math
import jax, jax.numpy as jnp
from jax import lax
import numpy as np

D_MODEL = 2048
BATCH = 1
SEQ = 8192
DEPTH = 2
DEC_BATCH = 128
DEC_SEQ = 8
PAST_LEN = 2048
PAGE_SIZE = 128

GROUP_W = D_MODEL // 4
RET_HEADS = 4
RET_HD = GROUP_W // RET_HEADS
RET_CHUNK = 128
RET_ROPE_BASE = 10000.0
DIFF_HEADS = 4
DIFF_VD = GROUP_W // DIFF_HEADS
DIFF_QD = DIFF_VD // 2
ROPE_THETA = 500000.0
ROPE_DIM = DIFF_QD // 4
Q_BLOCK = 128
RWKV_HD = 64
RWKV_HEADS = GROUP_W // RWKV_HD
RWKV_DECAY_LORA = 64
RWKV_A_LORA = 64
RWKV_GATE_LORA = 128
RWKV_LN_EPS = 64e-5
S5_CH = 16
S5_GROUPS = GROUP_W // S5_CH
S5_N = 64
D_FF = 5632
CONV_W = 3
PLE_DIM = 256
EPS = 1e-6
NEG_INF = -1e30

RET_COLS = 4 * GROUP_W
DIFF_COLS = 3 * GROUP_W
RWKV_COLS = 3 * GROUP_W + RWKV_DECAY_LORA + RWKV_A_LORA + RWKV_GATE_LORA
S5_COLS = GROUP_W
IN_COLS = RET_COLS + DIFF_COLS + RWKV_COLS + S5_COLS
SPLITS = (RET_COLS, RET_COLS + DIFF_COLS, RET_COLS + DIFF_COLS + RWKV_COLS)

kernel_name = 'hybrid_parallel_head_groups_step'


def rmsnorm(x, g):
    xf = x.astype(jnp.float32)
    y = xf * lax.rsqrt(jnp.mean(xf * xf, axis=-1, keepdims=True) + EPS)
    return (y * g.astype(jnp.float32)).astype(x.dtype)


def group_norm(x, w, b, eps):
    xf = x.astype(jnp.float32)
    mu = jnp.mean(xf, axis=-1, keepdims=True)
    var = jnp.mean(jnp.square(xf - mu), axis=-1, keepdims=True)
    return (xf - mu) * lax.rsqrt(var + eps) * w.astype(jnp.float32) + b.astype(jnp.float32)


def rope(x, pos, base, rot_dim):
    half = rot_dim // 2
    inv = jnp.power(base, -jnp.arange(half, dtype=jnp.float32) / half)
    ang = pos.astype(jnp.float32)[:, None] * inv[None, :]
    cos = jnp.cos(ang)[:, None, :]
    sin = jnp.sin(ang)[:, None, :]
    xf = x.astype(jnp.float32)
    x1, x2, rest = xf[..., :half], xf[..., half:rot_dim], xf[..., rot_dim:]
    return jnp.concatenate([x1 * cos - x2 * sin, x1 * sin + x2 * cos, rest], axis=-1).astype(x.dtype)


def retention_scan(q, k, v, s0):
    B, T, H, d = q.shape
    L = RET_CHUNK if T % RET_CHUNK == 0 else T
    nc = T // L
    log_g = jnp.log1p(-jnp.exp2(-5.0 - jnp.arange(H, dtype=jnp.float32)))
    idx = jnp.arange(L, dtype=jnp.float32)
    rel = idx[:, None] - idx[None, :]
    dmask = jnp.where(rel >= 0, jnp.exp(log_g[:, None, None] * jnp.maximum(rel, 0.0)), 0.0)
    q_dec = jnp.exp(log_g[:, None] * (idx + 1.0))[:, :, None]
    k_dec = jnp.exp(log_g[:, None] * (L - 1.0 - idx))[:, :, None]
    c_dec = jnp.exp(log_g * L)[:, None, None]

    def to_chunks(t):
        return t.astype(jnp.float32).reshape(B, nc, L, H, d).transpose(1, 0, 3, 2, 4)

    def step(S, qkv):
        qc, kc, vc = qkv
        att = jnp.einsum('bhld,bhmd->bhlm', qc, kc) * dmask
        o = jnp.einsum('bhlm,bhme->bhle', att, vc) + jnp.einsum('bhld,bhde->bhle', qc * q_dec, S)
        S = S * c_dec + jnp.einsum('bhld,bhle->bhde', kc * k_dec, vc)
        return S, o

    S, o = lax.scan(step, s0.astype(jnp.float32), (to_chunks(q), to_chunks(k), to_chunks(v)))
    return o.transpose(1, 0, 3, 2, 4).reshape(B, T, H, d), S


def retention_mixer(cols, pos, s0, norm_w, norm_b):
    B, T, _ = cols.shape
    q, k, v, g = jnp.split(cols, 4, axis=-1)
    q = rope(q.reshape(B, T, RET_HEADS, RET_HD), pos, RET_ROPE_BASE, RET_HD)
    k = rope(k.reshape(B, T, RET_HEADS, RET_HD), pos, RET_ROPE_BASE, RET_HD) * (RET_HD ** -0.5)
    v = v.reshape(B, T, RET_HEADS, RET_HD)
    o, s_new = retention_scan(q, k, v, s0)
    o = group_norm(o, norm_w, norm_b, EPS) * jax.nn.silu(g.astype(jnp.float32)).reshape(B, T, RET_HEADS, RET_HD)
    return o.reshape(B, T, GROUP_W), s_new


def diff_block(qi, qpos, k_all, v_all, kpos, lam):
    s = jnp.einsum('bqhmd,bkhmd->bhmqk', qi, k_all, preferred_element_type=jnp.float32) * (DIFF_QD ** -0.5)
    s = jnp.where(kpos[None, :] <= qpos[:, None], s, NEG_INF)
    pr = jax.nn.softmax(s, axis=-1)
    pd = pr[:, :, 0] - lam * pr[:, :, 1]
    return jnp.einsum('bhqk,bkhd->bqhd', pd, v_all)


def diff_mixer(cols, pos, lam, lam_init, subln_w, k_past, v_past):
    B, T, _ = cols.shape
    q, k, v = jnp.split(cols, 3, axis=-1)
    q = rope(q.reshape(B, T, DIFF_HEADS * 2, DIFF_QD), pos, ROPE_THETA, ROPE_DIM).reshape(B, T, DIFF_HEADS, 2, DIFF_QD)
    k = rope(k.reshape(B, T, DIFF_HEADS * 2, DIFF_QD), pos, ROPE_THETA, ROPE_DIM).reshape(B, T, DIFF_HEADS, 2, DIFF_QD)
    v = v.reshape(B, T, DIFF_HEADS, DIFF_VD)
    if k_past is None:
        nb = T // Q_BLOCK
        qb = jnp.moveaxis(q.reshape(B, nb, Q_BLOCK, DIFF_HEADS, 2, DIFF_QD), 1, 0)
        kpos = jnp.arange(T)
        vf = v.astype(jnp.float32)

        def blk(args):
            qi, bi = args
            return diff_block(qi, bi * Q_BLOCK + jnp.arange(Q_BLOCK), k, vf, kpos, lam)

        o = lax.map(blk, (qb, jnp.arange(nb)))
        o = jnp.moveaxis(o, 0, 1).reshape(B, T, DIFF_HEADS, DIFF_VD)
    else:
        P = k_past.shape[1]
        k_all = jnp.concatenate([k_past.astype(k.dtype), k], axis=1)
        v_all = jnp.concatenate([v_past.astype(jnp.float32), v.astype(jnp.float32)], axis=1)
        o = diff_block(q, P + jnp.arange(T), k_all, v_all, jnp.arange(P + T), lam)
    o = rmsnorm(o, subln_w) * (1.0 - lam_init)
    return o.reshape(B, T, GROUP_W), k.reshape(B, T, DIFF_HEADS, 2 * DIFF_QD), v


def rwkv7_scan(r, w, k, v, a, b, s0):
    def step(S, inp):
        rt, wt, kt, vt, at, bt = inp
        sa = jnp.einsum('bhij,bhj->bhi', S, at)
        S = S * wt[:, :, None, :] + sa[..., None] * bt[:, :, None, :] + vt[..., None] * kt[:, :, None, :]
        return S, jnp.einsum('bhij,bhj->bhi', S, rt)

    seq = tuple(jnp.moveaxis(t, 1, 0) for t in (r, w, k, v, a, b))
    S, y = lax.scan(step, s0.astype(jnp.float32), seq)
    return jnp.moveaxis(y, 0, 1), S


def rwkv_mixer(cols, shift0, s0, mu, w0, w2, a0, a2, g2, k_k, k_a, r_k, ln_w, ln_b):
    B, T, _ = cols.shape
    f32 = jnp.float32
    prev = jnp.concatenate([shift0[:, None].astype(cols.dtype), cols[:, :-1]], axis=1)
    xm = (cols + (prev - cols) * mu).astype(f32)
    o1 = GROUP_W
    o4 = 3 * GROUP_W + RWKV_DECAY_LORA
    r, k, v, wl, al, gl = jnp.split(xm, [o1, 2 * o1, 3 * o1, o4, o4 + RWKV_A_LORA], axis=-1)
    w = -jax.nn.softplus(-(w0 + jnp.tanh(wl) @ w2)) - 0.5
    decay = jnp.exp(-jnp.exp(w.astype(f32)))
    a = jax.nn.sigmoid(a0 + al @ a2)
    g = jax.nn.sigmoid(gl) @ g2
    hs = lambda t: t.astype(f32).reshape(B, T, RWKV_HEADS, RWKV_HD)
    r, k, v, decay, a = hs(r), hs(k), hs(v), hs(decay), hs(a)
    kk = k * k_k.astype(f32).reshape(RWKV_HEADS, RWKV_HD)
    kk = kk * lax.rsqrt(jnp.maximum(jnp.sum(kk * kk, axis=-1, keepdims=True), 1e-24))
    k = k * (1.0 + (a - 1.0) * k_a.astype(f32).reshape(RWKV_HEADS, RWKV_HD))
    y, s_new = rwkv7_scan(r, decay, k, v, -kk, kk * a, s0)
    y = group_norm(y, ln_w, ln_b, RWKV_LN_EPS)
    y = y + jnp.sum(r * k * r_k.astype(f32), axis=-1, keepdims=True) * v
    return y.reshape(B, T, GROUP_W) * g, s_new, cols[:, -1]


def s5_combine(e1, e2):
    a1r, a1i, b1r, b1i = e1
    a2r, a2i, b2r, b2i = e2
    return (a2r * a1r - a2i * a1i, a2r * a1i + a2i * a1r,
            a2r * b1r - a2i * b1i + b2r, a2r * b1i + a2i * b1r + b2i)


def s5_mixer(u, s0_re, s0_im, lam_re, lam_im, log_step, b_re, b_im, c_re, c_im, d_skip, w_glu, b_glu, norm_w):
    B, T, _ = u.shape
    f32 = jnp.float32
    uf = u.astype(f32).reshape(B, T, S5_GROUPS, S5_CH)
    lr, li = lam_re.astype(f32), lam_im.astype(f32)
    dt = jnp.exp(log_step.astype(f32))[:, None]
    mag = jnp.exp(lr * dt)
    ab_re, ab_im = mag * jnp.cos(li * dt), mag * jnp.sin(li * dt)
    den = lr * lr + li * li
    cf_re = ((ab_re - 1.0) * lr + ab_im * li) / den
    cf_im = (ab_im * lr - (ab_re - 1.0) * li) / den
    br, bi = b_re.astype(f32), b_im.astype(f32)
    bb_re = cf_re[..., None] * br - cf_im[..., None] * bi
    bb_im = cf_re[..., None] * bi + cf_im[..., None] * br
    bu_re = jnp.einsum('btgc,gnc->btgn', uf, bb_re)
    bu_im = jnp.einsum('btgc,gnc->btgn', uf, bb_im)
    s0r, s0i = s0_re.astype(f32), s0_im.astype(f32)
    bu_re = bu_re.at[:, 0].add(ab_re * s0r - ab_im * s0i)
    bu_im = bu_im.at[:, 0].add(ab_re * s0i + ab_im * s0r)
    a_re = jnp.broadcast_to(ab_re, bu_re.shape)
    a_im = jnp.broadcast_to(ab_im, bu_im.shape)
    _, _, s_re, s_im = lax.associative_scan(s5_combine, (a_re, a_im, bu_re, bu_im), axis=1)
    y = jnp.einsum('gcn,btgn->btgc', c_re.astype(f32), s_re) - jnp.einsum('gcn,btgn->btgc', c_im.astype(f32), s_im)
    y = y.reshape(B, T, GROUP_W) + d_skip.astype(f32) * u.astype(f32)
    y = jax.nn.gelu(y)
    y = y * jax.nn.sigmoid(y @ w_glu + b_glu)
    return rmsnorm(y, norm_w), s_re[:, -1], s_im[:, -1]


def conv_ffn(h, conv0, w_up, conv_w, conv_b, w_down):
    up = h @ w_up
    T = up.shape[1]
    padded = jnp.concatenate([conv0.astype(up.dtype), up], axis=1)
    conv = conv_b
    for j in range(CONV_W):
        conv = conv + conv_w[j] * padded[:, j:j + T]
    gate, val = jnp.split(conv, 2, axis=-1)
    return (jax.nn.silu(gate) * val) @ w_down, padded[:, T:]


def run_group(x, p, pos, ret0, rwkv0, shift0, s5re0, s5im0, conv0, cache_k, cache_v, page_table, W):
    k_rows, v_rows, ret_s, rwkv_s, shift_s, s5r_s, s5i_s, conv_s = [], [], [], [], [], [], [], []
    for i in range(DEPTH):
        h = rmsnorm(x, W['norm_mix'][i])
        proj = h @ W['w_in'][i]
        c_ret, c_diff, c_rwkv, c_s5 = jnp.split(proj, SPLITS, axis=-1)
        o_ret, s_ret = retention_mixer(c_ret, pos, ret0[i], W['ret_norm_w'][i], W['ret_norm_b'][i])
        lam_init = 0.8 - 0.6 * math.exp(-0.3 * i)
        lam = (jnp.exp(jnp.sum(W['diff_lq1'][i].astype(jnp.float32) * W['diff_lk1'][i].astype(jnp.float32)))
               - jnp.exp(jnp.sum(W['diff_lq2'][i].astype(jnp.float32) * W['diff_lk2'][i].astype(jnp.float32)))
               + lam_init)
        if page_table is None:
            k_past, v_past = None, None
        else:
            nb, npg = page_table.shape
            k_past = cache_k[i][page_table].reshape(nb, npg * PAGE_SIZE, DIFF_HEADS, 2, DIFF_QD)
            v_past = cache_v[i][page_table].reshape(nb, npg * PAGE_SIZE, DIFF_HEADS, DIFF_VD)
        o_diff, k_new, v_new = diff_mixer(c_diff, pos, lam, lam_init, W['diff_subln'][i], k_past, v_past)
        o_rwkv, s_rwkv, shift_new = rwkv_mixer(c_rwkv, shift0[i], rwkv0[i], W['rwkv_mu'][i], W['rwkv_w0'][i],
                                               W['rwkv_w2'][i], W['rwkv_a0'][i], W['rwkv_a2'][i], W['rwkv_g2'][i],
                                               W['rwkv_kk'][i], W['rwkv_ka'][i], W['rwkv_rk'][i],
                                               W['rwkv_ln_w'][i], W['rwkv_ln_b'][i])
        o_s5, s_re, s_im = s5_mixer(c_s5, s5re0[i], s5im0[i], W['s5_lam_re'][i], W['s5_lam_im'][i],
                                    W['s5_log_step'][i], W['s5_b_re'][i], W['s5_b_im'][i], W['s5_c_re'][i],
                                    W['s5_c_im'][i], W['s5_d'][i], W['s5_w_glu'][i], W['s5_b_glu'][i],
                                    W['s5_norm'][i])
        x = x + jnp.concatenate([o_ret, o_diff, o_rwkv, o_s5], axis=-1) @ W['w_out'][i]
        f, conv_new = conv_ffn(rmsnorm(x, W['norm_ffn'][i]), conv0[i], W['ffn_w_up'][i], W['ffn_conv_w'][i],
                               W['ffn_conv_b'][i], W['ffn_w_down'][i])
        x = x + f
        e = rmsnorm(p[i] @ W['ple_w_proj'][i], W['ple_norm_e'][i])
        x = x + e * jax.nn.sigmoid(rmsnorm(x, W['norm_ple'][i]) @ W['ple_w_gate'][i])
        k_rows.append(k_new); v_rows.append(v_new); ret_s.append(s_ret); rwkv_s.append(s_rwkv)
        shift_s.append(shift_new); s5r_s.append(s_re); s5i_s.append(s_im); conv_s.append(conv_new)
    y = rmsnorm(x, W['norm_final'])
    return (y, jnp.stack(k_rows), jnp.stack(v_rows), jnp.stack(ret_s), jnp.stack(rwkv_s), jnp.stack(shift_s),
            jnp.stack(s5r_s), jnp.stack(s5i_s), jnp.stack(conv_s))


def setup_inputs(seed: int = 0) -> dict:
    key = jax.random.key(seed)
    ks = iter(jax.random.split(key, 96))
    f32 = jnp.float32

    def nrm(shape, scale=1.0):
        return jax.random.normal(next(ks), shape, f32) * scale

    def gain(shape):
        return 1.0 + nrm(shape, 0.02)

    n_pages = PAST_LEN // PAGE_SIZE
    n_pool = (DEC_BATCH * n_pages * 5) // 4
    page_table = jax.random.permutation(next(ks), n_pool)[:DEC_BATCH * n_pages].reshape(DEC_BATCH, n_pages).astype(jnp.int32)
    frac = jnp.arange(GROUP_W, dtype=f32) / (GROUP_W - 1)
    conv_id = jnp.zeros((CONV_W, 1), f32).at[CONV_W - 1].set(1.0)
    return {
        'x_prompt': nrm((BATCH, SEQ, D_MODEL)),
        'x_sample': nrm((DEC_BATCH, DEC_SEQ, D_MODEL)),
        'p_prompt': nrm((DEPTH, BATCH, SEQ, PLE_DIM)),
        'p_sample': nrm((DEPTH, DEC_BATCH, DEC_SEQ, PLE_DIM)),
        'cache_k': nrm((DEPTH, n_pool, PAGE_SIZE, DIFF_HEADS, 2 * DIFF_QD)),
        'cache_v': nrm((DEPTH, n_pool, PAGE_SIZE, DIFF_HEADS, DIFF_VD)),
        'page_table': page_table,
        'state_ret': nrm((DEPTH, DEC_BATCH, RET_HEADS, RET_HD, RET_HD), 0.5),
        'state_rwkv': nrm((DEPTH, DEC_BATCH, RWKV_HEADS, RWKV_HD, RWKV_HD), 0.5),
        'state_rwkv_shift': nrm((DEPTH, DEC_BATCH, RWKV_COLS)),
        'state_s5_re': nrm((DEPTH, DEC_BATCH, S5_GROUPS, S5_N), 0.5),
        'state_s5_im': nrm((DEPTH, DEC_BATCH, S5_GROUPS, S5_N), 0.5),
        'state_ffn_conv': nrm((DEPTH, DEC_BATCH, CONV_W - 1, 2 * D_FF)),
        'norm_mix': gain((DEPTH, D_MODEL)),
        'w_in': nrm((DEPTH, D_MODEL, IN_COLS), D_MODEL ** -0.5),
        'w_out': nrm((DEPTH, D_MODEL, D_MODEL), D_MODEL ** -0.5),
        'ret_norm_w': gain((DEPTH, RET_HEADS, RET_HD)),
        'ret_norm_b': nrm((DEPTH, RET_HEADS, RET_HD), 0.02),
        'diff_lq1': nrm((DEPTH, DIFF_QD), 0.1),
        'diff_lk1': nrm((DEPTH, DIFF_QD), 0.1),
        'diff_lq2': nrm((DEPTH, DIFF_QD), 0.1),
        'diff_lk2': nrm((DEPTH, DIFF_QD), 0.1),
        'diff_subln': gain((DEPTH, DIFF_VD)),
        'rwkv_mu': jax.random.uniform(next(ks), (DEPTH, RWKV_COLS), f32),
        'rwkv_w0': -6.0 + 5.0 * frac ** 0.7 + nrm((DEPTH, GROUP_W), 0.1),
        'rwkv_w2': nrm((DEPTH, RWKV_DECAY_LORA, GROUP_W), 0.1),
        'rwkv_a0': nrm((DEPTH, GROUP_W), 0.1),
        'rwkv_a2': nrm((DEPTH, RWKV_A_LORA, GROUP_W), 0.1),
        'rwkv_g2': nrm((DEPTH, RWKV_GATE_LORA, GROUP_W), RWKV_GATE_LORA ** -0.5),
        'rwkv_kk': 0.85 + nrm((DEPTH, GROUP_W), 0.02),
        'rwkv_ka': gain((DEPTH, GROUP_W)),
        'rwkv_rk': nrm((DEPTH, RWKV_HEADS, RWKV_HD), 0.1),
        'rwkv_ln_w': gain((DEPTH, RWKV_HEADS, RWKV_HD)),
        'rwkv_ln_b': nrm((DEPTH, RWKV_HEADS, RWKV_HD), 0.02),
        's5_lam_re': -0.5 + nrm((DEPTH, S5_GROUPS, S5_N), 0.01),
        's5_lam_im': math.pi * jnp.arange(S5_N, dtype=f32) + nrm((DEPTH, S5_GROUPS, S5_N), 0.01),
        's5_log_step': jax.random.uniform(next(ks), (DEPTH, S5_GROUPS), f32, math.log(1e-3), math.log(1e-1)),
        's5_b_re': nrm((DEPTH, S5_GROUPS, S5_N, S5_CH), (2 * S5_CH) ** -0.5),
        's5_b_im': nrm((DEPTH, S5_GROUPS, S5_N, S5_CH), (2 * S5_CH) ** -0.5),
        's5_c_re': nrm((DEPTH, S5_GROUPS, S5_CH, S5_N), (2 * S5_N) ** -0.5),
        's5_c_im': nrm((DEPTH, S5_GROUPS, S5_CH, S5_N), (2 * S5_N) ** -0.5),
        's5_d': nrm((DEPTH, GROUP_W), 0.5),
        's5_w_glu': nrm((DEPTH, GROUP_W, GROUP_W), GROUP_W ** -0.5),
        's5_b_glu': nrm((DEPTH, GROUP_W), 0.02),
        's5_norm': gain((DEPTH, GROUP_W)),
        'norm_ffn': gain((DEPTH, D_MODEL)),
        'ffn_w_up': nrm((DEPTH, D_MODEL, 2 * D_FF), D_MODEL ** -0.5),
        'ffn_conv_w': conv_id[None] + nrm((DEPTH, CONV_W, 2 * D_FF), 0.3),
        'ffn_conv_b': nrm((DEPTH, 2 * D_FF), 0.02),
        'ffn_w_down': nrm((DEPTH, D_FF, D_MODEL), D_FF ** -0.5),
        'norm_ple': gain((DEPTH, D_MODEL)),
        'ple_w_proj': nrm((DEPTH, PLE_DIM, D_MODEL), PLE_DIM ** -0.5),
        'ple_norm_e': gain((DEPTH, D_MODEL)),
        'ple_w_gate': nrm((DEPTH, D_MODEL, D_MODEL), D_MODEL ** -0.5),
        'norm_final': gain((D_MODEL,)),
    }


def reference(x_prompt, x_sample, p_prompt, p_sample, cache_k, cache_v, page_table,
              state_ret, state_rwkv, state_rwkv_shift, state_s5_re, state_s5_im, state_ffn_conv,
              norm_mix, w_in, w_out, ret_norm_w, ret_norm_b,
              diff_lq1, diff_lk1, diff_lq2, diff_lk2, diff_subln,
              rwkv_mu, rwkv_w0, rwkv_w2, rwkv_a0, rwkv_a2, rwkv_g2, rwkv_kk, rwkv_ka, rwkv_rk, rwkv_ln_w, rwkv_ln_b,
              s5_lam_re, s5_lam_im, s5_log_step, s5_b_re, s5_b_im, s5_c_re, s5_c_im, s5_d, s5_w_glu, s5_b_glu, s5_norm,
              norm_ffn, ffn_w_up, ffn_conv_w, ffn_conv_b, ffn_w_down,
              norm_ple, ple_w_proj, ple_norm_e, ple_w_gate, norm_final):
    W = dict(norm_mix=norm_mix, w_in=w_in, w_out=w_out, ret_norm_w=ret_norm_w, ret_norm_b=ret_norm_b,
             diff_lq1=diff_lq1, diff_lk1=diff_lk1, diff_lq2=diff_lq2, diff_lk2=diff_lk2, diff_subln=diff_subln,
             rwkv_mu=rwkv_mu, rwkv_w0=rwkv_w0, rwkv_w2=rwkv_w2, rwkv_a0=rwkv_a0, rwkv_a2=rwkv_a2, rwkv_g2=rwkv_g2,
             rwkv_kk=rwkv_kk, rwkv_ka=rwkv_ka, rwkv_rk=rwkv_rk, rwkv_ln_w=rwkv_ln_w, rwkv_ln_b=rwkv_ln_b,
             s5_lam_re=s5_lam_re, s5_lam_im=s5_lam_im, s5_log_step=s5_log_step, s5_b_re=s5_b_re, s5_b_im=s5_b_im,
             s5_c_re=s5_c_re, s5_c_im=s5_c_im, s5_d=s5_d, s5_w_glu=s5_w_glu, s5_b_glu=s5_b_glu, s5_norm=s5_norm,
             norm_ffn=norm_ffn, ffn_w_up=ffn_w_up, ffn_conv_w=ffn_conv_w, ffn_conv_b=ffn_conv_b, ffn_w_down=ffn_w_down,
             norm_ple=norm_ple, ple_w_proj=ple_w_proj, ple_norm_e=ple_norm_e, ple_w_gate=ple_w_gate,
             norm_final=norm_final)
    f32 = jnp.float32
    bp, tp = x_prompt.shape[0], x_prompt.shape[1]
    pos_p = jnp.arange(tp, dtype=jnp.int32)
    past_len = page_table.shape[1] * PAGE_SIZE
    pos_s = past_len + jnp.arange(x_sample.shape[1], dtype=jnp.int32)
    ret0 = jnp.zeros((DEPTH, bp, RET_HEADS, RET_HD, RET_HD), f32)
    rwkv0 = jnp.zeros((DEPTH, bp, RWKV_HEADS, RWKV_HD, RWKV_HD), f32)
    shift0 = jnp.zeros((DEPTH, bp, RWKV_COLS), f32)
    s5r0 = jnp.zeros((DEPTH, bp, S5_GROUPS, S5_N), f32)
    s5i0 = jnp.zeros((DEPTH, bp, S5_GROUPS, S5_N), f32)
    conv0 = jnp.zeros((DEPTH, bp, CONV_W - 1, 2 * D_FF), f32)
    (y_prompt, k_prompt, v_prompt, ret_prompt, rwkv_prompt, shift_prompt,
     s5_re_prompt, s5_im_prompt, conv_prompt) = run_group(
        x_prompt, p_prompt, pos_p, ret0, rwkv0, shift0, s5r0, s5i0, conv0, None, None, None, W)
    (y_sample, k_sample, v_sample, ret_sample, rwkv_sample, shift_sample,
     s5_re_sample, s5_im_sample, conv_sample) = run_group(
        x_sample, p_sample, pos_s, state_ret, state_rwkv, state_rwkv_shift, state_s5_re, state_s5_im,
        state_ffn_conv, cache_k, cache_v, page_table, W)
    return (y_prompt, y_sample, k_prompt, v_prompt, k_sample, v_sample, ret_prompt, ret_sample,
            rwkv_prompt, rwkv_sample, shift_prompt, shift_sample, s5_re_prompt, s5_im_prompt,
            s5_re_sample, s5_im_sample, conv_prompt, conv_sample)
```

```python
import functools
import math

import jax
import jax.numpy as jnp
import numpy as np
from jax import lax
from jax.experimental import pallas as pl
from jax.experimental.pallas import tpu as pltpu

F32 = jnp.float32
BF16 = jnp.bfloat16

D_MODEL = 2048
GROUP_W = 512
RET_HEADS = 4
RET_HD = 128
RET_CHUNK = 128
RET_ROPE_BASE = 10000.0
DIFF_HEADS = 4
DIFF_VD = 128
DIFF_QD = 64
ROPE_THETA = 500000.0
ROPE_DIM = 16
PAGE_SIZE = 128
RWKV_HD = 64
RWKV_HEADS = 8
RWKV_LN_EPS = 64e-5
S5_CH = 16
S5_GROUPS = 32
S5_N = 64
D_FF = 5632
EPS = 1e-6
NEG_INF = -1e30

RET_COLS = 4 * GROUP_W
DIFF_COLS = 3 * GROUP_W
RWKV_COLS = 3 * GROUP_W + 64 + 64 + 128
S5_COLS = GROUP_W

LANES = 128
SUBLANES = 8
VMEM_LIMIT_BYTES = 52 * 1024 * 1024


def _cparams(sem, vmem=VMEM_LIMIT_BYTES):
    return pltpu.CompilerParams(dimension_semantics=sem, vmem_limit_bytes=vmem)


def _split_bf16(x):
    hi = x.astype(BF16)
    lo = (x - hi.astype(F32)).astype(BF16)
    return hi, lo


def _dot_hilo(x, w_bf16):
    hi, lo = _split_bf16(x)
    return (jnp.dot(hi, w_bf16, preferred_element_type=F32)
            + jnp.dot(lo, w_bf16, preferred_element_type=F32))


def _dot3(x, w_hi, w_lo):
    hi, lo = _split_bf16(x)
    return (jnp.dot(hi, w_hi, preferred_element_type=F32)
            + jnp.dot(hi, w_lo, preferred_element_type=F32)
            + jnp.dot(lo, w_hi, preferred_element_type=F32))


def _rmsnorm_kernel(x_ref, g_ref, o_ref):
    x = x_ref[...]
    y = x * lax.rsqrt(jnp.mean(x * x, axis=-1, keepdims=True) + EPS)
    o_ref[...] = (y * g_ref[...]).astype(o_ref.dtype)


def rmsnorm_rows(x, g, out_dtype, tm=512):
    m, d = x.shape
    return pl.pallas_call(
        _rmsnorm_kernel,
        grid=(m // tm,),
        in_specs=[pl.BlockSpec((tm, d), lambda i: (i, 0)),
                  pl.BlockSpec((1, d), lambda i: (0, 0))],
        out_specs=pl.BlockSpec((tm, d), lambda i: (i, 0)),
        out_shape=jax.ShapeDtypeStruct((m, d), out_dtype),
        compiler_params=_cparams(("parallel",)),
        name="rmsnorm_rows",
    )(x, g.reshape(1, d))


def _mm_kernel(x_ref, w_ref, *rest, epilogue, n_extra):
    extra = rest[:n_extra]
    o_ref, wb_ref = rest[n_extra], rest[n_extra + 1]

    @pl.when(pl.program_id(1) == 0)
    def _():
        wb_ref[...] = w_ref[...].astype(BF16)

    acc = jnp.dot(x_ref[...].astype(BF16), wb_ref[...], preferred_element_type=F32)
    if epilogue is not None:
        acc = epilogue(acc, *[e[...] for e in extra])
    o_ref[...] = acc.astype(o_ref.dtype)


def matmul(x, w, *, tn, tm=512, col0=0, ncols=None, epilogue=None, tiles=(), rows=(),
           out_dtype=F32, name="matmul"):
    m, k = x.shape
    n = w.shape[1] - col0 if ncols is None else ncols
    assert col0 % tn == 0 and n % tn == 0 and m % tm == 0
    joff = col0 // tn
    in_specs = [pl.BlockSpec((tm, k), lambda j, i: (i, 0)),
                pl.BlockSpec((k, tn), lambda j, i: (0, j + joff))]
    in_specs += [pl.BlockSpec((tm, tn), lambda j, i: (i, j)) for _ in tiles]
    in_specs += [pl.BlockSpec((1, tn), lambda j, i: (0, j)) for _ in rows]
    kern = functools.partial(_mm_kernel, epilogue=epilogue, n_extra=len(tiles) + len(rows))
    return pl.pallas_call(
        kern,
        grid=(n // tn, m // tm),
        in_specs=in_specs,
        out_specs=pl.BlockSpec((tm, tn), lambda j, i: (i, j)),
        out_shape=jax.ShapeDtypeStruct((m, n), out_dtype),
        scratch_shapes=[pltpu.VMEM((k, tn), BF16)],
        compiler_params=_cparams(("parallel", "arbitrary")),
        name=name,
    )(x, w, *tiles, *[r.reshape(1, -1) for r in rows])


def _ret_kernel(q_ref, k_ref, v_ref, g_ref, cos_ref, sin_ref, dmask_ref, qdec_ref, kdec_ref,
                cdec_ref, nw_ref, nb_ref, s0_ref, o_ref, sn_ref, s_scr, *, chunk, n_chunks):
    c = pl.program_id(1)

    @pl.when(c == 0)
    def _():
        s_scr[...] = s0_ref[...]

    cos = cos_ref[...]
    sin = sin_ref[...]
    for h in range(RET_HEADS):
        hs = slice(h * RET_HD, (h + 1) * RET_HD)
        qh = q_ref[:, hs]
        kh = k_ref[:, hs]
        qr = qh * cos + pltpu.roll(qh, RET_HD // 2, axis=1) * sin
        kr = (kh * cos + pltpu.roll(kh, RET_HD // 2, axis=1) * sin) * (RET_HD ** -0.5)
        vh = v_ref[:, hs]
        s = s_scr[h]
        outs = []
        for ci in range(n_chunks):
            rs = slice(ci * chunk, (ci + 1) * chunk)
            qc, kc, vc = qr[rs], kr[rs], vh[rs].astype(BF16)
            att = lax.dot_general(qc.astype(BF16), kc.astype(BF16), (((1,), (1,)), ((), ())),
                                  preferred_element_type=F32) * dmask_ref[h]
            o = jnp.dot(att.astype(BF16), vc, preferred_element_type=F32)
            o += jnp.dot((qc * qdec_ref[h]).astype(BF16), s.astype(BF16), preferred_element_type=F32)
            kd = (kc * kdec_ref[h]).astype(BF16)
            s = s * cdec_ref[h, 0:1, :] + lax.dot_general(kd, vc, (((0,), (0,)), ((), ())),
                                                  preferred_element_type=F32)
            outs.append(o)
        s_scr[h] = s
        o = outs[0] if n_chunks == 1 else jnp.concatenate(outs, axis=0)
        mu = jnp.mean(o, axis=-1, keepdims=True)
        var = jnp.mean(jnp.square(o - mu), axis=-1, keepdims=True)
        o = (o - mu) * lax.rsqrt(var + EPS) * nw_ref[h:h + 1, :] + nb_ref[h:h + 1, :]
        gh = g_ref[:, hs]
        o_ref[:, hs] = (o * (gh * jax.nn.sigmoid(gh))).astype(o_ref.dtype)

    @pl.when(c == pl.num_programs(1) - 1)
    def _():
        sn_ref[...] = s_scr[...]


def retention(proj, row0, n_seq, t_seq, cos, sin, s0, norm_w, norm_b, *, chunk, chunks_per_step):
    rb = chunk * chunks_per_step
    steps = t_seq // rb
    assert row0 % rb == 0 and t_seq % rb == 0
    b0 = row0 // rb
    log_g = jnp.log1p(-jnp.exp2(-5.0 - jnp.arange(RET_HEADS, dtype=F32)))
    idx = jnp.arange(chunk, dtype=F32)
    rel = idx[:, None] - idx[None, :]
    dmask = jnp.where(rel >= 0, jnp.exp(log_g[:, None, None] * jnp.maximum(rel, 0.0)), 0.0)
    ones = jnp.ones((1, 1, RET_HD), F32)
    qdec = jnp.exp(log_g[:, None] * (idx + 1.0))[:, :, None] * ones
    kdec = jnp.exp(log_g[:, None] * (chunk - 1.0 - idx))[:, :, None] * ones
    cdec = jnp.exp(log_g * chunk)[:, None, None] * jnp.ones((1, SUBLANES, RET_HD), F32)

    def col(j):
        return pl.BlockSpec((rb, GROUP_W), lambda s, c: (b0 + s * steps + c, j))

    full = lambda shape: pl.BlockSpec(shape, lambda s, c: (0,) * len(shape))
    kern = functools.partial(_ret_kernel, chunk=chunk, n_chunks=chunks_per_step)
    return pl.pallas_call(
        kern,
        grid=(n_seq, steps),
        in_specs=[col(0), col(1), col(2), col(3),
                  pl.BlockSpec((rb, RET_HD), lambda s, c: (c, 0)),
                  pl.BlockSpec((rb, RET_HD), lambda s, c: (c, 0)),
                  full((RET_HEADS, chunk, chunk)),
                  full((RET_HEADS, chunk, RET_HD)),
                  full((RET_HEADS, chunk, RET_HD)),
                  full((RET_HEADS, SUBLANES, RET_HD)),
                  full((RET_HEADS, RET_HD)),
                  full((RET_HEADS, RET_HD)),
                  pl.BlockSpec((None, RET_HEADS, RET_HD, RET_HD), lambda s, c: (s, 0, 0, 0))],
        out_specs=[pl.BlockSpec((rb, GROUP_W), lambda s, c: (s * steps + c, 0)),
                   pl.BlockSpec((None, RET_HEADS, RET_HD, RET_HD), lambda s, c: (s, 0, 0, 0))],
        out_shape=[jax.ShapeDtypeStruct((n_seq * t_seq, GROUP_W), BF16),
                   jax.ShapeDtypeStruct((n_seq, RET_HEADS, RET_HD, RET_HD), F32)],
        scratch_shapes=[pltpu.VMEM((RET_HEADS, RET_HD, RET_HD), F32)],
        compiler_params=_cparams(("arbitrary", "arbitrary")),
        name="retention",
    )(proj, proj, proj, proj, cos, sin, dmask, qdec, kdec, cdec, norm_w, norm_b, s0)


def _diff_prep_kernel(q_ref, k_ref, v_ref, c_ref, s1_ref, s2_ref, q0_ref, q1_ref, kn_ref, kb_ref, vb_ref):
    c, s1, s2 = c_ref[...], s1_ref[...], s2_ref[...]
    lane = lax.broadcasted_iota(jnp.int32, c.shape, 1)
    lo = lane < DIFF_QD
    for j in range(GROUP_W // LANES):
        cs = slice(j * LANES, (j + 1) * LANES)
        q = q_ref[:, cs]
        k = k_ref[:, cs]
        qr = q * c + pltpu.roll(q, LANES - ROPE_DIM // 2, axis=1) * s1 + pltpu.roll(q, ROPE_DIM // 2, axis=1) * s2
        kr = k * c + pltpu.roll(k, LANES - ROPE_DIM // 2, axis=1) * s1 + pltpu.roll(k, ROPE_DIM // 2, axis=1) * s2
        qr = qr * (DIFF_QD ** -0.5)
        q0_ref[:, cs] = jnp.where(lo, qr, 0.0).astype(BF16)
        q1_ref[:, cs] = jnp.where(lo, 0.0, qr).astype(BF16)
        kn_ref[:, cs] = kr
        kb_ref[:, cs] = kr.astype(BF16)
    vb_ref[...] = v_ref[...].astype(BF16)


def diff_prep(proj, c, s1, s2, tm=512):
    m = proj.shape[0]
    col = lambda j: pl.BlockSpec((tm, GROUP_W), lambda i: (i, j))
    tab = pl.BlockSpec((tm, LANES), lambda i: (i, 0))
    out = pl.BlockSpec((tm, GROUP_W), lambda i: (i, 0))
    return pl.pallas_call(
        _diff_prep_kernel,
        grid=(m // tm,),
        in_specs=[col(0), col(1), col(2), tab, tab, tab],
        out_specs=[out] * 5,
        out_shape=[jax.ShapeDtypeStruct((m, GROUP_W), BF16), jax.ShapeDtypeStruct((m, GROUP_W), BF16),
                   jax.ShapeDtypeStruct((m, GROUP_W), F32), jax.ShapeDtypeStruct((m, GROUP_W), BF16),
                   jax.ShapeDtypeStruct((m, GROUP_W), BF16)],
        compiler_params=_cparams(("parallel",)),
        name="diff_prep",
    )(proj, proj, proj, c, s1, s2)


def _diff_flash_kernel(qi_ref, ki_ref, lam_ref, q0_ref, q1_ref, k_ref, v_ref, sub_ref, o_ref,
                       m0, l0, a0, m1, l1, a1, *, bq, bk, out_scale):
    step = pl.program_id(1)
    qi = qi_ref[step]
    ki = ki_ref[step]

    @pl.when(ki == 0)
    def _():
        m0[...] = jnp.full(m0.shape, NEG_INF, F32)
        m1[...] = jnp.full(m1.shape, NEG_INF, F32)
        l0[...] = jnp.zeros(l0.shape, F32)
        l1[...] = jnp.zeros(l1.shape, F32)
        a0[...] = jnp.zeros(a0.shape, F32)
        a1[...] = jnp.zeros(a1.shape, F32)

    k = k_ref[...]
    v = v_ref[...]
    row = qi * bq + lax.broadcasted_iota(jnp.int32, (bq, bk), 0)
    colp = ki * bk + lax.broadcasted_iota(jnp.int32, (bq, bk), 1)
    visible = colp <= row

    def update(q_ref, m_ref, l_ref, a_ref):
        s = lax.dot_general(q_ref[...], k, (((1,), (1,)), ((), ())), preferred_element_type=F32)
        s = jnp.where(visible, s, NEG_INF)
        m_prev = m_ref[...]
        m_new = jnp.maximum(m_prev, jnp.max(s, axis=1, keepdims=True))
        alpha = jnp.exp(m_prev - m_new)
        p = jnp.exp(s - m_new)
        l_ref[...] = alpha * l_ref[...] + jnp.sum(p, axis=1, keepdims=True)
        a_ref[...] = alpha * a_ref[...] + jnp.dot(p.astype(BF16), v, preferred_element_type=F32)
        m_ref[...] = m_new

    update(q0_ref, m0, l0, a0)
    update(q1_ref, m1, l1, a1)

    @pl.when(ki == qi)
    def _():
        o = a0[...] / l0[...] - lam_ref[0, 0] * (a1[...] / l1[...])
        y = o * lax.rsqrt(jnp.mean(o * o, axis=-1, keepdims=True) + EPS)
        o_ref[...] = (y * sub_ref[...] * out_scale).astype(o_ref.dtype)


def diff_attention_prompt(q0, q1, kb, vb, lam, subln, lam_init, *, blk=512):
    t = q0.shape[0]
    nb = t // blk
    qi_tbl = np.concatenate([np.full(i + 1, i, np.int32) for i in range(nb)])
    ki_tbl = np.concatenate([np.arange(i + 1, dtype=np.int32) for i in range(nb)])
    kern = functools.partial(_diff_flash_kernel, bq=blk, bk=blk, out_scale=1.0 - lam_init)
    grid_spec = pltpu.PrefetchScalarGridSpec(
        num_scalar_prefetch=2,
        grid=(DIFF_HEADS, len(qi_tbl)),
        in_specs=[pl.BlockSpec(memory_space=pltpu.SMEM),
                  pl.BlockSpec((blk, LANES), lambda h, s, qi, ki: (qi[s], h)),
                  pl.BlockSpec((blk, LANES), lambda h, s, qi, ki: (qi[s], h)),
                  pl.BlockSpec((blk, LANES), lambda h, s, qi, ki: (ki[s], h)),
                  pl.BlockSpec((blk, LANES), lambda h, s, qi, ki: (ki[s], h)),
                  pl.BlockSpec((1, LANES), lambda h, s, qi, ki: (0, 0))],
        out_specs=pl.BlockSpec((blk, LANES), lambda h, s, qi, ki: (qi[s], h)),
        scratch_shapes=[pltpu.VMEM((blk, 1), F32), pltpu.VMEM((blk, 1), F32), pltpu.VMEM((blk, DIFF_VD), F32),
                        pltpu.VMEM((blk, 1), F32), pltpu.VMEM((blk, 1), F32), pltpu.VMEM((blk, DIFF_VD), F32)],
    )
    return pl.pallas_call(
        kern,
        grid_spec=grid_spec,
        out_shape=jax.ShapeDtypeStruct((t, GROUP_W), BF16),
        compiler_params=_cparams(("arbitrary", "arbitrary")),
        name="diff_attention_prompt",
    )(jnp.asarray(qi_tbl), jnp.asarray(ki_tbl), lam.reshape(1, 1), q0, q1, kb, vb, subln.reshape(1, LANES))


def _diff_paged_kernel(pt_ref, lam_ref, qbd_ref, kn_ref, vn_ref, sub_ref, *rest, n_pages, t_new, out_scale):
    k_pages = rest[:n_pages]
    v_pages = rest[n_pages:2 * n_pages]
    o_ref = rest[2 * n_pages]
    kpad, vpad, s_scr = rest[2 * n_pages + 1:]
    half = DIFF_HEADS * t_new
    b = pl.program_id(0)

    @pl.when(b == 0)
    def _():
        kpad[...] = jnp.zeros(kpad.shape, kpad.dtype)
        vpad[...] = jnp.zeros(vpad.shape, vpad.dtype)

    kpad[0:t_new, :] = kn_ref[...]
    vpad[0:t_new, :] = vn_ref[...]
    qbd = qbd_ref[...]
    for p in range(n_pages):
        s_scr[p] = jnp.dot(k_pages[p][...].astype(BF16), qbd, preferred_element_type=F32)
    s_new = jnp.dot(kpad[...].astype(BF16), qbd, preferred_element_type=F32)
    key = lax.broadcasted_iota(jnp.int32, (PAGE_SIZE, LANES), 0)
    qpos = lax.broadcasted_iota(jnp.int32, (PAGE_SIZE, LANES), 1) % t_new
    s_new = jnp.where(key <= qpos, s_new, NEG_INF)
    s_scr[n_pages] = s_new

    s_all = s_scr[...]
    m = jnp.max(jnp.max(s_all, axis=0), axis=0, keepdims=True)
    p_all = jnp.exp(s_all - m[None])
    l = jnp.sum(jnp.sum(p_all, axis=0), axis=0, keepdims=True)
    inv = (1.0 / l)[None]
    lam = lam_ref[0, 0]
    acc = jnp.zeros((LANES, GROUP_W), F32)
    for p in range(n_pages + 1):
        pn = p_all[p] * inv[0]
        pd = pn - lam * pltpu.roll(pn, LANES - half, axis=1)
        vsrc = v_pages[p][...] if p < n_pages else vpad[...]
        acc += lax.dot_general(pd.astype(BF16), vsrc.astype(BF16), (((0,), (0,)), ((), ())),
                               preferred_element_type=F32)
    outs = []
    for h in range(DIFF_HEADS):
        o = acc[h * t_new:(h + 1) * t_new, h * DIFF_VD:(h + 1) * DIFF_VD]
        y = o * lax.rsqrt(jnp.mean(o * o, axis=-1, keepdims=True) + EPS)
        outs.append(y * sub_ref[...] * out_scale)
    o_ref[...] = jnp.concatenate(outs, axis=1).astype(o_ref.dtype)


def diff_attention_sample(qs, kn, vn, cache_k, cache_v, layer, page_table, lam, subln, lam_init):
    nb, t_new, _ = qs.shape
    n_pages = page_table.shape[1]
    qt = jnp.transpose(qs, (0, 2, 1))
    cblk = jnp.arange(GROUP_W) // DIFF_QD
    col_h = jnp.arange(DIFF_HEADS)
    parts = []
    for mp in range(2):
        sel = (cblk[:, None] == (2 * col_h + mp)[None, :]).astype(F32)
        parts.append((qt[:, :, None, :] * sel[None, :, :, None]).reshape(nb, GROUP_W, DIFF_HEADS * t_new))
    pad = jnp.zeros((nb, GROUP_W, LANES - 2 * DIFF_HEADS * t_new), F32)
    qbd = jnp.concatenate(parts + [pad], axis=-1).astype(BF16)

    def page_spec(p):
        return pl.BlockSpec((None, None, PAGE_SIZE, GROUP_W), lambda b, pt: (layer, pt[b, p], 0, 0))

    kern = functools.partial(_diff_paged_kernel, n_pages=n_pages, t_new=t_new, out_scale=1.0 - lam_init)
    grid_spec = pltpu.PrefetchScalarGridSpec(
        num_scalar_prefetch=1,
        grid=(nb,),
        in_specs=[pl.BlockSpec(memory_space=pltpu.SMEM),
                  pl.BlockSpec((None, GROUP_W, LANES), lambda b, pt: (b, 0, 0)),
                  pl.BlockSpec((t_new, GROUP_W), lambda b, pt: (b, 0)),
                  pl.BlockSpec((t_new, GROUP_W), lambda b, pt: (b, 0)),
                  pl.BlockSpec((1, LANES), lambda b, pt: (0, 0))]
                 + [page_spec(p) for p in range(n_pages)] * 2,
        out_specs=pl.BlockSpec((t_new, GROUP_W), lambda b, pt: (b, 0)),
        scratch_shapes=[pltpu.VMEM((PAGE_SIZE, GROUP_W), F32), pltpu.VMEM((PAGE_SIZE, GROUP_W), F32),
                        pltpu.VMEM((n_pages + 1, PAGE_SIZE, LANES), F32)],
    )
    return pl.pallas_call(
        kern,
        grid_spec=grid_spec,
        out_shape=jax.ShapeDtypeStruct((nb * t_new, GROUP_W), BF16),
        compiler_params=_cparams(("arbitrary",)),
        name="diff_attention_sample",
    )(page_table, lam.reshape(1, 1), qbd, kn, vn, subln.reshape(1, LANES),
      *([cache_k] * n_pages), *([cache_v] * n_pages))


def _interleave64(x, y):
    lane = lax.broadcasted_iota(jnp.int32, (x.shape[0], LANES), 1)
    lo = lane < RWKV_HD
    blocks = []
    for c in range(GROUP_W // LANES):
        xc = x[:, c * LANES:(c + 1) * LANES]
        yc = y[:, c * LANES:(c + 1) * LANES]
        rx = pltpu.roll(xc, RWKV_HD, axis=1)
        ry = pltpu.roll(yc, RWKV_HD, axis=1)
        blocks.append(jnp.where(lo, xc, ry))
        blocks.append(jnp.where(lo, rx, yc))
    return jnp.concatenate(blocks, axis=1)


def _rwkv_prep_kernel(cols_ref, prev_ref, mu_ref, w0_ref, a0_ref, kk_ref, ka_ref, rk_ref,
                      w2h_ref, w2l_ref, a2h_ref, a2l_ref, g2h_ref, g2l_ref, e_ref,
                      wb_ref, ka_out_ref, wr_ref, v_ref, br_ref, kr_ref, bon_ref, g_ref):
    cols = cols_ref[...]
    xm = cols + (prev_ref[...] - cols) * mu_ref[...]
    o1 = GROUP_W
    r, k, v = xm[:, 0:o1], xm[:, o1:2 * o1], xm[:, 2 * o1:3 * o1]
    lora = xm[:, 3 * o1:3 * o1 + LANES]
    gl = xm[:, 3 * o1 + LANES:3 * o1 + 2 * LANES]
    wterm = _dot3(jnp.tanh(lora), w2h_ref[...], w2l_ref[...])
    aterm = _dot3(lora, a2h_ref[...], a2l_ref[...])
    z = -(w0_ref[...] + wterm)
    softplus = jnp.maximum(z, 0.0) + jnp.log1p(jnp.exp(-jnp.abs(z)))
    w = -softplus - 0.5
    decay = jnp.exp(-jnp.exp(w))
    a = jax.nn.sigmoid(a0_ref[...] + aterm)
    g = _dot3(jax.nn.sigmoid(gl), g2h_ref[...], g2l_ref[...])
    e = e_ref[...]
    kk = k * kk_ref[...]
    kk = kk * lax.rsqrt(jnp.maximum(_dot_hilo(kk * kk, e), 1e-24))
    k2 = k * (1.0 + (a - 1.0) * ka_ref[...])
    bv = kk * a
    wb_ref[...] = _interleave64(decay, bv)
    ka_out_ref[...] = _interleave64(k2, -kk)
    wr = decay * r
    wr_ref[...] = _interleave64(wr, wr)
    v_ref[...] = v
    br_ref[...] = _dot_hilo(bv * r, e)
    kr_ref[...] = _dot_hilo(k2 * r, e)
    bon_ref[...] = _dot_hilo(r * k2 * rk_ref[...], e)
    g_ref[...] = g


def rwkv_prep(cols, prev, p, e512, tm=256):
    m = cols.shape[0]
    row = lambda n: pl.BlockSpec((1, n), lambda i: (0, 0))
    full = lambda a: pl.BlockSpec(a.shape, lambda i: (0, 0))
    wide = pl.BlockSpec((tm, 2 * GROUP_W), lambda i: (i, 0))
    nar = pl.BlockSpec((tm, GROUP_W), lambda i: (i, 0))
    z64 = jnp.zeros((64, GROUP_W), F32)
    w2p = jnp.concatenate([p["w2"], z64], axis=0)
    a2p = jnp.concatenate([z64, p["a2"]], axis=0)
    w2h, w2l = _split_bf16(w2p)
    a2h, a2l = _split_bf16(a2p)
    g2h, g2l = _split_bf16(p["g2"])
    mats = [w2h, w2l, a2h, a2l, g2h, g2l, e512]
    return pl.pallas_call(
        _rwkv_prep_kernel,
        grid=(m // tm,),
        in_specs=[pl.BlockSpec((tm, RWKV_COLS), lambda i: (i, 0)),
                  pl.BlockSpec((tm, RWKV_COLS), lambda i: (i, 0)),
                  row(RWKV_COLS), row(GROUP_W), row(GROUP_W), row(GROUP_W), row(GROUP_W), row(GROUP_W)]
                 + [full(a) for a in mats],
        out_specs=[wide, wide, wide, nar, nar, nar, nar, nar],
        out_shape=[jax.ShapeDtypeStruct((m, 2 * GROUP_W), F32)] * 3
                  + [jax.ShapeDtypeStruct((m, GROUP_W), F32)] * 5,
        compiler_params=_cparams(("parallel",)),
        name="rwkv_prep",
    )(cols, prev, p["mu"].reshape(1, -1), p["w0"].reshape(1, -1), p["a0"].reshape(1, -1),
      p["kk"].reshape(1, -1), p["ka"].reshape(1, -1), p["rk"].reshape(1, -1), *mats)


def _rwkv_scan_kernel(wb_ref, ka_ref, wr_ref, v_ref, br_ref, kr_ref, bon_ref, g_ref, lnw_ref, lnb_ref,
                      e_ref, s0_ref, o_ref, sn_ref, s_scr, y_scr, *, n_seq, t_len):
    c = pl.program_id(2)
    half = RWKV_HD

    def col_forms(tile, k):
        top = jnp.broadcast_to(tile[k:k + 1, 0:LANES], (half, LANES))
        bot = jnp.broadcast_to(tile[k:k + 1, LANES:2 * LANES], (half, LANES))
        t = jnp.concatenate([top, bot], axis=0).T
        return t[0:half], t[half:2 * half]

    def tokens8(g, s):
        r0 = pl.multiple_of(g * SUBLANES, SUBLANES)
        rows = pl.ds(r0, SUBLANES)
        wb8, ka8, wr8 = wb_ref[rows, :], ka_ref[rows, :], wr_ref[rows, :]
        v8, br8, kr8 = v_ref[rows, :], br_ref[rows, :], kr_ref[rows, :]
        ys = []
        for k in range(SUBLANES):
            wc, bc = col_forms(wb8, k)
            kc, ac = col_forms(ka8, k)
            wrc, _ = col_forms(wr8, k)
            sa = jnp.sum(s * ac, axis=0, keepdims=True)
            yp = jnp.sum(s * wrc, axis=0, keepdims=True)
            vrow = v8[k:k + 1, :]
            ys.append(yp + sa * br8[k:k + 1, :] + vrow * kr8[k:k + 1, :])
            s = s * wc + bc * sa + kc * vrow
        y_scr[rows, :] = jnp.concatenate(ys, axis=0)
        return s

    @pl.when(c == 0)
    def _():
        s_scr[...] = s0_ref[...]

    groups = t_len // SUBLANES

    def seq(si, carry):
        s = s_scr[si]
        s = lax.fori_loop(0, groups, lambda g, st: tokens8(si * groups + g, st), s)
        s_scr[si] = s
        return carry

    lax.fori_loop(0, n_seq, seq, 0)

    e = e_ref[...]
    y = y_scr[...]
    mu = _dot_hilo(y, e) * (1.0 / RWKV_HD)
    d = y - mu
    var = _dot_hilo(d * d, e) * (1.0 / RWKV_HD)
    yn = d * lax.rsqrt(var + RWKV_LN_EPS) * lnw_ref[...] + lnb_ref[...]
    o_ref[...] = ((yn + bon_ref[...] * v_ref[...]) * g_ref[...]).astype(o_ref.dtype)

    @pl.when(c == pl.num_programs(2) - 1)
    def _():
        sn_ref[...] = s_scr[...]


def rwkv_scan(wb, ka, wr, v, br, kr, bon, g, ln_w, ln_b, e128, s0, *, t_seq, seq_per_step, t_step):
    n_seq = s0.shape[0]
    chunks = t_seq // t_step
    rb = seq_per_step * t_step
    assert seq_per_step == 1 or chunks == 1
    n_pairs = RWKV_HEADS // 2

    def rows(width):
        return pl.BlockSpec((rb, width), lambda p, s, c: (s * chunks + c, p))

    kern = functools.partial(_rwkv_scan_kernel, n_seq=seq_per_step, t_len=t_step)
    st = pl.BlockSpec((seq_per_step, None, RWKV_HD, LANES), lambda p, s, c: (s, p, 0, 0))
    return pl.pallas_call(
        kern,
        grid=(n_pairs, n_seq // seq_per_step, chunks),
        in_specs=[rows(2 * LANES), rows(2 * LANES), rows(2 * LANES),
                  rows(LANES), rows(LANES), rows(LANES), rows(LANES), rows(LANES),
                  pl.BlockSpec((1, LANES), lambda p, s, c: (0, p)),
                  pl.BlockSpec((1, LANES), lambda p, s, c: (0, p)),
                  pl.BlockSpec((LANES, LANES), lambda p, s, c: (0, 0)),
                  st],
        out_specs=[rows(LANES), st],
        out_shape=[jax.ShapeDtypeStruct((n_seq * t_seq, GROUP_W), BF16),
                   jax.ShapeDtypeStruct(s0.shape, F32)],
        scratch_shapes=[pltpu.VMEM((seq_per_step, RWKV_HD, LANES), F32),
                        pltpu.VMEM((rb, LANES), F32)],
        compiler_params=_cparams(("arbitrary", "arbitrary", "arbitrary")),
        name="rwkv_scan",
    )(wb, ka, wr, v, br, kr, bon, g, ln_w.reshape(1, -1), ln_b.reshape(1, -1), e128, s0)


def _s5_kernel(u_ref, bb_ref, cc_ref, pw_ref, ad_ref, d_ref, wg_ref, bg_ref, nw_ref, sr0_ref, si0_ref,
               o_ref, srn_ref, sin_ref, bur, bui, cr, ci, wgb, *, tm, per_group_state):
    i = pl.program_id(0)
    nch = S5_GROUPS * S5_N
    ngrp = tm // SUBLANES

    @pl.when(i == 0)
    def _():
        wgb[...] = wg_ref[...].astype(BF16)
        if not per_group_state:
            cr[...] = sr0_ref[...]
            ci[...] = si0_ref[...]

    u = u_ref[...]
    bu = jnp.dot(u.astype(BF16), bb_ref[...], preferred_element_type=F32)
    bur[...] = bu[:, 0:nch]
    bui[...] = bu[:, nch:2 * nch]
    sub = lax.broadcasted_iota(jnp.int32, (SUBLANES, nch), 0)

    def group(r0, c_r, c_i):
        xr = bur[pl.ds(r0, SUBLANES), :]
        xi = bui[pl.ds(r0, SUBLANES), :]
        for di, dsh in enumerate((1, 2, 4)):
            ar = ad_ref[2 * di:2 * di + 1, :]
            ai = ad_ref[2 * di + 1:2 * di + 2, :]
            keep = sub >= dsh
            sr = jnp.where(keep, pltpu.roll(xr, dsh, axis=0), 0.0)
            si = jnp.where(keep, pltpu.roll(xi, dsh, axis=0), 0.0)
            xr, xi = xr + ar * sr - ai * si, xi + ar * si + ai * sr
        pr = pw_ref[0:SUBLANES, :]
        pi = pw_ref[SUBLANES:2 * SUBLANES, :]
        xr, xi = xr + pr * c_r - pi * c_i, xi + pr * c_i + pi * c_r
        bur[pl.ds(r0, SUBLANES), :] = xr
        bui[pl.ds(r0, SUBLANES), :] = xi
        return xr[SUBLANES - 1:SUBLANES, :], xi[SUBLANES - 1:SUBLANES, :]

    def block(bi, carry):
        g0 = pl.multiple_of(bi * SUBLANES, SUBLANES)
        if per_group_state:
            st_r = sr0_ref[pl.ds(g0, SUBLANES), :]
            st_i = si0_ref[pl.ds(g0, SUBLANES), :]
            lasts_r, lasts_i = [], []
        else:
            c_r, c_i = cr[...], ci[...]
        for k in range(SUBLANES):
            r0 = pl.multiple_of((g0 + k) * SUBLANES, SUBLANES)
            if per_group_state:
                l_r, l_i = group(r0, st_r[k:k + 1, :], st_i[k:k + 1, :])
                lasts_r.append(l_r)
                lasts_i.append(l_i)
            else:
                c_r, c_i = group(r0, c_r, c_i)
        if per_group_state:
            srn_ref[pl.ds(g0, SUBLANES), :] = jnp.concatenate(lasts_r, axis=0)
            sin_ref[pl.ds(g0, SUBLANES), :] = jnp.concatenate(lasts_i, axis=0)
        else:
            cr[...] = c_r
            ci[...] = c_i
        return carry

    lax.fori_loop(0, ngrp // SUBLANES, block, 0)

    if not per_group_state:
        srn_ref[...] = cr[...]
        sin_ref[...] = ci[...]

    ccv = cc_ref[...]
    y = (jnp.dot(bur[...].astype(BF16), ccv[0:nch], preferred_element_type=F32)
         + jnp.dot(bui[...].astype(BF16), ccv[nch:2 * nch], preferred_element_type=F32))
    y = y + d_ref[...] * u
    y = 0.5 * y * (1.0 + jnp.tanh(math.sqrt(2.0 / math.pi) * (y + 0.044715 * (y * y * y))))
    gate = jnp.dot(y.astype(BF16), wgb[...], preferred_element_type=F32) + bg_ref[...]
    y = y * jax.nn.sigmoid(gate)
    y = y * lax.rsqrt(jnp.mean(y * y, axis=-1, keepdims=True) + EPS)
    o_ref[...] = (y * nw_ref[...]).astype(o_ref.dtype)


def s5_mixer(u, p, sr0, si0, *, per_group_state, tm=256):
    m = u.shape[0]
    nch = S5_GROUPS * S5_N
    lr, li = p["lam_re"], p["lam_im"]
    dt = jnp.exp(p["log_step"])[:, None]
    mag = jnp.exp(lr * dt)
    ab_re, ab_im = mag * jnp.cos(li * dt), mag * jnp.sin(li * dt)
    den = lr * lr + li * li
    cf_re = ((ab_re - 1.0) * lr + ab_im * li) / den
    cf_im = (ab_im * lr - (ab_re - 1.0) * li) / den
    bb_re = cf_re[..., None] * p["b_re"] - cf_im[..., None] * p["b_im"]
    bb_im = cf_re[..., None] * p["b_im"] + cf_im[..., None] * p["b_re"]
    eye = jnp.eye(S5_GROUPS, dtype=F32)

    def bd_in(b):
        return jnp.einsum("gnc,gh->gchn", b, eye).reshape(GROUP_W, nch)

    def bd_out(cm):
        return jnp.einsum("gcn,gh->gnhc", cm, eye).reshape(nch, GROUP_W)

    bb = jnp.concatenate([bd_in(bb_re), bd_in(bb_im)], axis=1).astype(BF16)
    cc = jnp.concatenate([bd_out(p["c_re"]), -bd_out(p["c_im"])], axis=0).astype(BF16)
    ar, ai = ab_re.reshape(1, nch), ab_im.reshape(1, nch)
    pows = [(ar, ai)]
    for _ in range(SUBLANES - 1):
        pr, pi = pows[-1]
        pows.append((pr * ar - pi * ai, pr * ai + pi * ar))
    pw = jnp.concatenate([jnp.concatenate([q[0] for q in pows], axis=0),
                          jnp.concatenate([q[1] for q in pows], axis=0)], axis=0)
    ad = jnp.concatenate([pows[0][0], pows[0][1], pows[1][0], pows[1][1], pows[3][0], pows[3][1],
                          jnp.zeros((2, nch), F32)], axis=0)
    n_state = sr0.shape[0]
    full = lambda a: pl.BlockSpec(a.shape, lambda i: (0,) * a.ndim)
    row = lambda n: pl.BlockSpec((1, n), lambda i: (0, 0))
    if per_group_state:
        st = pl.BlockSpec((tm // SUBLANES, nch), lambda i: (i, 0))
    else:
        st = pl.BlockSpec((1, nch), lambda i: (0, 0))
    kern = functools.partial(_s5_kernel, tm=tm, per_group_state=per_group_state)
    return pl.pallas_call(
        kern,
        grid=(m // tm,),
        in_specs=[pl.BlockSpec((tm, GROUP_W), lambda i: (i, 0)), full(bb), full(cc), full(pw), full(ad),
                  row(GROUP_W), full(p["w_glu"]), row(GROUP_W), row(GROUP_W), st, st],
        out_specs=[pl.BlockSpec((tm, GROUP_W), lambda i: (i, 0)), st, st],
        out_shape=[jax.ShapeDtypeStruct((m, GROUP_W), BF16),
                   jax.ShapeDtypeStruct((n_state, nch), F32), jax.ShapeDtypeStruct((n_state, nch), F32)],
        scratch_shapes=[pltpu.VMEM((tm, nch), F32), pltpu.VMEM((tm, nch), F32),
                        pltpu.VMEM((1, nch), F32), pltpu.VMEM((1, nch), F32),
                        pltpu.VMEM((GROUP_W, GROUP_W), BF16)],
        compiler_params=_cparams(("arbitrary",)),
        name="s5_mixer",
    )(u, bb, cc, pw, ad, p["d"].reshape(1, -1), p["w_glu"], p["b_glu"].reshape(1, -1),
      p["norm"].reshape(1, -1), sr0, si0)


def _ffn_up_kernel(h_ref, wg_ref, wv_ref, cwg_ref, cwv_ref, cbg_ref, cbv_ref, c0g_ref, c0v_ref,
                   act_ref, cng_ref, cnv_ref, wgb, wvb, hg, hv, *, shift, tm, off):
    mstep = pl.program_id(1)
    hist = 2 * shift

    @pl.when(mstep == 0)
    def _():
        wgb[...] = wg_ref[...].astype(BF16)
        wvb[...] = wv_ref[...].astype(BF16)
        hg[off - hist:off, :] = c0g_ref[...]
        hv[off - hist:off, :] = c0v_ref[...]

    hb = h_ref[...]
    hg[off:off + tm, :] = jnp.dot(hb, wgb[...], preferred_element_type=F32)
    hv[off:off + tm, :] = jnp.dot(hb, wvb[...], preferred_element_type=F32)

    def conv(hs, cw_ref, cb_ref):
        return (cb_ref[...] + cw_ref[0:1, :] * hs[off - hist:off - hist + tm, :]
                + cw_ref[1:2, :] * hs[off - shift:off - shift + tm, :]
                + cw_ref[2:3, :] * hs[off:off + tm, :])

    gate = conv(hg, cwg_ref, cbg_ref)
    val = conv(hv, cwv_ref, cbv_ref)
    act_ref[...] = (gate * jax.nn.sigmoid(gate) * val).astype(act_ref.dtype)
    tail_g = hg[off + tm - hist:off + tm, :]
    tail_v = hv[off + tm - hist:off + tm, :]
    hg[off - hist:off, :] = tail_g
    hv[off - hist:off, :] = tail_v

    @pl.when(mstep == pl.num_programs(1) - 1)
    def _():
        cng_ref[...] = tail_g
        cnv_ref[...] = tail_v


def ffn_up(h, w_up, conv_w, conv_b, c0g, c0v, *, shift, tm, tn=512):
    m, k = h.shape
    nj = D_FF // tn
    hist = 2 * shift
    off = max(SUBLANES, hist)
    kern = functools.partial(_ffn_up_kernel, shift=shift, tm=tm, off=off)
    cw = jnp.concatenate([conv_w, jnp.zeros((SUBLANES - conv_w.shape[0], conv_w.shape[1]), F32)], axis=0)
    cb = conv_b.reshape(1, -1)
    return pl.pallas_call(
        kern,
        grid=(nj, m // tm),
        in_specs=[pl.BlockSpec((tm, k), lambda j, i: (i, 0)),
                  pl.BlockSpec((k, tn), lambda j, i: (0, j)),
                  pl.BlockSpec((k, tn), lambda j, i: (0, j + nj)),
                  pl.BlockSpec((SUBLANES, tn), lambda j, i: (0, j)),
                  pl.BlockSpec((SUBLANES, tn), lambda j, i: (0, j + nj)),
                  pl.BlockSpec((1, tn), lambda j, i: (0, j)),
                  pl.BlockSpec((1, tn), lambda j, i: (0, j + nj)),
                  pl.BlockSpec((hist, tn), lambda j, i: (0, j)),
                  pl.BlockSpec((hist, tn), lambda j, i: (0, j))],
        out_specs=[pl.BlockSpec((tm, tn), lambda j, i: (i, j)),
                   pl.BlockSpec((hist, tn), lambda j, i: (0, j)),
                   pl.BlockSpec((hist, tn), lambda j, i: (0, j))],
        out_shape=[jax.ShapeDtypeStruct((m, D_FF), BF16),
                   jax.ShapeDtypeStruct((hist, D_FF), F32), jax.ShapeDtypeStruct((hist, D_FF), F32)],
        scratch_shapes=[pltpu.VMEM((k, tn), BF16), pltpu.VMEM((k, tn), BF16),
                        pltpu.VMEM((off + tm, tn), F32), pltpu.VMEM((off + tm, tn), F32)],
        compiler_params=_cparams(("parallel", "arbitrary")),
        name="ffn_up",
    )(h, w_up, w_up, cw, cw, cb, cb, c0g, c0v)


def _ret_rope_tables(pos):
    half = RET_HD // 2
    inv = jnp.power(RET_ROPE_BASE, -jnp.arange(half, dtype=F32) / half)
    ang = pos.astype(F32)[:, None] * inv[None, :]
    cos, sin = jnp.cos(ang), jnp.sin(ang)
    return jnp.concatenate([cos, cos], axis=1), jnp.concatenate([-sin, sin], axis=1)


def _diff_rope_tables(pos):
    half = ROPE_DIM // 2
    inv = jnp.power(ROPE_THETA, -jnp.arange(half, dtype=F32) / half)
    ang = pos.astype(F32)[:, None] * inv[None, :]
    cos, sin = jnp.cos(ang), jnp.sin(ang)
    n = pos.shape[0]
    rest = DIFF_QD - ROPE_DIM
    c = jnp.concatenate([cos, cos, jnp.ones((n, rest), F32)], axis=1)
    s1 = jnp.concatenate([-sin, jnp.zeros((n, half + rest), F32)], axis=1)
    s2 = jnp.concatenate([jnp.zeros((n, half), F32), sin, jnp.zeros((n, rest), F32)], axis=1)
    rep = LANES // DIFF_QD
    return jnp.tile(c, (1, rep)), jnp.tile(s1, (1, rep)), jnp.tile(s2, (1, rep))


def _rwkv_state_to_pairs(s):
    b = s.shape[0]
    s = s.reshape(b, RWKV_HEADS // 2, 2, RWKV_HD, RWKV_HD)
    return jnp.transpose(s, (0, 1, 4, 2, 3)).reshape(b, RWKV_HEADS // 2, RWKV_HD, LANES)


def _rwkv_state_from_pairs(s):
    b = s.shape[0]
    s = s.reshape(b, RWKV_HEADS // 2, RWKV_HD, 2, RWKV_HD)
    return jnp.transpose(s, (0, 1, 3, 4, 2)).reshape(b, RWKV_HEADS, RWKV_HD, RWKV_HD)


def kernel(x_prompt, x_sample, p_prompt, p_sample, cache_k, cache_v, page_table, state_ret, state_rwkv, state_rwkv_shift, state_s5_re, state_s5_im, state_ffn_conv, norm_mix, w_in, w_out, ret_norm_w, ret_norm_b, diff_lq1, diff_lk1, diff_lq2, diff_lk2, diff_subln, rwkv_mu, rwkv_w0, rwkv_w2, rwkv_a0, rwkv_a2, rwkv_g2, rwkv_kk, rwkv_ka, rwkv_rk, rwkv_ln_w, rwkv_ln_b, s5_lam_re, s5_lam_im, s5_log_step, s5_b_re, s5_b_im, s5_c_re, s5_c_im, s5_d, s5_w_glu, s5_b_glu, s5_norm, norm_ffn, ffn_w_up, ffn_conv_w, ffn_conv_b, ffn_w_down, norm_ple, ple_w_proj, ple_norm_e, ple_w_gate, norm_final):
    depth = w_in.shape[0]
    bp, tp, d = x_prompt.shape
    nb, ts, _ = x_sample.shape
    assert bp == 1
    mp, ms = bp * tp, nb * ts
    past_len = page_table.shape[1] * PAGE_SIZE
    nch = S5_GROUPS * S5_N

    x = jnp.concatenate([x_prompt.reshape(mp, d), x_sample.reshape(ms, d)], axis=0)
    p_all = jnp.concatenate([p_prompt.reshape(depth, mp, -1), p_sample.reshape(depth, ms, -1)], axis=1)
    pos_p = jnp.arange(tp, dtype=jnp.int32)
    pos_s = past_len + jnp.arange(ts, dtype=jnp.int32)
    ret_cos_p, ret_sin_p = _ret_rope_tables(pos_p)
    ret_cos_s, ret_sin_s = _ret_rope_tables(pos_s)
    pos_all = jnp.concatenate([pos_p, jnp.tile(pos_s, nb)])
    dc, ds1, ds2 = _diff_rope_tables(pos_all)
    cache_k2 = cache_k.reshape(cache_k.shape[0], cache_k.shape[1], PAGE_SIZE, GROUP_W)
    cache_v2 = cache_v.reshape(cache_v.shape[0], cache_v.shape[1], PAGE_SIZE, GROUP_W)
    head_of = jnp.arange(GROUP_W) // RWKV_HD
    e512 = (head_of[:, None] == head_of[None, :]).astype(BF16)
    e128 = e512[:LANES, :LANES]

    outs = {k: [] for k in ("k_p", "v_p", "k_s", "v_s", "ret_p", "ret_s", "rwkv_p", "rwkv_s", "sh_p", "sh_s",
                            "s5r_p", "s5i_p", "s5r_s", "s5i_s", "conv_p", "conv_s")}
    for i in range(depth):
        h = rmsnorm_rows(x, norm_mix[i], BF16)
        wi = w_in[i]
        c_ret = matmul(h, wi, tn=1024, col0=0, ncols=RET_COLS, name="proj_ret")
        c_diff = matmul(h, wi, tn=512, col0=RET_COLS, ncols=DIFF_COLS, name="proj_diff")
        c_rwkv = matmul(h, wi, tn=896, col0=RET_COLS + DIFF_COLS, ncols=RWKV_COLS, name="proj_rwkv")
        c_s5 = matmul(h, wi, tn=256, col0=RET_COLS + DIFF_COLS + RWKV_COLS, ncols=S5_COLS, name="proj_s5")

        o_ret_p, s_ret_p = retention(c_ret, 0, bp, tp, ret_cos_p, ret_sin_p,
                                     jnp.zeros((bp, RET_HEADS, RET_HD, RET_HD), F32),
                                     ret_norm_w[i], ret_norm_b[i], chunk=RET_CHUNK, chunks_per_step=4)
        o_ret_s, s_ret_s = retention(c_ret, mp, nb, ts, ret_cos_s, ret_sin_s, state_ret[i],
                                     ret_norm_w[i], ret_norm_b[i], chunk=ts, chunks_per_step=1)

        lam_init = 0.8 - 0.6 * math.exp(-0.3 * i)
        lam = (jnp.exp(jnp.sum(diff_lq1[i] * diff_lk1[i])) - jnp.exp(jnp.sum(diff_lq2[i] * diff_lk2[i])) + lam_init)
        q0, q1, k_new, kb, vb = diff_prep(c_diff, dc, ds1, ds2)
        v_new = c_diff[:, 2 * GROUP_W:]
        o_diff_p = diff_attention_prompt(q0[:mp], q1[:mp], kb[:mp], vb[:mp], lam, diff_subln[i], lam_init)
        qs = (q0[mp:].astype(F32) + q1[mp:].astype(F32)).reshape(nb, ts, GROUP_W)
        o_diff_s = diff_attention_sample(qs, k_new[mp:], v_new[mp:], cache_k2, cache_v2, i, page_table,
                                         lam, diff_subln[i], lam_init)

        cr_p = c_rwkv[:mp]
        cr_s = c_rwkv[mp:].reshape(nb, ts, RWKV_COLS)
        prev = jnp.concatenate([jnp.zeros((1, RWKV_COLS), F32), cr_p[:-1],
                                jnp.concatenate([state_rwkv_shift[i][:, None], cr_s[:, :-1]], axis=1).reshape(ms, RWKV_COLS)],
                               axis=0)
        rp = dict(mu=rwkv_mu[i], w0=rwkv_w0[i], w2=rwkv_w2[i], a0=rwkv_a0[i], a2=rwkv_a2[i], g2=rwkv_g2[i],
                  kk=rwkv_kk[i], ka=rwkv_ka[i], rk=rwkv_rk[i].reshape(-1))
        wb, ka, wr, rv, br, kr, bon, rg = rwkv_prep(c_rwkv, prev, rp, e512)
        lnw, lnb = rwkv_ln_w[i].reshape(-1), rwkv_ln_b[i].reshape(-1)
        sl = lambda a, lo, hi: a[lo:hi]
        o_rwkv_p, s_rwkv_p = rwkv_scan(*[sl(a, 0, mp) for a in (wb, ka, wr, rv, br, kr, bon, rg)], lnw, lnb, e128,
                                       jnp.zeros((bp, RWKV_HEADS // 2, RWKV_HD, LANES), F32),
                                       t_seq=tp, seq_per_step=1, t_step=256)
        o_rwkv_s, s_rwkv_s = rwkv_scan(*[sl(a, mp, mp + ms) for a in (wb, ka, wr, rv, br, kr, bon, rg)], lnw, lnb, e128,
                                       _rwkv_state_to_pairs(state_rwkv[i]),
                                       t_seq=ts, seq_per_step=16, t_step=ts)

        sp = dict(lam_re=s5_lam_re[i], lam_im=s5_lam_im[i], log_step=s5_log_step[i], b_re=s5_b_re[i], b_im=s5_b_im[i],
                  c_re=s5_c_re[i], c_im=s5_c_im[i], d=s5_d[i], w_glu=s5_w_glu[i], b_glu=s5_b_glu[i], norm=s5_norm[i])
        o_s5_p, s5r_p, s5i_p = s5_mixer(c_s5[:mp], sp, jnp.zeros((1, nch), F32), jnp.zeros((1, nch), F32),
                                        per_group_state=False)
        o_s5_s, s5r_s, s5i_s = s5_mixer(c_s5[mp:], sp, state_s5_re[i].reshape(nb, nch), state_s5_im[i].reshape(nb, nch),
                                        per_group_state=True)

        o_all = jnp.concatenate([jnp.concatenate([o_ret_p, o_diff_p, o_rwkv_p, o_s5_p], axis=1),
                                 jnp.concatenate([o_ret_s, o_diff_s, o_rwkv_s, o_s5_s], axis=1)], axis=0)
        x = matmul(o_all, w_out[i], tn=512, tiles=(x,), epilogue=lambda acc, res: res + acc, name="w_out")

        h2 = rmsnorm_rows(x, norm_ffn[i], BF16)
        zc = jnp.zeros((2, D_FF), F32)
        act_p, cg_p, cv_p = ffn_up(h2[:mp], ffn_w_up[i], ffn_conv_w[i], ffn_conv_b[i], zc, zc, shift=1, tm=512)
        h2_s = jnp.transpose(h2[mp:].reshape(nb, ts, d), (1, 0, 2)).reshape(ms, d)
        c0 = jnp.transpose(state_ffn_conv[i], (1, 0, 2)).reshape(2 * nb, 2 * D_FF)
        act_s, cg_s, cv_s = ffn_up(h2_s, ffn_w_up[i], ffn_conv_w[i], ffn_conv_b[i], c0[:, :D_FF], c0[:, D_FF:],
                                   shift=nb, tm=nb)
        x_tm = jnp.concatenate([x[:mp], jnp.transpose(x[mp:].reshape(nb, ts, d), (1, 0, 2)).reshape(ms, d)], axis=0)
        x_tm = matmul(jnp.concatenate([act_p, act_s], axis=0), ffn_w_down[i], tn=512, tiles=(x_tm,),
                      epilogue=lambda acc, res: res + acc, name="ffn_down")
        x = jnp.concatenate([x_tm[:mp], jnp.transpose(x_tm[mp:].reshape(ts, nb, d), (1, 0, 2)).reshape(ms, d)], axis=0)

        e = matmul(p_all[i], ple_w_proj[i], tn=d, tm=256, rows=(ple_norm_e[i],),
                   epilogue=lambda acc, g: acc * lax.rsqrt(jnp.mean(acc * acc, axis=-1, keepdims=True) + EPS) * g,
                   name="ple_proj")
        h3 = rmsnorm_rows(x, norm_ple[i], BF16)
        x = matmul(h3, ple_w_gate[i], tn=512, tiles=(x, e),
                   epilogue=lambda acc, res, ee: res + ee * jax.nn.sigmoid(acc), name="ple_gate")

        outs["k_p"].append(k_new[:mp].reshape(bp, tp, DIFF_HEADS, 2 * DIFF_QD))
        outs["v_p"].append(v_new[:mp].reshape(bp, tp, DIFF_HEADS, DIFF_VD))
        outs["k_s"].append(k_new[mp:].reshape(nb, ts, DIFF_HEADS, 2 * DIFF_QD))
        outs["v_s"].append(v_new[mp:].reshape(nb, ts, DIFF_HEADS, DIFF_VD))
        outs["ret_p"].append(s_ret_p)
        outs["ret_s"].append(s_ret_s)
        outs["rwkv_p"].append(_rwkv_state_from_pairs(s_rwkv_p))
        outs["rwkv_s"].append(_rwkv_state_from_pairs(s_rwkv_s))
        outs["sh_p"].append(cr_p[-1:].reshape(bp, RWKV_COLS))
        outs["sh_s"].append(cr_s[:, -1])
        outs["s5r_p"].append(s5r_p.reshape(bp, S5_GROUPS, S5_N))
        outs["s5i_p"].append(s5i_p.reshape(bp, S5_GROUPS, S5_N))
        outs["s5r_s"].append(s5r_s.reshape(nb, S5_GROUPS, S5_N))
        outs["s5i_s"].append(s5i_s.reshape(nb, S5_GROUPS, S5_N))
        outs["conv_p"].append(jnp.concatenate([cg_p, cv_p], axis=1).reshape(bp, 2, 2 * D_FF))
        outs["conv_s"].append(jnp.transpose(jnp.concatenate([cg_s, cv_s], axis=1).reshape(2, nb, 2 * D_FF), (1, 0, 2)))

    y = rmsnorm_rows(x, norm_final, F32)
    st = lambda k: jnp.stack(outs[k])
    return (y[:mp].reshape(bp, tp, d), y[mp:].reshape(nb, ts, d),
            st("k_p"), st("v_p"), st("k_s"), st("v_s"), st("ret_p"), st("ret_s"), st("rwkv_p"), st("rwkv_s"),
            st("sh_p"), st("sh_s"), st("s5r_p"), st("s5i_p"), st("s5r_s"), st("s5i_s"), st("conv_p"), st("conv_s"))
```

```python
import functools
import math

import jax
import jax.numpy as jnp
import numpy as np
from jax import lax
from jax.experimental import pallas as pl
from jax.experimental.pallas import tpu as pltpu

F32 = jnp.float32
BF16 = jnp.bfloat16

D_MODEL = 2048
GROUP_W = 512
RET_HEADS = 4
RET_HD = 128
RET_CHUNK = 128
RET_ROPE_BASE = 10000.0
DIFF_HEADS = 4
DIFF_VD = 128
DIFF_QD = 64
ROPE_THETA = 500000.0
ROPE_DIM = 16
PAGE_SIZE = 128
RWKV_HD = 64
RWKV_HEADS = 8
RWKV_LN_EPS = 64e-5
S5_CH = 16
S5_GROUPS = 32
S5_N = 64
D_FF = 5632
EPS = 1e-6
NEG_INF = -1e30

RET_COLS = 4 * GROUP_W
DIFF_COLS = 3 * GROUP_W
RWKV_COLS = 3 * GROUP_W + 64 + 64 + 128
S5_COLS = GROUP_W

LANES = 128
SUBLANES = 8
VMEM_LIMIT_BYTES = 52 * 1024 * 1024


def _cparams(sem, vmem=VMEM_LIMIT_BYTES):
    return pltpu.CompilerParams(dimension_semantics=sem, vmem_limit_bytes=vmem)


def _split_bf16(x):
    hi = x.astype(BF16)
    lo = (x - hi.astype(F32)).astype(BF16)
    return hi, lo


def _dot_hilo(x, w_bf16):
    hi, lo = _split_bf16(x)
    return (jnp.dot(hi, w_bf16, preferred_element_type=F32)
            + jnp.dot(lo, w_bf16, preferred_element_type=F32))


def _dot3(x, w_hi, w_lo):
    hi, lo = _split_bf16(x)
    return (jnp.dot(hi, w_hi, preferred_element_type=F32)
            + jnp.dot(hi, w_lo, preferred_element_type=F32)
            + jnp.dot(lo, w_hi, preferred_element_type=F32))


def _rmsnorm_kernel(x_ref, g_ref, o_ref):
    x = x_ref[...]
    y = x * lax.rsqrt(jnp.mean(x * x, axis=-1, keepdims=True) + EPS)
    o_ref[...] = (y * g_ref[...]).astype(o_ref.dtype)


def rmsnorm_rows(x, g, out_dtype, tm=512):
    m, d = x.shape
    return pl.pallas_call(
        _rmsnorm_kernel,
        grid=(m // tm,),
        in_specs=[pl.BlockSpec((tm, d), lambda i: (i, 0)),
                  pl.BlockSpec((1, d), lambda i: (0, 0))],
        out_specs=pl.BlockSpec((tm, d), lambda i: (i, 0)),
        out_shape=jax.ShapeDtypeStruct((m, d), out_dtype),
        compiler_params=_cparams(("parallel",)),
        name="rmsnorm_rows",
    )(x, g.reshape(1, d))


def _mm_kernel(x_ref, w_ref, *rest, epilogue, n_extra):
    extra = rest[:n_extra]
    o_ref, wb_ref = rest[n_extra], rest[n_extra + 1]

    @pl.when(pl.program_id(1) == 0)
    def _():
        wb_ref[...] = w_ref[...].astype(BF16)

    acc = jnp.dot(x_ref[...].astype(BF16), wb_ref[...], preferred_element_type=F32)
    if epilogue is not None:
        acc = epilogue(acc, *[e[...] for e in extra])
    o_ref[...] = acc.astype(o_ref.dtype)


def matmul(x, w, *, tn, tm=512, col0=0, ncols=None, epilogue=None, tiles=(), rows=(),
           out_dtype=F32, name="matmul"):
    m, k = x.shape
    n = w.shape[1] - col0 if ncols is None else ncols
    assert col0 % tn == 0 and n % tn == 0 and m % tm == 0
    joff = col0 // tn
    in_specs = [pl.BlockSpec((tm, k), lambda j, i: (i, 0)),
                pl.BlockSpec((k, tn), lambda j, i: (0, j + joff))]
    in_specs += [pl.BlockSpec((tm, tn), lambda j, i: (i, j)) for _ in tiles]
    in_specs += [pl.BlockSpec((1, tn), lambda j, i: (0, j)) for _ in rows]
    kern = functools.partial(_mm_kernel, epilogue=epilogue, n_extra=len(tiles) + len(rows))
    return pl.pallas_call(
        kern,
        grid=(n // tn, m // tm),
        in_specs=in_specs,
        out_specs=pl.BlockSpec((tm, tn), lambda j, i: (i, j)),
        out_shape=jax.ShapeDtypeStruct((m, n), out_dtype),
        scratch_shapes=[pltpu.VMEM((k, tn), BF16)],
        compiler_params=_cparams(("parallel", "arbitrary")),
        name=name,
    )(x, w, *tiles, *[r.reshape(1, -1) for r in rows])


def _mm_split_kernel(*refs, n_parts, part_k, n_prompt_tiles, epilogue, n_extra):
    xp = refs[:n_parts]
    xs = refs[n_parts:2 * n_parts]
    w_ref = refs[2 * n_parts]
    extra = refs[2 * n_parts + 1:2 * n_parts + 1 + n_extra]
    o_ref, wb_ref = refs[2 * n_parts + 1 + n_extra], refs[2 * n_parts + 2 + n_extra]
    i = pl.program_id(1)

    @pl.when(i == 0)
    def _():
        wb_ref[...] = w_ref[...].astype(BF16)

    def body(parts):
        acc = None
        for g, r in enumerate(parts):
            d = jnp.dot(r[...], wb_ref[g * part_k:(g + 1) * part_k, :], preferred_element_type=F32)
            acc = d if acc is None else acc + d
        o_ref[...] = epilogue(acc, *[e[...] for e in extra]).astype(o_ref.dtype)

    @pl.when(i < n_prompt_tiles)
    def _():
        body(xp)

    @pl.when(i >= n_prompt_tiles)
    def _():
        body(xs)


def matmul_split(xs_prompt, xs_sample, w, *, tn, tm, epilogue, tiles=(), out_dtype=F32, name="matmul_split"):
    n_parts = len(xs_prompt)
    mp, part_k = xs_prompt[0].shape
    ms = xs_sample[0].shape[0]
    k, n = w.shape
    assert k == n_parts * part_k and mp % tm == 0 and ms % tm == 0 and n % tn == 0
    npt, nst = mp // tm, ms // tm
    in_specs = [pl.BlockSpec((tm, part_k), lambda j, i: (jnp.minimum(i, npt - 1), 0)) for _ in xs_prompt]
    in_specs += [pl.BlockSpec((tm, part_k), lambda j, i: (jnp.maximum(i - npt, 0), 0)) for _ in xs_sample]
    in_specs += [pl.BlockSpec((k, tn), lambda j, i: (0, j))]
    in_specs += [pl.BlockSpec((tm, tn), lambda j, i: (i, j)) for _ in tiles]
    kern = functools.partial(_mm_split_kernel, n_parts=n_parts, part_k=part_k, n_prompt_tiles=npt,
                             epilogue=epilogue, n_extra=len(tiles))
    return pl.pallas_call(
        kern,
        grid=(n // tn, npt + nst),
        in_specs=in_specs,
        out_specs=pl.BlockSpec((tm, tn), lambda j, i: (i, j)),
        out_shape=jax.ShapeDtypeStruct((mp + ms, n), out_dtype),
        scratch_shapes=[pltpu.VMEM((k, tn), BF16)],
        compiler_params=_cparams(("parallel", "arbitrary")),
        name=name,
    )(*xs_prompt, *xs_sample, w, *tiles)


def _ret_kernel(q_ref, k_ref, v_ref, g_ref, cos_ref, sin_ref, dmask_ref, qdec_ref, kdec_ref,
                cdec_ref, nw_ref, nb_ref, s0_ref, o_ref, sn_ref, s_scr, *, chunk, n_chunks):
    c = pl.program_id(1)

    @pl.when(c == 0)
    def _():
        s_scr[...] = s0_ref[...]

    cos = cos_ref[...]
    sin = sin_ref[...]
    for h in range(RET_HEADS):
        hs = slice(h * RET_HD, (h + 1) * RET_HD)
        qh = q_ref[:, hs]
        kh = k_ref[:, hs]
        qr = qh * cos + pltpu.roll(qh, RET_HD // 2, axis=1) * sin
        kr = (kh * cos + pltpu.roll(kh, RET_HD // 2, axis=1) * sin) * (RET_HD ** -0.5)
        vh = v_ref[:, hs]
        s = s_scr[h]
        outs = []
        for ci in range(n_chunks):
            rs = slice(ci * chunk, (ci + 1) * chunk)
            qc, kc, vc = qr[rs], kr[rs], vh[rs].astype(BF16)
            att = lax.dot_general(qc.astype(BF16), kc.astype(BF16), (((1,), (1,)), ((), ())),
                                  preferred_element_type=F32) * dmask_ref[h]
            o = jnp.dot(att.astype(BF16), vc, preferred_element_type=F32)
            o += jnp.dot((qc * qdec_ref[h]).astype(BF16), s.astype(BF16), preferred_element_type=F32)
            kd = (kc * kdec_ref[h]).astype(BF16)
            s = s * cdec_ref[h, 0:1, :] + lax.dot_general(kd, vc, (((0,), (0,)), ((), ())),
                                                  preferred_element_type=F32)
            outs.append(o)
        s_scr[h] = s
        o = outs[0] if n_chunks == 1 else jnp.concatenate(outs, axis=0)
        mu = jnp.mean(o, axis=-1, keepdims=True)
        var = jnp.mean(jnp.square(o - mu), axis=-1, keepdims=True)
        o = (o - mu) * lax.rsqrt(var + EPS) * nw_ref[h:h + 1, :] + nb_ref[h:h + 1, :]
        gh = g_ref[:, hs]
        o_ref[:, hs] = (o * (gh * jax.nn.sigmoid(gh))).astype(o_ref.dtype)

    @pl.when(c == pl.num_programs(1) - 1)
    def _():
        sn_ref[...] = s_scr[...]


def retention(proj, row0, n_seq, t_seq, cos, sin, s0, norm_w, norm_b, *, chunk, chunks_per_step):
    rb = chunk * chunks_per_step
    steps = t_seq // rb
    assert row0 % rb == 0 and t_seq % rb == 0
    b0 = row0 // rb
    log_g = jnp.log1p(-jnp.exp2(-5.0 - jnp.arange(RET_HEADS, dtype=F32)))
    idx = jnp.arange(chunk, dtype=F32)
    rel = idx[:, None] - idx[None, :]
    dmask = jnp.where(rel >= 0, jnp.exp(log_g[:, None, None] * jnp.maximum(rel, 0.0)), 0.0)
    ones = jnp.ones((1, 1, RET_HD), F32)
    qdec = jnp.exp(log_g[:, None] * (idx + 1.0))[:, :, None] * ones
    kdec = jnp.exp(log_g[:, None] * (chunk - 1.0 - idx))[:, :, None] * ones
    cdec = jnp.exp(log_g * chunk)[:, None, None] * jnp.ones((1, SUBLANES, RET_HD), F32)

    def col(j):
        return pl.BlockSpec((rb, GROUP_W), lambda s, c: (b0 + s * steps + c, j))

    full = lambda shape: pl.BlockSpec(shape, lambda s, c: (0,) * len(shape))
    kern = functools.partial(_ret_kernel, chunk=chunk, n_chunks=chunks_per_step)
    return pl.pallas_call(
        kern,
        grid=(n_seq, steps),
        in_specs=[col(0), col(1), col(2), col(3),
                  pl.BlockSpec((rb, RET_HD), lambda s, c: (c, 0)),
                  pl.BlockSpec((rb, RET_HD), lambda s, c: (c, 0)),
                  full((RET_HEADS, chunk, chunk)),
                  full((RET_HEADS, chunk, RET_HD)),
                  full((RET_HEADS, chunk, RET_HD)),
                  full((RET_HEADS, SUBLANES, RET_HD)),
                  full((RET_HEADS, RET_HD)),
                  full((RET_HEADS, RET_HD)),
                  pl.BlockSpec((None, RET_HEADS, RET_HD, RET_HD), lambda s, c: (s, 0, 0, 0))],
        out_specs=[pl.BlockSpec((rb, GROUP_W), lambda s, c: (s * steps + c, 0)),
                   pl.BlockSpec((None, RET_HEADS, RET_HD, RET_HD), lambda s, c: (s, 0, 0, 0))],
        out_shape=[jax.ShapeDtypeStruct((n_seq * t_seq, GROUP_W), BF16),
                   jax.ShapeDtypeStruct((n_seq, RET_HEADS, RET_HD, RET_HD), F32)],
        scratch_shapes=[pltpu.VMEM((RET_HEADS, RET_HD, RET_HD), F32)],
        compiler_params=_cparams(("arbitrary", "arbitrary")),
        name="retention",
    )(proj, proj, proj, proj, cos, sin, dmask, qdec, kdec, cdec, norm_w, norm_b, s0)


def _diff_prep_kernel(q_ref, k_ref, v_ref, c_ref, s1_ref, s2_ref, q0_ref, q1_ref, kn_ref, kb_ref, vt_ref):
    c, s1, s2 = c_ref[...], s1_ref[...], s2_ref[...]
    lane = lax.broadcasted_iota(jnp.int32, c.shape, 1)
    lo = lane < DIFF_QD
    for j in range(GROUP_W // LANES):
        cs = slice(j * LANES, (j + 1) * LANES)
        q = q_ref[:, cs]
        k = k_ref[:, cs]
        qr = q * c + pltpu.roll(q, LANES - ROPE_DIM // 2, axis=1) * s1 + pltpu.roll(q, ROPE_DIM // 2, axis=1) * s2
        kr = k * c + pltpu.roll(k, LANES - ROPE_DIM // 2, axis=1) * s1 + pltpu.roll(k, ROPE_DIM // 2, axis=1) * s2
        qr = qr * (DIFF_QD ** -0.5)
        q0_ref[:, cs] = jnp.where(lo, qr, 0.0).astype(BF16)
        q1_ref[:, cs] = jnp.where(lo, 0.0, qr).astype(BF16)
        kn_ref[:, cs] = kr
        kb_ref[:, cs] = kr.astype(BF16)
    vt_ref[...] = v_ref[...].T.astype(BF16)


def diff_prep(proj, c, s1, s2, tm=512):
    m = proj.shape[0]
    col = lambda j: pl.BlockSpec((tm, GROUP_W), lambda i: (i, j))
    tab = pl.BlockSpec((tm, LANES), lambda i: (i, 0))
    out = pl.BlockSpec((tm, GROUP_W), lambda i: (i, 0))
    return pl.pallas_call(
        _diff_prep_kernel,
        grid=(m // tm,),
        in_specs=[col(0), col(1), col(2), tab, tab, tab],
        out_specs=[out] * 4 + [pl.BlockSpec((GROUP_W, tm), lambda i: (0, i))],
        out_shape=[jax.ShapeDtypeStruct((m, GROUP_W), BF16), jax.ShapeDtypeStruct((m, GROUP_W), BF16),
                   jax.ShapeDtypeStruct((m, GROUP_W), F32), jax.ShapeDtypeStruct((m, GROUP_W), BF16),
                   jax.ShapeDtypeStruct((GROUP_W, m), BF16)],
        compiler_params=_cparams(("parallel",)),
        name="diff_prep",
    )(proj, proj, proj, c, s1, s2)


def _diff_flash_kernel(qi_ref, ki_ref, lam_ref, q0_ref, q1_ref, k_ref, vt_ref, sub_ref, o_ref,
                       m0, l0, a0, m1, l1, a1, *, blk, out_scale):
    step = pl.program_id(1)
    qi = qi_ref[step]
    ki = ki_ref[step]

    @pl.when(ki == 0)
    def _():
        m0[...] = jnp.full(m0.shape, NEG_INF, F32)
        m1[...] = jnp.full(m1.shape, NEG_INF, F32)
        l0[...] = jnp.zeros(l0.shape, F32)
        l1[...] = jnp.zeros(l1.shape, F32)
        a0[...] = jnp.zeros(a0.shape, F32)
        a1[...] = jnp.zeros(a1.shape, F32)

    def update(q_ref, m_ref, l_ref, a_ref, diagonal):
        st = lax.dot_general(k_ref[...], q_ref[...], (((1,), (1,)), ((), ())), preferred_element_type=F32)
        if diagonal:
            kpos = lax.broadcasted_iota(jnp.int32, (blk, blk), 0)
            qpos = lax.broadcasted_iota(jnp.int32, (blk, blk), 1)
            st = jnp.where(kpos <= qpos, st, NEG_INF)
        m_prev = m_ref[...]
        m_new = jnp.maximum(m_prev, jnp.max(st, axis=0, keepdims=True))
        alpha = jnp.exp(m_prev - m_new)
        pt = jnp.exp(st - m_new)
        l_ref[...] = alpha * l_ref[...] + jnp.sum(pt, axis=0, keepdims=True)
        a_ref[...] = alpha * a_ref[...] + jnp.dot(vt_ref[...], pt.astype(BF16), preferred_element_type=F32)
        m_ref[...] = m_new

    @pl.when(ki < qi)
    def _():
        update(q0_ref, m0, l0, a0, False)
        update(q1_ref, m1, l1, a1, False)

    @pl.when(ki == qi)
    def _():
        update(q0_ref, m0, l0, a0, True)
        update(q1_ref, m1, l1, a1, True)
        ot = a0[...] / l0[...] - lam_ref[0, 0] * (a1[...] / l1[...])
        yt = ot * lax.rsqrt(jnp.mean(ot * ot, axis=0, keepdims=True) + EPS) * (sub_ref[...] * out_scale)
        o_ref[...] = yt.T.astype(o_ref.dtype)


def diff_attention_prompt(q0, q1, kb, vt, lam, subln, lam_init, *, t, blk=512):
    nb = t // blk
    qi_tbl = np.concatenate([np.full(i + 1, i, np.int32) for i in range(nb)])
    ki_tbl = np.concatenate([np.arange(i + 1, dtype=np.int32) for i in range(nb)])
    kern = functools.partial(_diff_flash_kernel, blk=blk, out_scale=1.0 - lam_init)
    grid_spec = pltpu.PrefetchScalarGridSpec(
        num_scalar_prefetch=2,
        grid=(DIFF_HEADS, len(qi_tbl)),
        in_specs=[pl.BlockSpec(memory_space=pltpu.SMEM),
                  pl.BlockSpec((blk, LANES), lambda h, s, qi, ki: (qi[s], h)),
                  pl.BlockSpec((blk, LANES), lambda h, s, qi, ki: (qi[s], h)),
                  pl.BlockSpec((blk, LANES), lambda h, s, qi, ki: (ki[s], h)),
                  pl.BlockSpec((LANES, blk), lambda h, s, qi, ki: (h, ki[s])),
                  pl.BlockSpec((DIFF_VD, 1), lambda h, s, qi, ki: (0, 0))],
        out_specs=pl.BlockSpec((blk, LANES), lambda h, s, qi, ki: (qi[s], h)),
        scratch_shapes=[pltpu.VMEM((1, blk), F32), pltpu.VMEM((1, blk), F32), pltpu.VMEM((DIFF_VD, blk), F32),
                        pltpu.VMEM((1, blk), F32), pltpu.VMEM((1, blk), F32), pltpu.VMEM((DIFF_VD, blk), F32)],
    )
    return pl.pallas_call(
        kern,
        grid_spec=grid_spec,
        out_shape=jax.ShapeDtypeStruct((t, GROUP_W), BF16),
        compiler_params=_cparams(("arbitrary", "arbitrary")),
        name="diff_attention_prompt",
    )(jnp.asarray(qi_tbl), jnp.asarray(ki_tbl), lam.reshape(1, 1), q0, q1, kb, vt, subln.reshape(DIFF_VD, 1))


def _diff_paged_kernel(pt_ref, lam_ref, qbd_ref, kn_ref, vn_ref, sub_ref, *rest, n_pages, t_new, out_scale):
    k_pages = rest[:n_pages]
    v_pages = rest[n_pages:2 * n_pages]
    o_ref = rest[2 * n_pages]
    kpad, vpad, s_scr = rest[2 * n_pages + 1:]
    half = DIFF_HEADS * t_new
    b = pl.program_id(0)

    @pl.when(b == 0)
    def _():
        kpad[...] = jnp.zeros(kpad.shape, kpad.dtype)
        vpad[...] = jnp.zeros(vpad.shape, vpad.dtype)

    kpad[0:t_new, :] = kn_ref[...]
    vpad[0:t_new, :] = vn_ref[...]

    def heads_to_lanes(ref):
        return jnp.concatenate([ref[pl.ds(h, PAGE_SIZE, stride=DIFF_HEADS), :] for h in range(DIFF_HEADS)],
                               axis=1).astype(BF16)

    qbd = qbd_ref[...]
    key = lax.broadcasted_iota(jnp.int32, (PAGE_SIZE, LANES), 0)
    qpos = lax.broadcasted_iota(jnp.int32, (PAGE_SIZE, LANES), 1) & (t_new - 1)
    m = jnp.full((1, LANES), NEG_INF, F32)
    for p in range(n_pages + 1):
        if p < n_pages:
            s = jnp.dot(heads_to_lanes(k_pages[p]), qbd, preferred_element_type=F32)
        else:
            s = jnp.dot(kpad[...].astype(BF16), qbd, preferred_element_type=F32)
            s = jnp.where(key <= qpos, s, NEG_INF)
        s_scr[p] = s
        m = jnp.maximum(m, jnp.max(s, axis=0, keepdims=True))
    l = jnp.zeros((1, LANES), F32)
    acc = jnp.zeros((LANES, GROUP_W), F32)
    for p in range(n_pages + 1):
        e = jnp.exp(s_scr[p] - m)
        l = l + jnp.sum(e, axis=0, keepdims=True)
        vsrc = heads_to_lanes(v_pages[p]) if p < n_pages else vpad[...].astype(BF16)
        acc += lax.dot_general(e.astype(BF16), vsrc, (((0,), (0,)), ((), ())), preferred_element_type=F32)
    inv_col = jnp.broadcast_to(1.0 / l, (LANES, LANES)).T
    lam = lam_ref[0, 0]
    outs = []
    for h in range(DIFF_HEADS):
        r0, r1 = h * t_new, half + h * t_new
        cs = slice(h * DIFF_VD, (h + 1) * DIFF_VD)
        o = acc[r0:r0 + t_new, cs] * inv_col[r0:r0 + t_new, :] - lam * (acc[r1:r1 + t_new, cs] * inv_col[r1:r1 + t_new, :])
        y = o * lax.rsqrt(jnp.mean(o * o, axis=-1, keepdims=True) + EPS)
        outs.append(y * sub_ref[...] * out_scale)
    o_ref[...] = jnp.concatenate(outs, axis=1).astype(o_ref.dtype)


def diff_attention_sample(qs, kn, vn, cache_k, cache_v, layer, page_table, lam, subln, lam_init):
    nb, t_new, _ = qs.shape
    n_pages = page_table.shape[1]
    rows = PAGE_SIZE * DIFF_HEADS
    assert t_new & (t_new - 1) == 0
    qt = jnp.transpose(qs, (0, 2, 1))
    cblk = jnp.arange(GROUP_W) // DIFF_QD
    col_h = jnp.arange(DIFF_HEADS)
    parts = []
    for mp in range(2):
        sel = (cblk[:, None] == (2 * col_h + mp)[None, :]).astype(F32)
        parts.append((qt[:, :, None, :] * sel[None, :, :, None]).reshape(nb, GROUP_W, DIFF_HEADS * t_new))
    pad = jnp.zeros((nb, GROUP_W, LANES - 2 * DIFF_HEADS * t_new), F32)
    qbd = jnp.concatenate(parts + [pad], axis=-1).astype(BF16)

    def page_spec(p):
        return pl.BlockSpec((None, None, rows, DIFF_VD), lambda b, pt: (layer, pt[b, p], 0, 0))

    kern = functools.partial(_diff_paged_kernel, n_pages=n_pages, t_new=t_new, out_scale=1.0 - lam_init)
    grid_spec = pltpu.PrefetchScalarGridSpec(
        num_scalar_prefetch=1,
        grid=(nb,),
        in_specs=[pl.BlockSpec(memory_space=pltpu.SMEM),
                  pl.BlockSpec((None, GROUP_W, LANES), lambda b, pt: (b, 0, 0)),
                  pl.BlockSpec((t_new, GROUP_W), lambda b, pt: (b, 0)),
                  pl.BlockSpec((t_new, GROUP_W), lambda b, pt: (b, 0)),
                  pl.BlockSpec((1, LANES), lambda b, pt: (0, 0))]
                 + [page_spec(p) for p in range(n_pages)] * 2,
        out_specs=pl.BlockSpec((t_new, GROUP_W), lambda b, pt: (b, 0)),
        scratch_shapes=[pltpu.VMEM((PAGE_SIZE, GROUP_W), F32), pltpu.VMEM((PAGE_SIZE, GROUP_W), F32),
                        pltpu.VMEM((n_pages + 1, PAGE_SIZE, LANES), F32)],
    )
    return pl.pallas_call(
        kern,
        grid_spec=grid_spec,
        out_shape=jax.ShapeDtypeStruct((nb * t_new, GROUP_W), BF16),
        compiler_params=_cparams(("arbitrary",)),
        name="diff_attention_sample",
    )(page_table, lam.reshape(1, 1), qbd, kn, vn, subln.reshape(1, LANES),
      *([cache_k] * n_pages), *([cache_v] * n_pages))


def _interleave64(x, y):
    lane = lax.broadcasted_iota(jnp.int32, (x.shape[0], LANES), 1)
    lo = lane < RWKV_HD
    blocks = []
    for c in range(GROUP_W // LANES):
        xc = x[:, c * LANES:(c + 1) * LANES]
        yc = y[:, c * LANES:(c + 1) * LANES]
        rx = pltpu.roll(xc, RWKV_HD, axis=1)
        ry = pltpu.roll(yc, RWKV_HD, axis=1)
        blocks.append(jnp.where(lo, xc, ry))
        blocks.append(jnp.where(lo, rx, yc))
    return jnp.concatenate(blocks, axis=1)


def _rwkv_prep_kernel(cols_ref, prev_ref, mu_ref, w0_ref, a0_ref, kk_ref, ka_ref, rk_ref,
                      w2h_ref, w2l_ref, a2h_ref, a2l_ref, g2h_ref, g2l_ref, e_ref,
                      wb_ref, ka_out_ref, wr_ref, v_ref, br_ref, kr_ref, bon_ref, g_ref, *, shifted):
    cols = cols_ref[...]
    if shifted:
        before = jnp.where(pl.program_id(0) == 0, 0.0, prev_ref[SUBLANES - 1:SUBLANES, :])
        first = lax.broadcasted_iota(jnp.int32, cols.shape, 0) == 0
        prev = jnp.where(first, before, pltpu.roll(cols, 1, axis=0))
    else:
        prev = prev_ref[...]
    xm = cols + (prev - cols) * mu_ref[...]
    o1 = GROUP_W
    r, k, v = xm[:, 0:o1], xm[:, o1:2 * o1], xm[:, 2 * o1:3 * o1]
    lora = xm[:, 3 * o1:3 * o1 + LANES]
    gl = xm[:, 3 * o1 + LANES:3 * o1 + 2 * LANES]
    wterm = _dot3(jnp.tanh(lora), w2h_ref[...], w2l_ref[...])
    aterm = _dot3(lora, a2h_ref[...], a2l_ref[...])
    z = -(w0_ref[...] + wterm)
    softplus = jnp.maximum(z, 0.0) + jnp.log1p(jnp.exp(-jnp.abs(z)))
    w = -softplus - 0.5
    decay = jnp.exp(-jnp.exp(w))
    a = jax.nn.sigmoid(a0_ref[...] + aterm)
    g = _dot3(jax.nn.sigmoid(gl), g2h_ref[...], g2l_ref[...])
    e = e_ref[...]
    kk = k * kk_ref[...]
    kk = kk * lax.rsqrt(jnp.maximum(_dot_hilo(kk * kk, e), 1e-24))
    k2 = k * (1.0 + (a - 1.0) * ka_ref[...])
    bv = kk * a
    wb_ref[...] = _interleave64(decay, bv)
    ka_out_ref[...] = _interleave64(k2, -kk)
    wr = decay * r
    wr_ref[...] = _interleave64(wr, pltpu.roll(wr, wr.shape[0] - 1, axis=0))
    v_ref[...] = v
    br_ref[...] = _dot_hilo(bv * r, e)
    kr_ref[...] = _dot_hilo(k2 * r, e)
    bon_ref[...] = _dot_hilo(r * k2 * rk_ref[...], e)
    g_ref[...] = g


def rwkv_prep(cols, prev, p, e512, *, row0, m, tm=256):
    assert row0 % tm == 0 and m % tm == 0
    b0 = row0 // tm
    shifted = prev is None
    if shifted:
        prev_arr = cols
        per8 = tm // SUBLANES
        prev_spec = pl.BlockSpec((SUBLANES, RWKV_COLS), lambda i: (jnp.maximum((b0 + i) * per8 - 1, 0), 0))
    else:
        prev_arr = prev
        prev_spec = pl.BlockSpec((tm, RWKV_COLS), lambda i: (i, 0))
    row = lambda n: pl.BlockSpec((1, n), lambda i: (0, 0))
    full = lambda a: pl.BlockSpec(a.shape, lambda i: (0, 0))
    wide = pl.BlockSpec((tm, 2 * GROUP_W), lambda i: (i, 0))
    nar = pl.BlockSpec((tm, GROUP_W), lambda i: (i, 0))
    z64 = jnp.zeros((64, GROUP_W), F32)
    w2p = jnp.concatenate([p["w2"], z64], axis=0)
    a2p = jnp.concatenate([z64, p["a2"]], axis=0)
    w2h, w2l = _split_bf16(w2p)
    a2h, a2l = _split_bf16(a2p)
    g2h, g2l = _split_bf16(p["g2"])
    mats = [w2h, w2l, a2h, a2l, g2h, g2l, e512]
    return pl.pallas_call(
        functools.partial(_rwkv_prep_kernel, shifted=shifted),
        grid=(m // tm,),
        in_specs=[pl.BlockSpec((tm, RWKV_COLS), lambda i: (b0 + i, 0)),
                  prev_spec,
                  row(RWKV_COLS), row(GROUP_W), row(GROUP_W), row(GROUP_W), row(GROUP_W), row(GROUP_W)]
                 + [full(a) for a in mats],
        out_specs=[wide, wide, wide, nar, nar, nar, nar, nar],
        out_shape=[jax.ShapeDtypeStruct((m, 2 * GROUP_W), F32)] * 3
                  + [jax.ShapeDtypeStruct((m, GROUP_W), F32)] * 5,
        compiler_params=_cparams(("parallel",)),
        name="rwkv_prep",
    )(cols, prev_arr, p["mu"].reshape(1, -1), p["w0"].reshape(1, -1), p["a0"].reshape(1, -1),
      p["kk"].reshape(1, -1), p["ka"].reshape(1, -1), p["rk"].reshape(1, -1), *mats)


def _rwkv_scan_kernel(wb_ref, ka_ref, wr_ref, v_ref, br_ref, kr_ref, bon_ref, g_ref, lnw_ref, lnb_ref,
                      e_ref, s0_ref, o_ref, sn_ref, s_scr, y_scr, *, n_seq, t_len):
    c = pl.program_id(2)
    half = RWKV_HD

    def col_forms(tile, k):
        top = jnp.broadcast_to(tile[k:k + 1, 0:LANES], (half, LANES))
        bot = jnp.broadcast_to(tile[k:k + 1, LANES:2 * LANES], (half, LANES))
        t = jnp.concatenate([top, bot], axis=0).T
        return t[0:half], t[half:2 * half]

    def tokens8(g, s):
        r0 = pl.multiple_of(g * SUBLANES, SUBLANES)
        rows = pl.ds(r0, SUBLANES)
        wb8, ka8, wr8 = wb_ref[rows, :], ka_ref[rows, :], wr_ref[rows, :]
        v8, br8, kr8 = v_ref[rows, :], br_ref[rows, :], kr_ref[rows, :]
        ys = []
        for k in range(SUBLANES):
            wc, bc = col_forms(wb8, k)
            kc, ac = col_forms(ka8, k)
            if k % 2 == 0:
                wrc, wrc_next = col_forms(wr8, k)
            else:
                wrc = wrc_next
            sa = jnp.sum(s * ac, axis=0, keepdims=True)
            yp = jnp.sum(s * wrc, axis=0, keepdims=True)
            vrow = v8[k:k + 1, :]
            ys.append(yp + sa * br8[k:k + 1, :] + vrow * kr8[k:k + 1, :])
            s = s * wc + bc * sa + kc * vrow
        y_scr[rows, :] = jnp.concatenate(ys, axis=0)
        return s

    @pl.when(c == 0)
    def _():
        s_scr[...] = s0_ref[...]

    groups = t_len // SUBLANES

    def seq(si, carry):
        s = s_scr[si]
        s = lax.fori_loop(0, groups, lambda g, st: tokens8(si * groups + g, st), s)
        s_scr[si] = s
        return carry

    lax.fori_loop(0, n_seq, seq, 0)

    e = e_ref[...]
    y = y_scr[...]
    mu = _dot_hilo(y, e) * (1.0 / RWKV_HD)
    d = y - mu
    var = _dot_hilo(d * d, e) * (1.0 / RWKV_HD)
    yn = d * lax.rsqrt(var + RWKV_LN_EPS) * lnw_ref[...] + lnb_ref[...]
    o_ref[...] = ((yn + bon_ref[...] * v_ref[...]) * g_ref[...]).astype(o_ref.dtype)

    @pl.when(c == pl.num_programs(2) - 1)
    def _():
        sn_ref[...] = s_scr[...]


def rwkv_scan(wb, ka, wr, v, br, kr, bon, g, ln_w, ln_b, e128, s0, *, t_seq, seq_per_step, t_step):
    n_seq = s0.shape[0]
    chunks = t_seq // t_step
    rb = seq_per_step * t_step
    assert seq_per_step == 1 or chunks == 1
    n_pairs = RWKV_HEADS // 2

    def rows(width):
        return pl.BlockSpec((rb, width), lambda p, s, c: (s * chunks + c, p))

    kern = functools.partial(_rwkv_scan_kernel, n_seq=seq_per_step, t_len=t_step)
    st = pl.BlockSpec((seq_per_step, None, RWKV_HD, LANES), lambda p, s, c: (s, p, 0, 0))
    return pl.pallas_call(
        kern,
        grid=(n_pairs, n_seq // seq_per_step, chunks),
        in_specs=[rows(2 * LANES), rows(2 * LANES), rows(2 * LANES),
                  rows(LANES), rows(LANES), rows(LANES), rows(LANES), rows(LANES),
                  pl.BlockSpec((1, LANES), lambda p, s, c: (0, p)),
                  pl.BlockSpec((1, LANES), lambda p, s, c: (0, p)),
                  pl.BlockSpec((LANES, LANES), lambda p, s, c: (0, 0)),
                  st],
        out_specs=[rows(LANES), st],
        out_shape=[jax.ShapeDtypeStruct((n_seq * t_seq, GROUP_W), BF16),
                   jax.ShapeDtypeStruct(s0.shape, F32)],
        scratch_shapes=[pltpu.VMEM((seq_per_step, RWKV_HD, LANES), F32),
                        pltpu.VMEM((rb, LANES), F32)],
        compiler_params=_cparams(("arbitrary", "arbitrary", "arbitrary")),
        name="rwkv_scan",
    )(wb, ka, wr, v, br, kr, bon, g, ln_w.reshape(1, -1), ln_b.reshape(1, -1), e128, s0)


def _s5_kernel(u_ref, bb_ref, cc_ref, pw_ref, ad_ref, d_ref, wg_ref, bg_ref, nw_ref, sr0_ref, si0_ref,
               o_ref, srn_ref, sin_ref, bur, bui, cr, ci, wgb, *, tm, per_group_state):
    i = pl.program_id(0)
    nch = S5_GROUPS * S5_N
    ngrp = tm // SUBLANES

    @pl.when(i == 0)
    def _():
        wgb[...] = wg_ref[...].astype(BF16)
        if not per_group_state:
            cr[...] = sr0_ref[...]
            ci[...] = si0_ref[...]

    u = u_ref[...]
    bu = jnp.dot(u.astype(BF16), bb_ref[...], preferred_element_type=F32)
    bur[...] = bu[:, 0:nch]
    bui[...] = bu[:, nch:2 * nch]
    sub = lax.broadcasted_iota(jnp.int32, (SUBLANES, nch), 0)

    def group(r0, c_r, c_i):
        xr = bur[pl.ds(r0, SUBLANES), :]
        xi = bui[pl.ds(r0, SUBLANES), :]
        for di, dsh in enumerate((1, 2, 4)):
            ar = ad_ref[2 * di:2 * di + 1, :]
            ai = ad_ref[2 * di + 1:2 * di + 2, :]
            keep = sub >= dsh
            sr = jnp.where(keep, pltpu.roll(xr, dsh, axis=0), 0.0)
            si = jnp.where(keep, pltpu.roll(xi, dsh, axis=0), 0.0)
            xr, xi = xr + ar * sr - ai * si, xi + ar * si + ai * sr
        pr = pw_ref[0:SUBLANES, :]
        pi = pw_ref[SUBLANES:2 * SUBLANES, :]
        xr, xi = xr + pr * c_r - pi * c_i, xi + pr * c_i + pi * c_r
        bur[pl.ds(r0, SUBLANES), :] = xr
        bui[pl.ds(r0, SUBLANES), :] = xi
        return xr[SUBLANES - 1:SUBLANES, :], xi[SUBLANES - 1:SUBLANES, :]

    def block(bi, carry):
        g0 = pl.multiple_of(bi * SUBLANES, SUBLANES)
        if per_group_state:
            st_r = sr0_ref[pl.ds(g0, SUBLANES), :]
            st_i = si0_ref[pl.ds(g0, SUBLANES), :]
            lasts_r, lasts_i = [], []
        else:
            c_r, c_i = cr[...], ci[...]
        for k in range(SUBLANES):
            r0 = pl.multiple_of((g0 + k) * SUBLANES, SUBLANES)
            if per_group_state:
                l_r, l_i = group(r0, st_r[k:k + 1, :], st_i[k:k + 1, :])
                lasts_r.append(l_r)
                lasts_i.append(l_i)
            else:
                c_r, c_i = group(r0, c_r, c_i)
        if per_group_state:
            srn_ref[pl.ds(g0, SUBLANES), :] = jnp.concatenate(lasts_r, axis=0)
            sin_ref[pl.ds(g0, SUBLANES), :] = jnp.concatenate(lasts_i, axis=0)
        else:
            cr[...] = c_r
            ci[...] = c_i
        return carry

    lax.fori_loop(0, ngrp // SUBLANES, block, 0)

    if not per_group_state:
        srn_ref[...] = cr[...]
        sin_ref[...] = ci[...]

    ccv = cc_ref[...]
    y = (jnp.dot(bur[...].astype(BF16), ccv[0:nch], preferred_element_type=F32)
         + jnp.dot(bui[...].astype(BF16), ccv[nch:2 * nch], preferred_element_type=F32))
    y = y + d_ref[...] * u
    y = 0.5 * y * (1.0 + jnp.tanh(math.sqrt(2.0 / math.pi) * (y + 0.044715 * (y * y * y))))
    gate = jnp.dot(y.astype(BF16), wgb[...], preferred_element_type=F32) + bg_ref[...]
    y = y * jax.nn.sigmoid(gate)
    y = y * lax.rsqrt(jnp.mean(y * y, axis=-1, keepdims=True) + EPS)
    o_ref[...] = (y * nw_ref[...]).astype(o_ref.dtype)


def s5_mixer(u, p, sr0, si0, *, per_group_state, row0, m, tm=256):
    assert row0 % tm == 0 and m % tm == 0
    b0 = row0 // tm
    nch = S5_GROUPS * S5_N
    lr, li = p["lam_re"], p["lam_im"]
    dt = jnp.exp(p["log_step"])[:, None]
    mag = jnp.exp(lr * dt)
    ab_re, ab_im = mag * jnp.cos(li * dt), mag * jnp.sin(li * dt)
    den = lr * lr + li * li
    cf_re = ((ab_re - 1.0) * lr + ab_im * li) / den
    cf_im = (ab_im * lr - (ab_re - 1.0) * li) / den
    bb_re = cf_re[..., None] * p["b_re"] - cf_im[..., None] * p["b_im"]
    bb_im = cf_re[..., None] * p["b_im"] + cf_im[..., None] * p["b_re"]
    eye = jnp.eye(S5_GROUPS, dtype=F32)

    def bd_in(b):
        return jnp.einsum("gnc,gh->gchn", b, eye).reshape(GROUP_W, nch)

    def bd_out(cm):
        return jnp.einsum("gcn,gh->gnhc", cm, eye).reshape(nch, GROUP_W)

    bb = jnp.concatenate([bd_in(bb_re), bd_in(bb_im)], axis=1).astype(BF16)
    cc = jnp.concatenate([bd_out(p["c_re"]), -bd_out(p["c_im"])], axis=0).astype(BF16)
    ar, ai = ab_re.reshape(1, nch), ab_im.reshape(1, nch)
    pows = [(ar, ai)]
    for _ in range(SUBLANES - 1):
        pr, pi = pows[-1]
        pows.append((pr * ar - pi * ai, pr * ai + pi * ar))
    pw = jnp.concatenate([jnp.concatenate([q[0] for q in pows], axis=0),
                          jnp.concatenate([q[1] for q in pows], axis=0)], axis=0)
    ad = jnp.concatenate([pows[0][0], pows[0][1], pows[1][0], pows[1][1], pows[3][0], pows[3][1],
                          jnp.zeros((2, nch), F32)], axis=0)
    n_state = sr0.shape[0]
    full = lambda a: pl.BlockSpec(a.shape, lambda i: (0,) * a.ndim)
    row = lambda n: pl.BlockSpec((1, n), lambda i: (0, 0))
    if per_group_state:
        st = pl.BlockSpec((tm // SUBLANES, nch), lambda i: (i, 0))
    else:
        st = pl.BlockSpec((1, nch), lambda i: (0, 0))
    kern = functools.partial(_s5_kernel, tm=tm, per_group_state=per_group_state)
    return pl.pallas_call(
        kern,
        grid=(m // tm,),
        in_specs=[pl.BlockSpec((tm, GROUP_W), lambda i: (b0 + i, 0)), full(bb), full(cc), full(pw), full(ad),
                  row(GROUP_W), full(p["w_glu"]), row(GROUP_W), row(GROUP_W), st, st],
        out_specs=[pl.BlockSpec((tm, GROUP_W), lambda i: (i, 0)), st, st],
        out_shape=[jax.ShapeDtypeStruct((m, GROUP_W), BF16),
                   jax.ShapeDtypeStruct((n_state, nch), F32), jax.ShapeDtypeStruct((n_state, nch), F32)],
        scratch_shapes=[pltpu.VMEM((tm, nch), F32), pltpu.VMEM((tm, nch), F32),
                        pltpu.VMEM((1, nch), F32), pltpu.VMEM((1, nch), F32),
                        pltpu.VMEM((GROUP_W, GROUP_W), BF16)],
        compiler_params=_cparams(("arbitrary",)),
        name="s5_mixer",
    )(u, bb, cc, pw, ad, p["d"].reshape(1, -1), p["w_glu"], p["b_glu"].reshape(1, -1),
      p["norm"].reshape(1, -1), sr0, si0)


def _ffn_up_kernel(h_ref, wg_ref, wv_ref, cwg_ref, cwv_ref, cbg_ref, cbv_ref, c0ga_ref, c0gb_ref, c0va_ref,
                   c0vb_ref, act_ref, cnga_ref, cngb_ref, cnva_ref, cnvb_ref, wgb, wvb, hg, hv, *, shift, tm, off):
    mstep = pl.program_id(1)
    hist = 2 * shift

    @pl.when(mstep == 0)
    def _():
        wgb[...] = wg_ref[...].astype(BF16)
        wvb[...] = wv_ref[...].astype(BF16)
        hg[off - hist:off - shift, :] = c0ga_ref[...]
        hg[off - shift:off, :] = c0gb_ref[...]
        hv[off - hist:off - shift, :] = c0va_ref[...]
        hv[off - shift:off, :] = c0vb_ref[...]

    hb = h_ref[...]
    hg[off:off + tm, :] = jnp.dot(hb, wgb[...], preferred_element_type=F32)
    hv[off:off + tm, :] = jnp.dot(hb, wvb[...], preferred_element_type=F32)

    def conv(hs, cw_ref, cb_ref):
        return (cb_ref[...] + cw_ref[0:1, :] * hs[off - hist:off - hist + tm, :]
                + cw_ref[1:2, :] * hs[off - shift:off - shift + tm, :]
                + cw_ref[2:3, :] * hs[off:off + tm, :])

    gate = conv(hg, cwg_ref, cbg_ref)
    val = conv(hv, cwv_ref, cbv_ref)
    act_ref[...] = (gate * jax.nn.sigmoid(gate) * val).astype(act_ref.dtype)
    tail_g = hg[off + tm - hist:off + tm, :]
    tail_v = hv[off + tm - hist:off + tm, :]
    hg[off - hist:off, :] = tail_g
    hv[off - hist:off, :] = tail_v

    @pl.when(mstep == pl.num_programs(1) - 1)
    def _():
        cnga_ref[...] = tail_g[0:shift]
        cngb_ref[...] = tail_g[shift:hist]
        cnva_ref[...] = tail_v[0:shift]
        cnvb_ref[...] = tail_v[shift:hist]


def ffn_up(h, w_up, conv_w, conv_b, c0, *, time_major, row0=0, m=None, tm=None, tn=512):
    n_seq = c0.shape[0]
    nj = D_FF // tn
    if time_major:
        _, steps, k = h.shape
        shift = tm = n_seq
        h = h.reshape(n_seq, steps * k)
        h_spec = pl.BlockSpec((n_seq, k), lambda j, i: (0, i))
        act_spec = pl.BlockSpec((n_seq, tn), lambda j, i: (0, i * nj + j))
        act_shape = jax.ShapeDtypeStruct((n_seq, steps * D_FF), BF16)
    else:
        k = h.shape[1]
        assert n_seq == 1 and row0 % tm == 0 and m % tm == 0
        shift, steps, b0 = 1, m // tm, row0 // tm
        h_spec = pl.BlockSpec((tm, k), lambda j, i: (b0 + i, 0))
        act_spec = pl.BlockSpec((tm, tn), lambda j, i: (i, j))
        act_shape = jax.ShapeDtypeStruct((m, D_FF), BF16)
    hist = 2 * shift
    off = max(SUBLANES, hist)
    kern = functools.partial(_ffn_up_kernel, shift=shift, tm=tm, off=off)
    cw = jnp.concatenate([conv_w, jnp.zeros((SUBLANES - conv_w.shape[0], conv_w.shape[1]), F32)], axis=0)
    cb = conv_b.reshape(1, -1)

    c0 = c0.reshape(n_seq, 4 * D_FF)

    def c0_spec(tap, half):
        return pl.BlockSpec((n_seq, tn), lambda j, i: (0, (2 * tap + half) * nj + j))

    tap_out = pl.BlockSpec((shift, tn), lambda j, i: (0, j))
    tap_shape = jax.ShapeDtypeStruct((shift, D_FF), F32)
    outs = pl.pallas_call(
        kern,
        grid=(nj, steps),
        in_specs=[h_spec,
                  pl.BlockSpec((k, tn), lambda j, i: (0, j)),
                  pl.BlockSpec((k, tn), lambda j, i: (0, j + nj)),
                  pl.BlockSpec((SUBLANES, tn), lambda j, i: (0, j)),
                  pl.BlockSpec((SUBLANES, tn), lambda j, i: (0, j + nj)),
                  pl.BlockSpec((1, tn), lambda j, i: (0, j)),
                  pl.BlockSpec((1, tn), lambda j, i: (0, j + nj)),
                  c0_spec(0, 0), c0_spec(1, 0), c0_spec(0, 1), c0_spec(1, 1)],
        out_specs=[act_spec, tap_out, tap_out, tap_out, tap_out],
        out_shape=[act_shape, tap_shape, tap_shape, tap_shape, tap_shape],
        scratch_shapes=[pltpu.VMEM((k, tn), BF16), pltpu.VMEM((k, tn), BF16),
                        pltpu.VMEM((off + tm, tn), F32), pltpu.VMEM((off + tm, tn), F32)],
        compiler_params=_cparams(("parallel", "arbitrary")),
        name="ffn_up",
    )(h, w_up, w_up, cw, cw, cb, cb, c0, c0, c0, c0)
    return outs[0], outs[1:]


def conv_state_from_taps(taps):
    ga, gb, va, vb = taps
    return jnp.stack([jnp.concatenate([ga, va], axis=1), jnp.concatenate([gb, vb], axis=1)], axis=1)


def _ret_rope_tables(pos):
    half = RET_HD // 2
    inv = jnp.power(RET_ROPE_BASE, -jnp.arange(half, dtype=F32) / half)
    ang = pos.astype(F32)[:, None] * inv[None, :]
    cos, sin = jnp.cos(ang), jnp.sin(ang)
    return jnp.concatenate([cos, cos], axis=1), jnp.concatenate([-sin, sin], axis=1)


def _diff_rope_tables(pos):
    half = ROPE_DIM // 2
    inv = jnp.power(ROPE_THETA, -jnp.arange(half, dtype=F32) / half)
    ang = pos.astype(F32)[:, None] * inv[None, :]
    cos, sin = jnp.cos(ang), jnp.sin(ang)
    n = pos.shape[0]
    rest = DIFF_QD - ROPE_DIM
    c = jnp.concatenate([cos, cos, jnp.ones((n, rest), F32)], axis=1)
    s1 = jnp.concatenate([-sin, jnp.zeros((n, half + rest), F32)], axis=1)
    s2 = jnp.concatenate([jnp.zeros((n, half), F32), sin, jnp.zeros((n, rest), F32)], axis=1)
    rep = LANES // DIFF_QD
    return jnp.tile(c, (1, rep)), jnp.tile(s1, (1, rep)), jnp.tile(s2, (1, rep))


def _rwkv_state_to_pairs(s):
    b = s.shape[0]
    s = s.reshape(b, RWKV_HEADS // 2, 2, RWKV_HD, RWKV_HD)
    return jnp.transpose(s, (0, 1, 4, 2, 3)).reshape(b, RWKV_HEADS // 2, RWKV_HD, LANES)


def _rwkv_state_from_pairs(s):
    b = s.shape[0]
    s = s.reshape(b, RWKV_HEADS // 2, RWKV_HD, 2, RWKV_HD)
    return jnp.transpose(s, (0, 1, 3, 4, 2)).reshape(b, RWKV_HEADS, RWKV_HD, RWKV_HD)


def kernel(x_prompt, x_sample, p_prompt, p_sample, cache_k, cache_v, page_table, state_ret, state_rwkv, state_rwkv_shift, state_s5_re, state_s5_im, state_ffn_conv, norm_mix, w_in, w_out, ret_norm_w, ret_norm_b, diff_lq1, diff_lk1, diff_lq2, diff_lk2, diff_subln, rwkv_mu, rwkv_w0, rwkv_w2, rwkv_a0, rwkv_a2, rwkv_g2, rwkv_kk, rwkv_ka, rwkv_rk, rwkv_ln_w, rwkv_ln_b, s5_lam_re, s5_lam_im, s5_log_step, s5_b_re, s5_b_im, s5_c_re, s5_c_im, s5_d, s5_w_glu, s5_b_glu, s5_norm, norm_ffn, ffn_w_up, ffn_conv_w, ffn_conv_b, ffn_w_down, norm_ple, ple_w_proj, ple_norm_e, ple_w_gate, norm_final):
    depth = w_in.shape[0]
    bp, tp, d = x_prompt.shape
    nb, ts, _ = x_sample.shape
    assert bp == 1
    mp, ms = bp * tp, nb * ts
    past_len = page_table.shape[1] * PAGE_SIZE
    nch = S5_GROUPS * S5_N

    x = jnp.concatenate([x_prompt.reshape(mp, d), x_sample.reshape(ms, d)], axis=0)
    p_all = jnp.concatenate([p_prompt.reshape(depth, mp, -1), p_sample.reshape(depth, ms, -1)], axis=1)
    pos_p = jnp.arange(tp, dtype=jnp.int32)
    pos_s = past_len + jnp.arange(ts, dtype=jnp.int32)
    ret_cos_p, ret_sin_p = _ret_rope_tables(pos_p)
    ret_cos_s, ret_sin_s = _ret_rope_tables(pos_s)
    pos_all = jnp.concatenate([pos_p, jnp.tile(pos_s, nb)])
    dc, ds1, ds2 = _diff_rope_tables(pos_all)
    cache_k2 = cache_k.reshape(cache_k.shape[0], cache_k.shape[1], PAGE_SIZE * DIFF_HEADS, DIFF_VD)
    cache_v2 = cache_v.reshape(cache_v.shape[0], cache_v.shape[1], PAGE_SIZE * DIFF_HEADS, DIFF_VD)
    head_of = jnp.arange(GROUP_W) // RWKV_HD
    e512 = (head_of[:, None] == head_of[None, :]).astype(BF16)
    e128 = e512[:LANES, :LANES]

    outs = {k: [] for k in ("k_p", "v_p", "k_s", "v_s", "ret_p", "ret_s", "rwkv_p", "rwkv_s", "sh_p", "sh_s",
                            "s5r_p", "s5i_p", "s5r_s", "s5i_s", "conv_p", "conv_s")}
    for i in range(depth):
        h = rmsnorm_rows(x, norm_mix[i], BF16)
        wi = w_in[i]
        c_ret = matmul(h, wi, tn=1024, col0=0, ncols=RET_COLS, name="proj_ret")
        c_diff = matmul(h, wi, tn=512, col0=RET_COLS, ncols=DIFF_COLS, name="proj_diff")
        c_rwkv = matmul(h, wi, tn=896, col0=RET_COLS + DIFF_COLS, ncols=RWKV_COLS, name="proj_rwkv")
        c_s5 = matmul(h, wi, tn=256, col0=RET_COLS + DIFF_COLS + RWKV_COLS, ncols=S5_COLS, name="proj_s5")

        o_ret_p, s_ret_p = retention(c_ret, 0, bp, tp, ret_cos_p, ret_sin_p,
                                     jnp.zeros((bp, RET_HEADS, RET_HD, RET_HD), F32),
                                     ret_norm_w[i], ret_norm_b[i], chunk=RET_CHUNK, chunks_per_step=4)
        o_ret_s, s_ret_s = retention(c_ret, mp, nb, ts, ret_cos_s, ret_sin_s, state_ret[i],
                                     ret_norm_w[i], ret_norm_b[i], chunk=ts, chunks_per_step=1)

        lam_init = 0.8 - 0.6 * math.exp(-0.3 * i)
        lam = (jnp.exp(jnp.sum(diff_lq1[i] * diff_lk1[i])) - jnp.exp(jnp.sum(diff_lq2[i] * diff_lk2[i])) + lam_init)
        q0, q1, k_new, kb, vt = diff_prep(c_diff, dc, ds1, ds2)
        v_new = c_diff[:, 2 * GROUP_W:]
        o_diff_p = diff_attention_prompt(q0, q1, kb, vt, lam, diff_subln[i], lam_init, t=mp)
        qs = (q0[mp:].astype(F32) + q1[mp:].astype(F32)).reshape(nb, ts, GROUP_W)
        o_diff_s = diff_attention_sample(qs, k_new[mp:], v_new[mp:], cache_k2, cache_v2, i, page_table,
                                         lam, diff_subln[i], lam_init)

        cr_s = c_rwkv[mp:].reshape(nb, ts, RWKV_COLS)
        prev_s = jnp.concatenate([state_rwkv_shift[i][:, None], cr_s[:, :-1]], axis=1).reshape(ms, RWKV_COLS)
        rp = dict(mu=rwkv_mu[i], w0=rwkv_w0[i], w2=rwkv_w2[i], a0=rwkv_a0[i], a2=rwkv_a2[i], g2=rwkv_g2[i],
                  kk=rwkv_kk[i], ka=rwkv_ka[i], rk=rwkv_rk[i].reshape(-1))
        lnw, lnb = rwkv_ln_w[i].reshape(-1), rwkv_ln_b[i].reshape(-1)
        o_rwkv_p, s_rwkv_p = rwkv_scan(*rwkv_prep(c_rwkv, None, rp, e512, row0=0, m=mp), lnw, lnb, e128,
                                       jnp.zeros((bp, RWKV_HEADS // 2, RWKV_HD, LANES), F32),
                                       t_seq=tp, seq_per_step=1, t_step=256)
        o_rwkv_s, s_rwkv_s = rwkv_scan(*rwkv_prep(c_rwkv, prev_s, rp, e512, row0=mp, m=ms), lnw, lnb, e128,
                                       _rwkv_state_to_pairs(state_rwkv[i]),
                                       t_seq=ts, seq_per_step=16, t_step=ts)

        sp = dict(lam_re=s5_lam_re[i], lam_im=s5_lam_im[i], log_step=s5_log_step[i], b_re=s5_b_re[i], b_im=s5_b_im[i],
                  c_re=s5_c_re[i], c_im=s5_c_im[i], d=s5_d[i], w_glu=s5_w_glu[i], b_glu=s5_b_glu[i], norm=s5_norm[i])
        o_s5_p, s5r_p, s5i_p = s5_mixer(c_s5, sp, jnp.zeros((1, nch), F32), jnp.zeros((1, nch), F32),
                                        per_group_state=False, row0=0, m=mp)
        o_s5_s, s5r_s, s5i_s = s5_mixer(c_s5, sp, state_s5_re[i].reshape(nb, nch), state_s5_im[i].reshape(nb, nch),
                                        per_group_state=True, row0=mp, m=ms)

        x = matmul_split([o_ret_p, o_diff_p, o_rwkv_p, o_s5_p], [o_ret_s, o_diff_s, o_rwkv_s, o_s5_s], w_out[i],
                         tn=512, tm=512, tiles=(x,), epilogue=lambda acc, res: res + acc, name="w_out")

        h2 = rmsnorm_rows(x, norm_ffn[i], BF16)
        act_p, taps_p = ffn_up(h2, ffn_w_up[i], ffn_conv_w[i], ffn_conv_b[i], jnp.zeros((bp, 2, 2 * D_FF), F32),
                               time_major=False, row0=0, m=mp, tm=512)
        act_s, taps_s = ffn_up(h2[mp:].reshape(nb, ts, d), ffn_w_up[i], ffn_conv_w[i], ffn_conv_b[i],
                               state_ffn_conv[i], time_major=True)
        x = matmul_split([act_p], [act_s.reshape(ms, D_FF)], ffn_w_down[i], tn=512, tm=256, tiles=(x,),
                         epilogue=lambda acc, res: res + acc, name="ffn_down")

        e = matmul(p_all[i], ple_w_proj[i], tn=d, tm=256, rows=(ple_norm_e[i],),
                   epilogue=lambda acc, g: acc * lax.rsqrt(jnp.mean(acc * acc, axis=-1, keepdims=True) + EPS) * g,
                   name="ple_proj")
        h3 = rmsnorm_rows(x, norm_ple[i], BF16)
        x = matmul(h3, ple_w_gate[i], tn=512, tiles=(x, e),
                   epilogue=lambda acc, res, ee: res + ee * jax.nn.sigmoid(acc), name="ple_gate")

        outs["k_p"].append(k_new[:mp].reshape(bp, tp, DIFF_HEADS, 2 * DIFF_QD))
        outs["v_p"].append(v_new[:mp].reshape(bp, tp, DIFF_HEADS, DIFF_VD))
        outs["k_s"].append(k_new[mp:].reshape(nb, ts, DIFF_HEADS, 2 * DIFF_QD))
        outs["v_s"].append(v_new[mp:].reshape(nb, ts, DIFF_HEADS, DIFF_VD))
        outs["ret_p"].append(s_ret_p)
        outs["ret_s"].append(s_ret_s)
        outs["rwkv_p"].append(_rwkv_state_from_pairs(s_rwkv_p))
        outs["rwkv_s"].append(_rwkv_state_from_pairs(s_rwkv_s))
        outs["sh_p"].append(c_rwkv[mp - 1:mp].reshape(bp, RWKV_COLS))
        outs["sh_s"].append(cr_s[:, -1])
        outs["s5r_p"].append(s5r_p.reshape(bp, S5_GROUPS, S5_N))
        outs["s5i_p"].append(s5i_p.reshape(bp, S5_GROUPS, S5_N))
        outs["s5r_s"].append(s5r_s.reshape(nb, S5_GROUPS, S5_N))
        outs["s5i_s"].append(s5i_s.reshape(nb, S5_GROUPS, S5_N))
        outs["conv_p"].append(conv_state_from_taps(taps_p))
        outs["conv_s"].append(conv_state_from_taps(taps_s))

    y = rmsnorm_rows(x, norm_final, F32)
    st = lambda k: jnp.stack(outs[k])
    return (y[:mp].reshape(bp, tp, d), y[mp:].reshape(nb, ts, d),
            st("k_p"), st("v_p"), st("k_s"), st("v_s"), st("ret_p"), st("ret_s"), st("rwkv_p"), st("rwkv_s"),
            st("sh_p"), st("sh_s"), st("s5r_p"), st("s5i_p"), st("s5r_s"), st("s5i_s"), st("conv_p"), st("conv_s"))
```

```python
import functools
import math

import jax
import jax.numpy as jnp
import numpy as np
from jax import lax
from jax.experimental import pallas as pl
from jax.experimental.pallas import tpu as pltpu

F32 = jnp.float32
BF16 = jnp.bfloat16

D_MODEL = 2048
GROUP_W = 512
RET_HEADS = 4
RET_HD = 128
RET_CHUNK = 128
RET_ROPE_BASE = 10000.0
DIFF_HEADS = 4
DIFF_VD = 128
DIFF_QD = 64
ROPE_THETA = 500000.0
ROPE_DIM = 16
PAGE_SIZE = 128
RWKV_HD = 64
RWKV_HEADS = 8
RWKV_LN_EPS = 64e-5
S5_CH = 16
S5_GROUPS = 32
S5_N = 64
D_FF = 5632
EPS = 1e-6
NEG_INF = -1e30
LOG2_E = math.log2(math.e)

RET_COLS = 4 * GROUP_W
DIFF_COLS = 3 * GROUP_W
RWKV_COLS = 3 * GROUP_W + 64 + 64 + 128
S5_COLS = GROUP_W

LANES = 128
SUBLANES = 8
VMEM_LIMIT_BYTES = 52 * 1024 * 1024


def _cparams(sem, vmem=VMEM_LIMIT_BYTES):
    return pltpu.CompilerParams(dimension_semantics=sem, vmem_limit_bytes=vmem)


def _split_bf16(x):
    hi = x.astype(BF16)
    lo = (x - hi.astype(F32)).astype(BF16)
    return hi, lo


def _dot_hilo(x, w_bf16):
    hi, lo = _split_bf16(x)
    return (jnp.dot(hi, w_bf16, preferred_element_type=F32)
            + jnp.dot(lo, w_bf16, preferred_element_type=F32))


def _dot3(x, w_hi, w_lo):
    hi, lo = _split_bf16(x)
    return (jnp.dot(hi, w_hi, preferred_element_type=F32)
            + jnp.dot(hi, w_lo, preferred_element_type=F32)
            + jnp.dot(lo, w_hi, preferred_element_type=F32))


def _rmsnorm_kernel(x_ref, g_ref, o_ref):
    x = x_ref[...]
    y = x * lax.rsqrt(jnp.mean(x * x, axis=-1, keepdims=True) + EPS)
    o_ref[...] = (y * g_ref[...]).astype(o_ref.dtype)


def rmsnorm_rows(x, g, out_dtype, tm=512):
    m, d = x.shape
    return pl.pallas_call(
        _rmsnorm_kernel,
        grid=(m // tm,),
        in_specs=[pl.BlockSpec((tm, d), lambda i: (i, 0)),
                  pl.BlockSpec((1, d), lambda i: (0, 0))],
        out_specs=pl.BlockSpec((tm, d), lambda i: (i, 0)),
        out_shape=jax.ShapeDtypeStruct((m, d), out_dtype),
        compiler_params=_cparams(("parallel",)),
        name="rmsnorm_rows",
    )(x, g.reshape(1, d))


def _mm_kernel(x_ref, w_ref, *rest, epilogue, n_extra):
    extra = rest[:n_extra]
    o_ref, wb_ref = rest[n_extra], rest[n_extra + 1]

    @pl.when(pl.program_id(1) == 0)
    def _():
        wb_ref[...] = w_ref[...].astype(BF16)

    acc = jnp.dot(x_ref[...].astype(BF16), wb_ref[...], preferred_element_type=F32)
    if epilogue is not None:
        acc = epilogue(acc, *[e[...] for e in extra])
    o_ref[...] = acc.astype(o_ref.dtype)


def matmul(x, w, *, tn, tm=512, col0=0, ncols=None, epilogue=None, tiles=(), rows=(),
           out_dtype=F32, name="matmul"):
    m, k = x.shape
    n = w.shape[1] - col0 if ncols is None else ncols
    assert col0 % tn == 0 and n % tn == 0 and m % tm == 0
    joff = col0 // tn
    in_specs = [pl.BlockSpec((tm, k), lambda j, i: (i, 0)),
                pl.BlockSpec((k, tn), lambda j, i: (0, j + joff))]
    in_specs += [pl.BlockSpec((tm, tn), lambda j, i: (i, j)) for _ in tiles]
    in_specs += [pl.BlockSpec((1, tn), lambda j, i: (0, j)) for _ in rows]
    kern = functools.partial(_mm_kernel, epilogue=epilogue, n_extra=len(tiles) + len(rows))
    return pl.pallas_call(
        kern,
        grid=(n // tn, m // tm),
        in_specs=in_specs,
        out_specs=pl.BlockSpec((tm, tn), lambda j, i: (i, j)),
        out_shape=jax.ShapeDtypeStruct((m, n), out_dtype),
        scratch_shapes=[pltpu.VMEM((k, tn), BF16)],
        compiler_params=_cparams(("parallel", "arbitrary")),
        name=name,
    )(x, w, *tiles, *[r.reshape(1, -1) for r in rows])


def _mm_split_kernel(*refs, n_parts, part_k, n_prompt_tiles, epilogue, n_extra):
    xp = refs[:n_parts]
    xs = refs[n_parts:2 * n_parts]
    w_ref = refs[2 * n_parts]
    extra = refs[2 * n_parts + 1:2 * n_parts + 1 + n_extra]
    o_ref, wb_ref = refs[2 * n_parts + 1 + n_extra], refs[2 * n_parts + 2 + n_extra]
    i = pl.program_id(1)

    @pl.when(i == 0)
    def _():
        wb_ref[...] = w_ref[...].astype(BF16)

    def body(parts):
        acc = None
        for g, r in enumerate(parts):
            d = jnp.dot(r[...], wb_ref[g * part_k:(g + 1) * part_k, :], preferred_element_type=F32)
            acc = d if acc is None else acc + d
        o_ref[...] = epilogue(acc, *[e[...] for e in extra]).astype(o_ref.dtype)

    @pl.when(i < n_prompt_tiles)
    def _():
        body(xp)

    @pl.when(i >= n_prompt_tiles)
    def _():
        body(xs)


def matmul_split(xs_prompt, xs_sample, w, *, tn, tm, epilogue, tiles=(), out_dtype=F32, name="matmul_split"):
    n_parts = len(xs_prompt)
    mp, part_k = xs_prompt[0].shape
    ms = xs_sample[0].shape[0]
    k, n = w.shape
    assert k == n_parts * part_k and mp % tm == 0 and ms % tm == 0 and n % tn == 0
    npt, nst = mp // tm, ms // tm
    in_specs = [pl.BlockSpec((tm, part_k), lambda j, i: (jnp.minimum(i, npt - 1), 0)) for _ in xs_prompt]
    in_specs += [pl.BlockSpec((tm, part_k), lambda j, i: (jnp.maximum(i - npt, 0), 0)) for _ in xs_sample]
    in_specs += [pl.BlockSpec((k, tn), lambda j, i: (0, j))]
    in_specs += [pl.BlockSpec((tm, tn), lambda j, i: (i, j)) for _ in tiles]
    kern = functools.partial(_mm_split_kernel, n_parts=n_parts, part_k=part_k, n_prompt_tiles=npt,
                             epilogue=epilogue, n_extra=len(tiles))
    return pl.pallas_call(
        kern,
        grid=(n // tn, npt + nst),
        in_specs=in_specs,
        out_specs=pl.BlockSpec((tm, tn), lambda j, i: (i, j)),
        out_shape=jax.ShapeDtypeStruct((mp + ms, n), out_dtype),
        scratch_shapes=[pltpu.VMEM((k, tn), BF16)],
        compiler_params=_cparams(("parallel", "arbitrary")),
        name=name,
    )(*xs_prompt, *xs_sample, w, *tiles)


def _ret_kernel(q_ref, k_ref, v_ref, g_ref, cos_ref, sin_ref, dmask_ref, qdec_ref, kdec_ref,
                cdec_ref, nw_ref, nb_ref, s0_ref, o_ref, sn_ref, s_scr, *, chunk, n_chunks):
    c = pl.program_id(1)

    @pl.when(c == 0)
    def _():
        s_scr[...] = s0_ref[...]

    cos = cos_ref[...]
    sin = sin_ref[...]
    for h in range(RET_HEADS):
        hs = slice(h * RET_HD, (h + 1) * RET_HD)
        qh = q_ref[:, hs]
        kh = k_ref[:, hs]
        qr = qh * cos + pltpu.roll(qh, RET_HD // 2, axis=1) * sin
        kr = (kh * cos + pltpu.roll(kh, RET_HD // 2, axis=1) * sin) * (RET_HD ** -0.5)
        vh = v_ref[:, hs]
        s = s_scr[h]
        outs = []
        for ci in range(n_chunks):
            rs = slice(ci * chunk, (ci + 1) * chunk)
            qc, kc, vc = qr[rs], kr[rs], vh[rs].astype(BF16)
            att = lax.dot_general(qc.astype(BF16), kc.astype(BF16), (((1,), (1,)), ((), ())),
                                  preferred_element_type=F32) * dmask_ref[h]
            o = jnp.dot(att.astype(BF16), vc, preferred_element_type=F32)
            o += jnp.dot((qc * qdec_ref[h]).astype(BF16), s.astype(BF16), preferred_element_type=F32)
            kd = (kc * kdec_ref[h]).astype(BF16)
            s = s * cdec_ref[h, 0:1, :] + lax.dot_general(kd, vc, (((0,), (0,)), ((), ())),
                                                  preferred_element_type=F32)
            outs.append(o)
        s_scr[h] = s
        o = outs[0] if n_chunks == 1 else jnp.concatenate(outs, axis=0)
        mu = jnp.mean(o, axis=-1, keepdims=True)
        var = jnp.mean(jnp.square(o - mu), axis=-1, keepdims=True)
        o = (o - mu) * lax.rsqrt(var + EPS) * nw_ref[h:h + 1, :] + nb_ref[h:h + 1, :]
        gh = g_ref[:, hs]
        o_ref[:, hs] = (o * (gh * jax.nn.sigmoid(gh))).astype(o_ref.dtype)

    @pl.when(c == pl.num_programs(1) - 1)
    def _():
        sn_ref[...] = s_scr[...]


def retention(proj, row0, n_seq, t_seq, cos, sin, s0, norm_w, norm_b, *, chunk, chunks_per_step):
    rb = chunk * chunks_per_step
    steps = t_seq // rb
    assert row0 % rb == 0 and t_seq % rb == 0
    b0 = row0 // rb
    log_g = jnp.log1p(-jnp.exp2(-5.0 - jnp.arange(RET_HEADS, dtype=F32)))
    idx = jnp.arange(chunk, dtype=F32)
    rel = idx[:, None] - idx[None, :]
    dmask = jnp.where(rel >= 0, jnp.exp(log_g[:, None, None] * jnp.maximum(rel, 0.0)), 0.0)
    ones = jnp.ones((1, 1, RET_HD), F32)
    qdec = jnp.exp(log_g[:, None] * (idx + 1.0))[:, :, None] * ones
    kdec = jnp.exp(log_g[:, None] * (chunk - 1.0 - idx))[:, :, None] * ones
    cdec = jnp.exp(log_g * chunk)[:, None, None] * jnp.ones((1, SUBLANES, RET_HD), F32)

    def col(j):
        return pl.BlockSpec((rb, GROUP_W), lambda s, c: (b0 + s * steps + c, j))

    full = lambda shape: pl.BlockSpec(shape, lambda s, c: (0,) * len(shape))
    kern = functools.partial(_ret_kernel, chunk=chunk, n_chunks=chunks_per_step)
    return pl.pallas_call(
        kern,
        grid=(n_seq, steps),
        in_specs=[col(0), col(1), col(2), col(3),
                  pl.BlockSpec((rb, RET_HD), lambda s, c: (c, 0)),
                  pl.BlockSpec((rb, RET_HD), lambda s, c: (c, 0)),
                  full((RET_HEADS, chunk, chunk)),
                  full((RET_HEADS, chunk, RET_HD)),
                  full((RET_HEADS, chunk, RET_HD)),
                  full((RET_HEADS, SUBLANES, RET_HD)),
                  full((RET_HEADS, RET_HD)),
                  full((RET_HEADS, RET_HD)),
                  pl.BlockSpec((None, RET_HEADS, RET_HD, RET_HD), lambda s, c: (s, 0, 0, 0))],
        out_specs=[pl.BlockSpec((rb, GROUP_W), lambda s, c: (s * steps + c, 0)),
                   pl.BlockSpec((None, RET_HEADS, RET_HD, RET_HD), lambda s, c: (s, 0, 0, 0))],
        out_shape=[jax.ShapeDtypeStruct((n_seq * t_seq, GROUP_W), BF16),
                   jax.ShapeDtypeStruct((n_seq, RET_HEADS, RET_HD, RET_HD), F32)],
        scratch_shapes=[pltpu.VMEM((RET_HEADS, RET_HD, RET_HD), F32)],
        compiler_params=_cparams(("arbitrary", "arbitrary")),
        name="retention",
    )(proj, proj, proj, proj, cos, sin, dmask, qdec, kdec, cdec, norm_w, norm_b, s0)


def _diff_prep_kernel(q_ref, k_ref, v_ref, c_ref, s1_ref, s2_ref, q0_ref, q1_ref, kn_ref, kb_ref, vt_ref):
    c, s1, s2 = c_ref[...], s1_ref[...], s2_ref[...]
    lane = lax.broadcasted_iota(jnp.int32, c.shape, 1)
    lo = lane < DIFF_QD
    for j in range(GROUP_W // LANES):
        cs = slice(j * LANES, (j + 1) * LANES)
        q = q_ref[:, cs]
        k = k_ref[:, cs]
        qr = q * c + pltpu.roll(q, LANES - ROPE_DIM // 2, axis=1) * s1 + pltpu.roll(q, ROPE_DIM // 2, axis=1) * s2
        kr = k * c + pltpu.roll(k, LANES - ROPE_DIM // 2, axis=1) * s1 + pltpu.roll(k, ROPE_DIM // 2, axis=1) * s2
        qr = qr * (DIFF_QD ** -0.5 * LOG2_E)
        q0_ref[:, cs] = jnp.where(lo, qr, 0.0).astype(BF16)
        q1_ref[:, cs] = jnp.where(lo, 0.0, qr).astype(BF16)
        kn_ref[:, cs] = kr
        kb_ref[:, cs] = kr.astype(BF16)
    vt_ref[...] = v_ref[...].T.astype(BF16)


def diff_prep(proj, c, s1, s2, tm=512):
    m = proj.shape[0]
    col = lambda j: pl.BlockSpec((tm, GROUP_W), lambda i: (i, j))
    tab = pl.BlockSpec((tm, LANES), lambda i: (i, 0))
    out = pl.BlockSpec((tm, GROUP_W), lambda i: (i, 0))
    return pl.pallas_call(
        _diff_prep_kernel,
        grid=(m // tm,),
        in_specs=[col(0), col(1), col(2), tab, tab, tab],
        out_specs=[out] * 4 + [pl.BlockSpec((GROUP_W, tm), lambda i: (0, i))],
        out_shape=[jax.ShapeDtypeStruct((m, GROUP_W), BF16), jax.ShapeDtypeStruct((m, GROUP_W), BF16),
                   jax.ShapeDtypeStruct((m, GROUP_W), F32), jax.ShapeDtypeStruct((m, GROUP_W), BF16),
                   jax.ShapeDtypeStruct((GROUP_W, m), BF16)],
        compiler_params=_cparams(("parallel",)),
        name="diff_prep",
    )(proj, proj, proj, c, s1, s2)


def _diff_flash_kernel(qi_ref, ki_ref, lam_ref, q0_ref, q1_ref, k_ref, vt_ref, sub_ref, o_ref,
                       m0, l0, a0, m1, l1, a1, *, blk, out_scale):
    step = pl.program_id(1)
    qi = qi_ref[step]
    ki = ki_ref[step]

    @pl.when(ki == 0)
    def _():
        m0[...] = jnp.full(m0.shape, NEG_INF, F32)
        m1[...] = jnp.full(m1.shape, NEG_INF, F32)
        l0[...] = jnp.zeros(l0.shape, F32)
        l1[...] = jnp.zeros(l1.shape, F32)
        a0[...] = jnp.zeros(a0.shape, F32)
        a1[...] = jnp.zeros(a1.shape, F32)

    def update(q_ref, m_ref, l_ref, a_ref, diagonal):
        st = lax.dot_general(k_ref[...], q_ref[...], (((1,), (1,)), ((), ())), preferred_element_type=F32)
        if diagonal:
            kpos = lax.broadcasted_iota(jnp.int32, (blk, blk), 0)
            qpos = lax.broadcasted_iota(jnp.int32, (blk, blk), 1)
            st = jnp.where(kpos <= qpos, st, NEG_INF)
        m_prev = m_ref[...]
        m_new = jnp.maximum(m_prev, jnp.max(st, axis=0, keepdims=True))
        alpha = jnp.exp2(m_prev - m_new)
        pt = jnp.exp2(st - m_new)
        l_ref[...] = alpha * l_ref[...] + jnp.sum(pt, axis=0, keepdims=True)
        a_ref[...] = alpha * a_ref[...] + jnp.dot(vt_ref[...], pt.astype(BF16), preferred_element_type=F32)
        m_ref[...] = m_new

    @pl.when(ki < qi)
    def _():
        update(q0_ref, m0, l0, a0, False)
        update(q1_ref, m1, l1, a1, False)

    @pl.when(ki == qi)
    def _():
        update(q0_ref, m0, l0, a0, True)
        update(q1_ref, m1, l1, a1, True)
        ot = a0[...] / l0[...] - lam_ref[0, 0] * (a1[...] / l1[...])
        yt = ot * lax.rsqrt(jnp.mean(ot * ot, axis=0, keepdims=True) + EPS) * (sub_ref[...] * out_scale)
        o_ref[...] = yt.T.astype(o_ref.dtype)


def diff_attention_prompt(q0, q1, kb, vt, lam, subln, lam_init, *, t, blk=512):
    nb = t // blk
    qi_tbl = np.concatenate([np.full(i + 1, i, np.int32) for i in range(nb)])
    ki_tbl = np.concatenate([np.arange(i + 1, dtype=np.int32) for i in range(nb)])
    kern = functools.partial(_diff_flash_kernel, blk=blk, out_scale=1.0 - lam_init)
    grid_spec = pltpu.PrefetchScalarGridSpec(
        num_scalar_prefetch=2,
        grid=(DIFF_HEADS, len(qi_tbl)),
        in_specs=[pl.BlockSpec(memory_space=pltpu.SMEM),
                  pl.BlockSpec((blk, LANES), lambda h, s, qi, ki: (qi[s], h)),
                  pl.BlockSpec((blk, LANES), lambda h, s, qi, ki: (qi[s], h)),
                  pl.BlockSpec((blk, LANES), lambda h, s, qi, ki: (ki[s], h)),
                  pl.BlockSpec((LANES, blk), lambda h, s, qi, ki: (h, ki[s])),
                  pl.BlockSpec((DIFF_VD, 1), lambda h, s, qi, ki: (0, 0))],
        out_specs=pl.BlockSpec((blk, LANES), lambda h, s, qi, ki: (qi[s], h)),
        scratch_shapes=[pltpu.VMEM((1, blk), F32), pltpu.VMEM((1, blk), F32), pltpu.VMEM((DIFF_VD, blk), F32),
                        pltpu.VMEM((1, blk), F32), pltpu.VMEM((1, blk), F32), pltpu.VMEM((DIFF_VD, blk), F32)],
    )
    return pl.pallas_call(
        kern,
        grid_spec=grid_spec,
        out_shape=jax.ShapeDtypeStruct((t, GROUP_W), BF16),
        compiler_params=_cparams(("arbitrary", "arbitrary")),
        name="diff_attention_prompt",
    )(jnp.asarray(qi_tbl), jnp.asarray(ki_tbl), lam.reshape(1, 1), q0, q1, kb, vt, subln.reshape(DIFF_VD, 1))


def _diff_paged_kernel(pt_ref, lam_ref, qbd_ref, kn_ref, vn_ref, sub_ref, *rest, n_pages, t_new, out_scale):
    k_pages = rest[:n_pages]
    v_pages = rest[n_pages:2 * n_pages]
    o_ref = rest[2 * n_pages]
    kpad, vpad, s_scr = rest[2 * n_pages + 1:]
    half = DIFF_HEADS * t_new
    b = pl.program_id(0)

    @pl.when(b == 0)
    def _():
        kpad[...] = jnp.zeros(kpad.shape, kpad.dtype)
        vpad[...] = jnp.zeros(vpad.shape, vpad.dtype)

    kpad[0:t_new, :] = kn_ref[...]
    vpad[0:t_new, :] = vn_ref[...]

    def heads_to_lanes(ref):
        return jnp.concatenate([ref[pl.ds(h, PAGE_SIZE, stride=DIFF_HEADS), :] for h in range(DIFF_HEADS)],
                               axis=1).astype(BF16)

    qbd = qbd_ref[...]
    key = lax.broadcasted_iota(jnp.int32, (PAGE_SIZE, LANES), 0)
    qpos = lax.broadcasted_iota(jnp.int32, (PAGE_SIZE, LANES), 1) & (t_new - 1)
    m = jnp.full((1, LANES), NEG_INF, F32)
    for p in range(n_pages + 1):
        if p < n_pages:
            s = jnp.dot(heads_to_lanes(k_pages[p]), qbd, preferred_element_type=F32)
        else:
            s = jnp.dot(kpad[...].astype(BF16), qbd, preferred_element_type=F32)
            s = jnp.where(key <= qpos, s, NEG_INF)
        s_scr[p] = s
        m = jnp.maximum(m, jnp.max(s, axis=0, keepdims=True))
    l = jnp.zeros((1, LANES), F32)
    acc = jnp.zeros((LANES, GROUP_W), F32)
    for p in range(n_pages + 1):
        e = jnp.exp2(s_scr[p] - m)
        l = l + jnp.sum(e, axis=0, keepdims=True)
        vsrc = heads_to_lanes(v_pages[p]) if p < n_pages else vpad[...].astype(BF16)
        acc += lax.dot_general(e.astype(BF16), vsrc, (((0,), (0,)), ((), ())), preferred_element_type=F32)
    inv_col = jnp.broadcast_to(1.0 / l, (LANES, LANES)).T
    lam = lam_ref[0, 0]
    outs = []
    for h in range(DIFF_HEADS):
        r0, r1 = h * t_new, half + h * t_new
        cs = slice(h * DIFF_VD, (h + 1) * DIFF_VD)
        o = acc[r0:r0 + t_new, cs] * inv_col[r0:r0 + t_new, :] - lam * (acc[r1:r1 + t_new, cs] * inv_col[r1:r1 + t_new, :])
        y = o * lax.rsqrt(jnp.mean(o * o, axis=-1, keepdims=True) + EPS)
        outs.append(y * sub_ref[...] * out_scale)
    o_ref[...] = jnp.concatenate(outs, axis=1).astype(o_ref.dtype)


def diff_attention_sample(qs, kn, vn, cache_k, cache_v, layer, page_table, lam, subln, lam_init):
    nb, t_new, _ = qs.shape
    n_pages = page_table.shape[1]
    rows = PAGE_SIZE * DIFF_HEADS
    assert t_new & (t_new - 1) == 0
    qt = jnp.transpose(qs, (0, 2, 1))
    cblk = jnp.arange(GROUP_W) // DIFF_QD
    col_h = jnp.arange(DIFF_HEADS)
    parts = []
    for mp in range(2):
        sel = (cblk[:, None] == (2 * col_h + mp)[None, :]).astype(F32)
        parts.append((qt[:, :, None, :] * sel[None, :, :, None]).reshape(nb, GROUP_W, DIFF_HEADS * t_new))
    pad = jnp.zeros((nb, GROUP_W, LANES - 2 * DIFF_HEADS * t_new), F32)
    qbd = jnp.concatenate(parts + [pad], axis=-1).astype(BF16)

    def page_spec(p):
        return pl.BlockSpec((None, None, rows, DIFF_VD), lambda b, pt: (layer, pt[b, p], 0, 0))

    kern = functools.partial(_diff_paged_kernel, n_pages=n_pages, t_new=t_new, out_scale=1.0 - lam_init)
    grid_spec = pltpu.PrefetchScalarGridSpec(
        num_scalar_prefetch=1,
        grid=(nb,),
        in_specs=[pl.BlockSpec(memory_space=pltpu.SMEM),
                  pl.BlockSpec((None, GROUP_W, LANES), lambda b, pt: (b, 0, 0)),
                  pl.BlockSpec((t_new, GROUP_W), lambda b, pt: (b, 0)),
                  pl.BlockSpec((t_new, GROUP_W), lambda b, pt: (b, 0)),
                  pl.BlockSpec((1, LANES), lambda b, pt: (0, 0))]
                 + [page_spec(p) for p in range(n_pages)] * 2,
        out_specs=pl.BlockSpec((t_new, GROUP_W), lambda b, pt: (b, 0)),
        scratch_shapes=[pltpu.VMEM((PAGE_SIZE, GROUP_W), F32), pltpu.VMEM((PAGE_SIZE, GROUP_W), F32),
                        pltpu.VMEM((n_pages + 1, PAGE_SIZE, LANES), F32)],
    )
    return pl.pallas_call(
        kern,
        grid_spec=grid_spec,
        out_shape=jax.ShapeDtypeStruct((nb * t_new, GROUP_W), BF16),
        compiler_params=_cparams(("arbitrary",)),
        name="diff_attention_sample",
    )(page_table, lam.reshape(1, 1), qbd, kn, vn, subln.reshape(1, LANES),
      *([cache_k] * n_pages), *([cache_v] * n_pages))


def _interleave64(x, y):
    lane = lax.broadcasted_iota(jnp.int32, (x.shape[0], LANES), 1)
    lo = lane < RWKV_HD
    blocks = []
    for c in range(GROUP_W // LANES):
        xc = x[:, c * LANES:(c + 1) * LANES]
        yc = y[:, c * LANES:(c + 1) * LANES]
        rx = pltpu.roll(xc, RWKV_HD, axis=1)
        ry = pltpu.roll(yc, RWKV_HD, axis=1)
        blocks.append(jnp.where(lo, xc, ry))
        blocks.append(jnp.where(lo, rx, yc))
    return jnp.concatenate(blocks, axis=1)


def _rwkv_prep_kernel(cols_ref, prev_ref, mu_ref, w0_ref, a0_ref, kk_ref, ka_ref, rk_ref,
                      w2h_ref, w2l_ref, a2h_ref, a2l_ref, g2h_ref, g2l_ref, e_ref,
                      ab_ref, kr2_ref, g8_ref, v_ref, br_ref, kr_ref, bon_ref, g_ref, *, shifted):
    cols = cols_ref[...]
    if shifted:
        before = jnp.where(pl.program_id(0) == 0, 0.0, prev_ref[SUBLANES - 1:SUBLANES, :])
        first = lax.broadcasted_iota(jnp.int32, cols.shape, 0) == 0
        prev = jnp.where(first, before, pltpu.roll(cols, 1, axis=0))
    else:
        prev = prev_ref[...]
    xm = cols + (prev - cols) * mu_ref[...]
    o1 = GROUP_W
    r, k, v = xm[:, 0:o1], xm[:, o1:2 * o1], xm[:, 2 * o1:3 * o1]
    lora = xm[:, 3 * o1:3 * o1 + LANES]
    gl = xm[:, 3 * o1 + LANES:3 * o1 + 2 * LANES]
    wterm = _dot3(jnp.tanh(lora), w2h_ref[...], w2l_ref[...])
    aterm = _dot3(lora, a2h_ref[...], a2l_ref[...])
    z = -(w0_ref[...] + wterm)
    softplus = jnp.maximum(z, 0.0) + jnp.log1p(jnp.exp(-jnp.abs(z)))
    w = -softplus - 0.5
    log_decay = -jnp.exp(w)
    a = jax.nn.sigmoid(a0_ref[...] + aterm)
    g = _dot3(jax.nn.sigmoid(gl), g2h_ref[...], g2l_ref[...])
    e = e_ref[...]
    kk = k * kk_ref[...]
    kk = kk * lax.rsqrt(jnp.maximum(_dot_hilo(kk * kk, e), 1e-24))
    k2 = k * (1.0 + (a - 1.0) * ka_ref[...])
    bv = kk * a
    n = cols.shape[0]
    sub = lax.broadcasted_iota(jnp.int32, log_decay.shape, 0) & (SUBLANES - 1)
    csum = log_decay
    rsum = log_decay
    for d in (1, 2, 4):
        csum = csum + jnp.where(sub >= d, pltpu.roll(csum, d, axis=0), 0.0)
        rsum = rsum + jnp.where(sub < SUBLANES - d, pltpu.roll(rsum, n - d, axis=0), 0.0)
    gamma = jnp.exp(csum)
    inv_gamma = jnp.exp(-csum)
    rnd = lambda t: t.astype(BF16).astype(F32)
    ab_ref[...] = _interleave64(rnd(jnp.exp(csum - log_decay) * (-kk)), rnd(bv * inv_gamma))
    kr2_ref[...] = _interleave64(rnd(k2 * inv_gamma), rnd(r * gamma))
    g8_ref[...] = jnp.exp(csum + rsum - log_decay)
    v_ref[...] = v
    br_ref[...] = _dot_hilo(bv * r, e)
    kr_ref[...] = _dot_hilo(k2 * r, e)
    bon_ref[...] = _dot_hilo(r * k2 * rk_ref[...], e)
    g_ref[...] = g


def rwkv_prep(cols, prev, p, e512, *, row0, m, tm=256):
    assert row0 % tm == 0 and m % tm == 0
    b0 = row0 // tm
    shifted = prev is None
    if shifted:
        prev_arr = cols
        per8 = tm // SUBLANES
        prev_spec = pl.BlockSpec((SUBLANES, RWKV_COLS), lambda i: (jnp.maximum((b0 + i) * per8 - 1, 0), 0))
    else:
        prev_arr = prev
        prev_spec = pl.BlockSpec((tm, RWKV_COLS), lambda i: (i, 0))
    row = lambda n: pl.BlockSpec((1, n), lambda i: (0, 0))
    full = lambda a: pl.BlockSpec(a.shape, lambda i: (0, 0))
    wide = pl.BlockSpec((tm, 2 * GROUP_W), lambda i: (i, 0))
    nar = pl.BlockSpec((tm, GROUP_W), lambda i: (i, 0))
    z64 = jnp.zeros((64, GROUP_W), F32)
    w2p = jnp.concatenate([p["w2"], z64], axis=0)
    a2p = jnp.concatenate([z64, p["a2"]], axis=0)
    w2h, w2l = _split_bf16(w2p)
    a2h, a2l = _split_bf16(a2p)
    g2h, g2l = _split_bf16(p["g2"])
    mats = [w2h, w2l, a2h, a2l, g2h, g2l, e512]
    return pl.pallas_call(
        functools.partial(_rwkv_prep_kernel, shifted=shifted),
        grid=(m // tm,),
        in_specs=[pl.BlockSpec((tm, RWKV_COLS), lambda i: (b0 + i, 0)),
                  prev_spec,
                  row(RWKV_COLS), row(GROUP_W), row(GROUP_W), row(GROUP_W), row(GROUP_W), row(GROUP_W)]
                 + [full(a) for a in mats],
        out_specs=[wide, wide, nar, nar, nar, nar, nar, nar],
        out_shape=[jax.ShapeDtypeStruct((m, 2 * GROUP_W), F32)] * 2
                  + [jax.ShapeDtypeStruct((m, GROUP_W), F32)] * 6,
        compiler_params=_cparams(("parallel",)),
        name="rwkv_prep",
    )(cols, prev_arr, p["mu"].reshape(1, -1), p["w0"].reshape(1, -1), p["a0"].reshape(1, -1),
      p["kk"].reshape(1, -1), p["ka"].reshape(1, -1), p["rk"].reshape(1, -1), *mats)


def _rwkv_scan_kernel(ab_ref, kr2_ref, g8_ref, v_ref, br_ref, kr_ref, bon_ref, g_ref, lnw_ref, lnb_ref,
                      e_ref, s0_ref, o_ref, sn_ref, s_scr, y_scr, *, n_seq, t_len, pairs):
    c = pl.program_id(2)
    half = RWKV_HD
    lane_lo = lax.broadcasted_iota(jnp.int32, (half, LANES), 1) < half

    def tokens8(g, states):
        r0 = pl.multiple_of(g * SUBLANES, SUBLANES)
        rows = pl.ds(r0, SUBLANES)
        ab8, kr28, g8 = ab_ref[rows, :], kr2_ref[rows, :], g8_ref[rows, :]
        v8, br8, kr8 = v_ref[rows, :], br_ref[rows, :], kr_ref[rows, :]
        states = list(states)
        ys = [[] for _ in range(pairs)]

        def col_forms(tile, k, pp):
            top = jnp.broadcast_to(tile[k:k + 1, 2 * pp * LANES:(2 * pp + 1) * LANES], (half, LANES))
            bot = jnp.broadcast_to(tile[k:k + 1, (2 * pp + 1) * LANES:(2 * pp + 2) * LANES], (half, LANES))
            t = jnp.concatenate([top, bot], axis=0).astype(BF16).T
            return t[0:half].astype(F32), t[half:2 * half].astype(F32)

        for k in range(SUBLANES):
            for pp in range(pairs):
                s = states[pp]
                ls = slice(pp * LANES, (pp + 1) * LANES)
                ac, bc = col_forms(ab8, k, pp)
                kc, rc = col_forms(kr28, k, pp)
                u = jnp.sum(s * ac, axis=0, keepdims=True)
                yp = jnp.sum(s * rc, axis=0, keepdims=True)
                vrow = v8[k:k + 1, ls]
                ys[pp].append(yp + u * br8[k:k + 1, ls] + vrow * kr8[k:k + 1, ls])
                states[pp] = s + bc * u + kc * vrow
        for pp in range(pairs):
            ls = slice(pp * LANES, (pp + 1) * LANES)
            y_scr[rows, ls] = jnp.concatenate(ys[pp], axis=0)
            gt = jnp.broadcast_to(g8[0:1, ls], (LANES, LANES)).T
            states[pp] = states[pp] * jnp.where(lane_lo, gt[0:half], gt[half:2 * half])
        return tuple(states)

    @pl.when(c == 0)
    def _():
        s_scr[...] = s0_ref[...]

    groups = t_len // SUBLANES

    def seq(si, carry):
        states = tuple(s_scr[si, pp] for pp in range(pairs))
        states = lax.fori_loop(0, groups, lambda g, st: tokens8(si * groups + g, st), states)
        for pp in range(pairs):
            s_scr[si, pp] = states[pp]
        return carry

    lax.fori_loop(0, n_seq, seq, 0)

    e = e_ref[...]
    for pp in range(pairs):
        ls = slice(pp * LANES, (pp + 1) * LANES)
        y = y_scr[:, ls]
        mu = _dot_hilo(y, e) * (1.0 / RWKV_HD)
        d = y - mu
        var = _dot_hilo(d * d, e) * (1.0 / RWKV_HD)
        yn = d * lax.rsqrt(var + RWKV_LN_EPS) * lnw_ref[:, ls] + lnb_ref[:, ls]
        o_ref[:, ls] = ((yn + bon_ref[:, ls] * v_ref[:, ls]) * g_ref[:, ls]).astype(o_ref.dtype)

    @pl.when(c == pl.num_programs(2) - 1)
    def _():
        sn_ref[...] = s_scr[...]


def rwkv_scan(ab, kr2, g8, v, br, kr, bon, g, ln_w, ln_b, e128, s0, *, t_seq, seq_per_step, t_step, pairs=2):
    n_seq = s0.shape[0]
    chunks = t_seq // t_step
    rb = seq_per_step * t_step
    assert seq_per_step == 1 or chunks == 1
    n_pairs = RWKV_HEADS // 2
    assert n_pairs % pairs == 0
    w = pairs * LANES

    def rows(width):
        return pl.BlockSpec((rb, width), lambda p, s, c: (s * chunks + c, p))

    kern = functools.partial(_rwkv_scan_kernel, n_seq=seq_per_step, t_len=t_step, pairs=pairs)
    st = pl.BlockSpec((seq_per_step, pairs, RWKV_HD, LANES), lambda p, s, c: (s, p, 0, 0))
    return pl.pallas_call(
        kern,
        grid=(n_pairs // pairs, n_seq // seq_per_step, chunks),
        in_specs=[rows(2 * w), rows(2 * w),
                  rows(w), rows(w), rows(w), rows(w), rows(w), rows(w),
                  pl.BlockSpec((1, w), lambda p, s, c: (0, p)),
                  pl.BlockSpec((1, w), lambda p, s, c: (0, p)),
                  pl.BlockSpec((LANES, LANES), lambda p, s, c: (0, 0)),
                  st],
        out_specs=[rows(w), st],
        out_shape=[jax.ShapeDtypeStruct((n_seq * t_seq, GROUP_W), BF16),
                   jax.ShapeDtypeStruct(s0.shape, F32)],
        scratch_shapes=[pltpu.VMEM((seq_per_step, pairs, RWKV_HD, LANES), F32),
                        pltpu.VMEM((rb, w), F32)],
        compiler_params=_cparams(("arbitrary", "arbitrary", "arbitrary")),
        name="rwkv_scan",
    )(ab, kr2, g8, v, br, kr, bon, g, ln_w.reshape(1, -1), ln_b.reshape(1, -1), e128, s0)


def _s5_kernel(u_ref, bb_ref, cc_ref, pw_ref, ad_ref, d_ref, wg_ref, bg_ref, nw_ref, sr0_ref, si0_ref,
               o_ref, srn_ref, sin_ref, bur, bui, cr, ci, wgb, *, tm, per_group_state):
    i = pl.program_id(0)
    nch = S5_GROUPS * S5_N
    ngrp = tm // SUBLANES

    @pl.when(i == 0)
    def _():
        wgb[...] = wg_ref[...].astype(BF16)
        if not per_group_state:
            cr[...] = sr0_ref[...]
            ci[...] = si0_ref[...]

    u = u_ref[...]
    bu = jnp.dot(u.astype(BF16), bb_ref[...], preferred_element_type=F32)
    bur[...] = bu[:, 0:nch]
    bui[...] = bu[:, nch:2 * nch]
    sub = lax.broadcasted_iota(jnp.int32, (SUBLANES, nch), 0)

    def group(r0, c_r, c_i):
        xr = bur[pl.ds(r0, SUBLANES), :]
        xi = bui[pl.ds(r0, SUBLANES), :]
        for di, dsh in enumerate((1, 2, 4)):
            ar = ad_ref[2 * di:2 * di + 1, :]
            ai = ad_ref[2 * di + 1:2 * di + 2, :]
            keep = sub >= dsh
            sr = jnp.where(keep, pltpu.roll(xr, dsh, axis=0), 0.0)
            si = jnp.where(keep, pltpu.roll(xi, dsh, axis=0), 0.0)
            xr, xi = xr + ar * sr - ai * si, xi + ar * si + ai * sr
        pr = pw_ref[0:SUBLANES, :]
        pi = pw_ref[SUBLANES:2 * SUBLANES, :]
        xr, xi = xr + pr * c_r - pi * c_i, xi + pr * c_i + pi * c_r
        bur[pl.ds(r0, SUBLANES), :] = xr
        bui[pl.ds(r0, SUBLANES), :] = xi
        return xr[SUBLANES - 1:SUBLANES, :], xi[SUBLANES - 1:SUBLANES, :]

    def block(bi, carry):
        g0 = pl.multiple_of(bi * SUBLANES, SUBLANES)
        if per_group_state:
            st_r = sr0_ref[pl.ds(g0, SUBLANES), :]
            st_i = si0_ref[pl.ds(g0, SUBLANES), :]
            lasts_r, lasts_i = [], []
        else:
            c_r, c_i = cr[...], ci[...]
        for k in range(SUBLANES):
            r0 = pl.multiple_of((g0 + k) * SUBLANES, SUBLANES)
            if per_group_state:
                l_r, l_i = group(r0, st_r[k:k + 1, :], st_i[k:k + 1, :])
                lasts_r.append(l_r)
                lasts_i.append(l_i)
            else:
                c_r, c_i = group(r0, c_r, c_i)
        if per_group_state:
            srn_ref[pl.ds(g0, SUBLANES), :] = jnp.concatenate(lasts_r, axis=0)
            sin_ref[pl.ds(g0, SUBLANES), :] = jnp.concatenate(lasts_i, axis=0)
        else:
            cr[...] = c_r
            ci[...] = c_i
        return carry

    lax.fori_loop(0, ngrp // SUBLANES, block, 0)

    if not per_group_state:
        srn_ref[...] = cr[...]
        sin_ref[...] = ci[...]

    ccv = cc_ref[...]
    y = (jnp.dot(bur[...].astype(BF16), ccv[0:nch], preferred_element_type=F32)
         + jnp.dot(bui[...].astype(BF16), ccv[nch:2 * nch], preferred_element_type=F32))
    y = y + d_ref[...] * u
    y = 0.5 * y * (1.0 + jnp.tanh(math.sqrt(2.0 / math.pi) * (y + 0.044715 * (y * y * y))))
    gate = jnp.dot(y.astype(BF16), wgb[...], preferred_element_type=F32) + bg_ref[...]
    y = y * jax.nn.sigmoid(gate)
    y = y * lax.rsqrt(jnp.mean(y * y, axis=-1, keepdims=True) + EPS)
    o_ref[...] = (y * nw_ref[...]).astype(o_ref.dtype)


def s5_mixer(u, p, sr0, si0, *, per_group_state, row0, m, tm=256):
    assert row0 % tm == 0 and m % tm == 0
    b0 = row0 // tm
    nch = S5_GROUPS * S5_N
    lr, li = p["lam_re"], p["lam_im"]
    dt = jnp.exp(p["log_step"])[:, None]
    mag = jnp.exp(lr * dt)
    ab_re, ab_im = mag * jnp.cos(li * dt), mag * jnp.sin(li * dt)
    den = lr * lr + li * li
    cf_re = ((ab_re - 1.0) * lr + ab_im * li) / den
    cf_im = (ab_im * lr - (ab_re - 1.0) * li) / den
    bb_re = cf_re[..., None] * p["b_re"] - cf_im[..., None] * p["b_im"]
    bb_im = cf_re[..., None] * p["b_im"] + cf_im[..., None] * p["b_re"]
    eye = jnp.eye(S5_GROUPS, dtype=F32)

    def bd_in(b):
        return jnp.einsum("gnc,gh->gchn", b, eye).reshape(GROUP_W, nch)

    def bd_out(cm):
        return jnp.einsum("gcn,gh->gnhc", cm, eye).reshape(nch, GROUP_W)

    bb = jnp.concatenate([bd_in(bb_re), bd_in(bb_im)], axis=1).astype(BF16)
    cc = jnp.concatenate([bd_out(p["c_re"]), -bd_out(p["c_im"])], axis=0).astype(BF16)
    ar, ai = ab_re.reshape(1, nch), ab_im.reshape(1, nch)
    pows = [(ar, ai)]
    for _ in range(SUBLANES - 1):
        pr, pi = pows[-1]
        pows.append((pr * ar - pi * ai, pr * ai + pi * ar))
    pw = jnp.concatenate([jnp.concatenate([q[0] for q in pows], axis=0),
                          jnp.concatenate([q[1] for q in pows], axis=0)], axis=0)
    ad = jnp.concatenate([pows[0][0], pows[0][1], pows[1][0], pows[1][1], pows[3][0], pows[3][1],
                          jnp.zeros((2, nch), F32)], axis=0)
    n_state = sr0.shape[0]
    full = lambda a: pl.BlockSpec(a.shape, lambda i: (0,) * a.ndim)
    row = lambda n: pl.BlockSpec((1, n), lambda i: (0, 0))
    if per_group_state:
        st = pl.BlockSpec((tm // SUBLANES, nch), lambda i: (i, 0))
    else:
        st = pl.BlockSpec((1, nch), lambda i: (0, 0))
    kern = functools.partial(_s5_kernel, tm=tm, per_group_state=per_group_state)
    return pl.pallas_call(
        kern,
        grid=(m // tm,),
        in_specs=[pl.BlockSpec((tm, GROUP_W), lambda i: (b0 + i, 0)), full(bb), full(cc), full(pw), full(ad),
                  row(GROUP_W), full(p["w_glu"]), row(GROUP_W), row(GROUP_W), st, st],
        out_specs=[pl.BlockSpec((tm, GROUP_W), lambda i: (i, 0)), st, st],
        out_shape=[jax.ShapeDtypeStruct((m, GROUP_W), BF16),
                   jax.ShapeDtypeStruct((n_state, nch), F32), jax.ShapeDtypeStruct((n_state, nch), F32)],
        scratch_shapes=[pltpu.VMEM((tm, nch), F32), pltpu.VMEM((tm, nch), F32),
                        pltpu.VMEM((1, nch), F32), pltpu.VMEM((1, nch), F32),
                        pltpu.VMEM((GROUP_W, GROUP_W), BF16)],
        compiler_params=_cparams(("arbitrary",)),
        name="s5_mixer",
    )(u, bb, cc, pw, ad, p["d"].reshape(1, -1), p["w_glu"], p["b_glu"].reshape(1, -1),
      p["norm"].reshape(1, -1), sr0, si0)


def _ffn_up_kernel(h_ref, wg_ref, wv_ref, cwg_ref, cwv_ref, cbg_ref, cbv_ref, c0ga_ref, c0gb_ref, c0va_ref,
                   c0vb_ref, act_ref, cnga_ref, cngb_ref, cnva_ref, cnvb_ref, wgb, wvb, hg, hv, *, shift, tm, off):
    mstep = pl.program_id(1)
    hist = 2 * shift

    @pl.when(mstep == 0)
    def _():
        wgb[...] = wg_ref[...].astype(BF16)
        wvb[...] = wv_ref[...].astype(BF16)
        hg[off - hist:off - shift, :] = c0ga_ref[...]
        hg[off - shift:off, :] = c0gb_ref[...]
        hv[off - hist:off - shift, :] = c0va_ref[...]
        hv[off - shift:off, :] = c0vb_ref[...]

    hb = h_ref[...]
    hg[off:off + tm, :] = jnp.dot(hb, wgb[...], preferred_element_type=F32)
    hv[off:off + tm, :] = jnp.dot(hb, wvb[...], preferred_element_type=F32)

    def conv(hs, cw_ref, cb_ref):
        return (cb_ref[...] + cw_ref[0:1, :] * hs[off - hist:off - hist + tm, :]
                + cw_ref[1:2, :] * hs[off - shift:off - shift + tm, :]
                + cw_ref[2:3, :] * hs[off:off + tm, :])

    gate = conv(hg, cwg_ref, cbg_ref)
    val = conv(hv, cwv_ref, cbv_ref)
    act_ref[...] = (gate * jax.nn.sigmoid(gate) * val).astype(act_ref.dtype)
    tail_g = hg[off + tm - hist:off + tm, :]
    tail_v = hv[off + tm - hist:off + tm, :]
    hg[off - hist:off, :] = tail_g
    hv[off - hist:off, :] = tail_v

    @pl.when(mstep == pl.num_programs(1) - 1)
    def _():
        cnga_ref[...] = tail_g[0:shift]
        cngb_ref[...] = tail_g[shift:hist]
        cnva_ref[...] = tail_v[0:shift]
        cnvb_ref[...] = tail_v[shift:hist]


def ffn_up(h, w_up, conv_w, conv_b, c0, *, time_major, row0=0, m=None, tm=None, tn=512):
    n_seq = c0.shape[0]
    nj = D_FF // tn
    if time_major:
        _, steps, k = h.shape
        shift = tm = n_seq
        h = h.reshape(n_seq, steps * k)
        h_spec = pl.BlockSpec((n_seq, k), lambda j, i: (0, i))
        act_spec = pl.BlockSpec((n_seq, tn), lambda j, i: (0, i * nj + j))
        act_shape = jax.ShapeDtypeStruct((n_seq, steps * D_FF), BF16)
    else:
        k = h.shape[1]
        assert n_seq == 1 and row0 % tm == 0 and m % tm == 0
        shift, steps, b0 = 1, m // tm, row0 // tm
        h_spec = pl.BlockSpec((tm, k), lambda j, i: (b0 + i, 0))
        act_spec = pl.BlockSpec((tm, tn), lambda j, i: (i, j))
        act_shape = jax.ShapeDtypeStruct((m, D_FF), BF16)
    hist = 2 * shift
    off = max(SUBLANES, hist)
    kern = functools.partial(_ffn_up_kernel, shift=shift, tm=tm, off=off)
    cw = jnp.concatenate([conv_w, jnp.zeros((SUBLANES - conv_w.shape[0], conv_w.shape[1]), F32)], axis=0)
    cb = conv_b.reshape(1, -1)

    c0 = c0.reshape(n_seq, 4 * D_FF)

    def c0_spec(tap, half):
        return pl.BlockSpec((n_seq, tn), lambda j, i: (0, (2 * tap + half) * nj + j))

    tap_out = pl.BlockSpec((shift, tn), lambda j, i: (0, j))
    tap_shape = jax.ShapeDtypeStruct((shift, D_FF), F32)
    outs = pl.pallas_call(
        kern,
        grid=(nj, steps),
        in_specs=[h_spec,
                  pl.BlockSpec((k, tn), lambda j, i: (0, j)),
                  pl.BlockSpec((k, tn), lambda j, i: (0, j + nj)),
                  pl.BlockSpec((SUBLANES, tn), lambda j, i: (0, j)),
                  pl.BlockSpec((SUBLANES, tn), lambda j, i: (0, j + nj)),
                  pl.BlockSpec((1, tn), lambda j, i: (0, j)),
                  pl.BlockSpec((1, tn), lambda j, i: (0, j + nj)),
                  c0_spec(0, 0), c0_spec(1, 0), c0_spec(0, 1), c0_spec(1, 1)],
        out_specs=[act_spec, tap_out, tap_out, tap_out, tap_out],
        out_shape=[act_shape, tap_shape, tap_shape, tap_shape, tap_shape],
        scratch_shapes=[pltpu.VMEM((k, tn), BF16), pltpu.VMEM((k, tn), BF16),
                        pltpu.VMEM((off + tm, tn), F32), pltpu.VMEM((off + tm, tn), F32)],
        compiler_params=_cparams(("parallel", "arbitrary")),
        name="ffn_up",
    )(h, w_up, w_up, cw, cw, cb, cb, c0, c0, c0, c0)
    return outs[0], outs[1:]


def conv_state_from_taps(taps):
    ga, gb, va, vb = taps
    return jnp.stack([jnp.concatenate([ga, va], axis=1), jnp.concatenate([gb, vb], axis=1)], axis=1)


def _ret_rope_tables(pos):
    half = RET_HD // 2
    inv = jnp.power(RET_ROPE_BASE, -jnp.arange(half, dtype=F32) / half)
    ang = pos.astype(F32)[:, None] * inv[None, :]
    cos, sin = jnp.cos(ang), jnp.sin(ang)
    return jnp.concatenate([cos, cos], axis=1), jnp.concatenate([-sin, sin], axis=1)


def _diff_rope_tables(pos):
    half = ROPE_DIM // 2
    inv = jnp.power(ROPE_THETA, -jnp.arange(half, dtype=F32) / half)
    ang = pos.astype(F32)[:, None] * inv[None, :]
    cos, sin = jnp.cos(ang), jnp.sin(ang)
    n = pos.shape[0]
    rest = DIFF_QD - ROPE_DIM
    c = jnp.concatenate([cos, cos, jnp.ones((n, rest), F32)], axis=1)
    s1 = jnp.concatenate([-sin, jnp.zeros((n, half + rest), F32)], axis=1)
    s2 = jnp.concatenate([jnp.zeros((n, half), F32), sin, jnp.zeros((n, rest), F32)], axis=1)
    rep = LANES // DIFF_QD
    return jnp.tile(c, (1, rep)), jnp.tile(s1, (1, rep)), jnp.tile(s2, (1, rep))


def _rwkv_state_to_pairs(s):
    b = s.shape[0]
    s = s.reshape(b, RWKV_HEADS // 2, 2, RWKV_HD, RWKV_HD)
    return jnp.transpose(s, (0, 1, 4, 2, 3)).reshape(b, RWKV_HEADS // 2, RWKV_HD, LANES)


def _rwkv_state_from_pairs(s):
    b = s.shape[0]
    s = s.reshape(b, RWKV_HEADS // 2, RWKV_HD, 2, RWKV_HD)
    return jnp.transpose(s, (0, 1, 3, 4, 2)).reshape(b, RWKV_HEADS, RWKV_HD, RWKV_HD)


def kernel(x_prompt, x_sample, p_prompt, p_sample, cache_k, cache_v, page_table, state_ret, state_rwkv, state_rwkv_shift, state_s5_re, state_s5_im, state_ffn_conv, norm_mix, w_in, w_out, ret_norm_w, ret_norm_b, diff_lq1, diff_lk1, diff_lq2, diff_lk2, diff_subln, rwkv_mu, rwkv_w0, rwkv_w2, rwkv_a0, rwkv_a2, rwkv_g2, rwkv_kk, rwkv_ka, rwkv_rk, rwkv_ln_w, rwkv_ln_b, s5_lam_re, s5_lam_im, s5_log_step, s5_b_re, s5_b_im, s5_c_re, s5_c_im, s5_d, s5_w_glu, s5_b_glu, s5_norm, norm_ffn, ffn_w_up, ffn_conv_w, ffn_conv_b, ffn_w_down, norm_ple, ple_w_proj, ple_norm_e, ple_w_gate, norm_final):
    depth = w_in.shape[0]
    bp, tp, d = x_prompt.shape
    nb, ts, _ = x_sample.shape
    assert bp == 1
    mp, ms = bp * tp, nb * ts
    past_len = page_table.shape[1] * PAGE_SIZE
    nch = S5_GROUPS * S5_N

    x = jnp.concatenate([x_prompt.reshape(mp, d), x_sample.reshape(ms, d)], axis=0)
    p_all = jnp.concatenate([p_prompt.reshape(depth, mp, -1), p_sample.reshape(depth, ms, -1)], axis=1)
    pos_p = jnp.arange(tp, dtype=jnp.int32)
    pos_s = past_len + jnp.arange(ts, dtype=jnp.int32)
    ret_cos_p, ret_sin_p = _ret_rope_tables(pos_p)
    ret_cos_s, ret_sin_s = _ret_rope_tables(pos_s)
    pos_all = jnp.concatenate([pos_p, jnp.tile(pos_s, nb)])
    dc, ds1, ds2 = _diff_rope_tables(pos_all)
    cache_k2 = cache_k.reshape(cache_k.shape[0], cache_k.shape[1], PAGE_SIZE * DIFF_HEADS, DIFF_VD)
    cache_v2 = cache_v.reshape(cache_v.shape[0], cache_v.shape[1], PAGE_SIZE * DIFF_HEADS, DIFF_VD)
    head_of = jnp.arange(GROUP_W) // RWKV_HD
    e512 = (head_of[:, None] == head_of[None, :]).astype(BF16)
    e128 = e512[:LANES, :LANES]

    outs = {k: [] for k in ("k_p", "v_p", "k_s", "v_s", "ret_p", "ret_s", "rwkv_p", "rwkv_s", "sh_p", "sh_s",
                            "s5r_p", "s5i_p", "s5r_s", "s5i_s", "conv_p", "conv_s")}
    for i in range(depth):
        h = rmsnorm_rows(x, norm_mix[i], BF16)
        wi = w_in[i]
        c_ret = matmul(h, wi, tn=1024, tm=1024, col0=0, ncols=RET_COLS, name="proj_ret")
        c_diff = matmul(h, wi, tn=512, tm=1024, col0=RET_COLS, ncols=DIFF_COLS, name="proj_diff")
        c_rwkv = matmul(h, wi, tn=896, tm=1024, col0=RET_COLS + DIFF_COLS, ncols=RWKV_COLS, name="proj_rwkv")
        c_s5 = matmul(h, wi, tn=256, tm=1024, col0=RET_COLS + DIFF_COLS + RWKV_COLS, ncols=S5_COLS, name="proj_s5")

        o_ret_p, s_ret_p = retention(c_ret, 0, bp, tp, ret_cos_p, ret_sin_p,
                                     jnp.zeros((bp, RET_HEADS, RET_HD, RET_HD), F32),
                                     ret_norm_w[i], ret_norm_b[i], chunk=RET_CHUNK, chunks_per_step=4)
        o_ret_s, s_ret_s = retention(c_ret, mp, nb, ts, ret_cos_s, ret_sin_s, state_ret[i],
                                     ret_norm_w[i], ret_norm_b[i], chunk=ts, chunks_per_step=1)

        lam_init = 0.8 - 0.6 * math.exp(-0.3 * i)
        lam = (jnp.exp(jnp.sum(diff_lq1[i] * diff_lk1[i])) - jnp.exp(jnp.sum(diff_lq2[i] * diff_lk2[i])) + lam_init)
        q0, q1, k_new, kb, vt = diff_prep(c_diff, dc, ds1, ds2)
        v_new = c_diff[:, 2 * GROUP_W:]
        o_diff_p = diff_attention_prompt(q0, q1, kb, vt, lam, diff_subln[i], lam_init, t=mp)
        qs = (q0[mp:].astype(F32) + q1[mp:].astype(F32)).reshape(nb, ts, GROUP_W)
        o_diff_s = diff_attention_sample(qs, k_new[mp:], v_new[mp:], cache_k2, cache_v2, i, page_table,
                                         lam, diff_subln[i], lam_init)

        cr_s = c_rwkv[mp:].reshape(nb, ts, RWKV_COLS)
        prev_s = jnp.concatenate([state_rwkv_shift[i][:, None], cr_s[:, :-1]], axis=1).reshape(ms, RWKV_COLS)
        rp = dict(mu=rwkv_mu[i], w0=rwkv_w0[i], w2=rwkv_w2[i], a0=rwkv_a0[i], a2=rwkv_a2[i], g2=rwkv_g2[i],
                  kk=rwkv_kk[i], ka=rwkv_ka[i], rk=rwkv_rk[i].reshape(-1))
        lnw, lnb = rwkv_ln_w[i].reshape(-1), rwkv_ln_b[i].reshape(-1)
        o_rwkv_p, s_rwkv_p = rwkv_scan(*rwkv_prep(c_rwkv, None, rp, e512, row0=0, m=mp), lnw, lnb, e128,
                                       jnp.zeros((bp, RWKV_HEADS // 2, RWKV_HD, LANES), F32),
                                       t_seq=tp, seq_per_step=1, t_step=256)
        o_rwkv_s, s_rwkv_s = rwkv_scan(*rwkv_prep(c_rwkv, prev_s, rp, e512, row0=mp, m=ms), lnw, lnb, e128,
                                       _rwkv_state_to_pairs(state_rwkv[i]),
                                       t_seq=ts, seq_per_step=16, t_step=ts)

        sp = dict(lam_re=s5_lam_re[i], lam_im=s5_lam_im[i], log_step=s5_log_step[i], b_re=s5_b_re[i], b_im=s5_b_im[i],
                  c_re=s5_c_re[i], c_im=s5_c_im[i], d=s5_d[i], w_glu=s5_w_glu[i], b_glu=s5_b_glu[i], norm=s5_norm[i])
        o_s5_p, s5r_p, s5i_p = s5_mixer(c_s5, sp, jnp.zeros((1, nch), F32), jnp.zeros((1, nch), F32),
                                        per_group_state=False, row0=0, m=mp)
        o_s5_s, s5r_s, s5i_s = s5_mixer(c_s5, sp, state_s5_re[i].reshape(nb, nch), state_s5_im[i].reshape(nb, nch),
                                        per_group_state=True, row0=mp, m=ms)

        x = matmul_split([o_ret_p, o_diff_p, o_rwkv_p, o_s5_p], [o_ret_s, o_diff_s, o_rwkv_s, o_s5_s], w_out[i],
                         tn=512, tm=1024, tiles=(x,), epilogue=lambda acc, res: res + acc, name="w_out")

        h2 = rmsnorm_rows(x, norm_ffn[i], BF16)
        act_p, taps_p = ffn_up(h2, ffn_w_up[i], ffn_conv_w[i], ffn_conv_b[i], jnp.zeros((bp, 2, 2 * D_FF), F32),
                               time_major=False, row0=0, m=mp, tm=1024)
        act_s, taps_s = ffn_up(h2[mp:].reshape(nb, ts, d), ffn_w_up[i], ffn_conv_w[i], ffn_conv_b[i],
                               state_ffn_conv[i], time_major=True)
        x = matmul_split([act_p], [act_s.reshape(ms, D_FF)], ffn_w_down[i], tn=512, tm=256, tiles=(x,),
                         epilogue=lambda acc, res: res + acc, name="ffn_down")

        e = matmul(p_all[i], ple_w_proj[i], tn=d, tm=256, rows=(ple_norm_e[i],),
                   epilogue=lambda acc, g: acc * lax.rsqrt(jnp.mean(acc * acc, axis=-1, keepdims=True) + EPS) * g,
                   name="ple_proj")
        h3 = rmsnorm_rows(x, norm_ple[i], BF16)
        x = matmul(h3, ple_w_gate[i], tn=512, tm=1024, tiles=(x, e),
                   epilogue=lambda acc, res, ee: res + ee * jax.nn.sigmoid(acc), name="ple_gate")

        outs["k_p"].append(k_new[:mp].reshape(bp, tp, DIFF_HEADS, 2 * DIFF_QD))
        outs["v_p"].append(v_new[:mp].reshape(bp, tp, DIFF_HEADS, DIFF_VD))
        outs["k_s"].append(k_new[mp:].reshape(nb, ts, DIFF_HEADS, 2 * DIFF_QD))
        outs["v_s"].append(v_new[mp:].reshape(nb, ts, DIFF_HEADS, DIFF_VD))
        outs["ret_p"].append(s_ret_p)
        outs["ret_s"].append(s_ret_s)
        outs["rwkv_p"].append(_rwkv_state_from_pairs(s_rwkv_p))
        outs["rwkv_s"].append(_rwkv_state_from_pairs(s_rwkv_s))
        outs["sh_p"].append(c_rwkv[mp - 1:mp].reshape(bp, RWKV_COLS))
        outs["sh_s"].append(cr_s[:, -1])
        outs["s5r_p"].append(s5r_p.reshape(bp, S5_GROUPS, S5_N))
        outs["s5i_p"].append(s5i_p.reshape(bp, S5_GROUPS, S5_N))
        outs["s5r_s"].append(s5r_s.reshape(nb, S5_GROUPS, S5_N))
        outs["s5i_s"].append(s5i_s.reshape(nb, S5_GROUPS, S5_N))
        outs["conv_p"].append(conv_state_from_taps(taps_p))
        outs["conv_s"].append(conv_state_from_taps(taps_s))

    y = rmsnorm_rows(x, norm_final, F32)
    st = lambda k: jnp.stack(outs[k])
    return (y[:mp].reshape(bp, tp, d), y[mp:].reshape(nb, ts, d),
            st("k_p"), st("v_p"), st("k_s"), st("v_s"), st("ret_p"), st("ret_s"), st("rwkv_p"), st("rwkv_s"),
            st("sh_p"), st("sh_s"), st("s5r_p"), st("s5i_p"), st("s5r_s"), st("s5i_s"), st("conv_p"), st("conv_s"))
```

```python
import functools
import math

import jax
import jax.numpy as jnp
import numpy as np
from jax import lax
from jax.experimental import pallas as pl
from jax.experimental.pallas import tpu as pltpu

F32 = jnp.float32
BF16 = jnp.bfloat16

D_MODEL = 2048
GROUP_W = 512
RET_HEADS = 4
RET_HD = 128
RET_CHUNK = 128
RET_ROPE_BASE = 10000.0
DIFF_HEADS = 4
DIFF_VD = 128
DIFF_QD = 64
ROPE_THETA = 500000.0
ROPE_DIM = 16
PAGE_SIZE = 128
RWKV_HD = 64
RWKV_HEADS = 8
RWKV_LN_EPS = 64e-5
S5_CH = 16
S5_GROUPS = 32
S5_N = 64
D_FF = 5632
EPS = 1e-6
NEG_INF = -1e30
LOG2_E = math.log2(math.e)

RET_COLS = 4 * GROUP_W
DIFF_COLS = 3 * GROUP_W
RWKV_COLS = 3 * GROUP_W + 64 + 64 + 128
S5_COLS = GROUP_W

LANES = 128
SUBLANES = 8
VMEM_LIMIT_BYTES = 52 * 1024 * 1024


def _cparams(sem, vmem=VMEM_LIMIT_BYTES):
    return pltpu.CompilerParams(dimension_semantics=sem, vmem_limit_bytes=vmem)


def _split_bf16(x):
    hi = x.astype(BF16)
    lo = (x - hi.astype(F32)).astype(BF16)
    return hi, lo


def _dot_hilo(x, w_bf16):
    hi, lo = _split_bf16(x)
    return (jnp.dot(hi, w_bf16, preferred_element_type=F32)
            + jnp.dot(lo, w_bf16, preferred_element_type=F32))


def _dot3(x, w_hi, w_lo):
    hi, lo = _split_bf16(x)
    return (jnp.dot(hi, w_hi, preferred_element_type=F32)
            + jnp.dot(hi, w_lo, preferred_element_type=F32)
            + jnp.dot(lo, w_hi, preferred_element_type=F32))


def _rmsnorm_kernel(x_ref, g_ref, o_ref):
    x = x_ref[...]
    y = x * lax.rsqrt(jnp.mean(x * x, axis=-1, keepdims=True) + EPS)
    o_ref[...] = (y * g_ref[...]).astype(o_ref.dtype)


def rmsnorm_rows(x, g, out_dtype, tm=512):
    m, d = x.shape
    return pl.pallas_call(
        _rmsnorm_kernel,
        grid=(m // tm,),
        in_specs=[pl.BlockSpec((tm, d), lambda i: (i, 0)),
                  pl.BlockSpec((1, d), lambda i: (0, 0))],
        out_specs=pl.BlockSpec((tm, d), lambda i: (i, 0)),
        out_shape=jax.ShapeDtypeStruct((m, d), out_dtype),
        compiler_params=_cparams(("parallel",)),
        name="rmsnorm_rows",
    )(x, g.reshape(1, d))


def _mm_kernel(x_ref, w_ref, *rest, epilogue, n_extra):
    extra = rest[:n_extra]
    o_ref, wb_ref = rest[n_extra], rest[n_extra + 1]

    @pl.when(pl.program_id(1) == 0)
    def _():
        wb_ref[...] = w_ref[...].astype(BF16)

    acc = jnp.dot(x_ref[...].astype(BF16), wb_ref[...], preferred_element_type=F32)
    if epilogue is not None:
        acc = epilogue(acc, *[e[...] for e in extra])
    o_ref[...] = acc.astype(o_ref.dtype)


def _weight_spec(w, layer, k, tn, joff=0):
    if w.ndim == 3:
        return pl.BlockSpec((None, k, tn), lambda j, i: (layer, 0, j + joff))
    return pl.BlockSpec((k, tn), lambda j, i: (0, j + joff))


def matmul(x, w, *, tn, tm=512, col0=0, ncols=None, epilogue=None, tiles=(), rows=(),
           out_dtype=F32, layer=None, name="matmul"):
    m, k = x.shape
    n = w.shape[-1] - col0 if ncols is None else ncols
    assert col0 % tn == 0 and n % tn == 0 and m % tm == 0
    joff = col0 // tn
    in_specs = [pl.BlockSpec((tm, k), lambda j, i: (i, 0)), _weight_spec(w, layer, k, tn, joff)]
    in_specs += [pl.BlockSpec((tm, tn), lambda j, i: (i, j)) for _ in tiles]
    in_specs += [pl.BlockSpec((1, tn), lambda j, i: (0, j)) for _ in rows]
    kern = functools.partial(_mm_kernel, epilogue=epilogue, n_extra=len(tiles) + len(rows))
    return pl.pallas_call(
        kern,
        grid=(n // tn, m // tm),
        in_specs=in_specs,
        out_specs=pl.BlockSpec((tm, tn), lambda j, i: (i, j)),
        out_shape=jax.ShapeDtypeStruct((m, n), out_dtype),
        scratch_shapes=[pltpu.VMEM((k, tn), BF16)],
        compiler_params=_cparams(("parallel", "arbitrary")),
        name=name,
    )(x, w, *tiles, *[r.reshape(1, -1) for r in rows])


def _mm_split_kernel(*refs, n_parts, part_k, n_prompt_tiles, epilogue, n_extra):
    xp = refs[:n_parts]
    xs = refs[n_parts:2 * n_parts]
    w_ref = refs[2 * n_parts]
    extra = refs[2 * n_parts + 1:2 * n_parts + 1 + n_extra]
    o_ref, wb_ref = refs[2 * n_parts + 1 + n_extra], refs[2 * n_parts + 2 + n_extra]
    i = pl.program_id(1)

    @pl.when(i == 0)
    def _():
        wb_ref[...] = w_ref[...].astype(BF16)

    def body(parts):
        acc = None
        for g, r in enumerate(parts):
            d = jnp.dot(r[...], wb_ref[g * part_k:(g + 1) * part_k, :], preferred_element_type=F32)
            acc = d if acc is None else acc + d
        o_ref[...] = epilogue(acc, *[e[...] for e in extra]).astype(o_ref.dtype)

    @pl.when(i < n_prompt_tiles)
    def _():
        body(xp)

    @pl.when(i >= n_prompt_tiles)
    def _():
        body(xs)


def matmul_split(xs_prompt, xs_sample, w, *, tn, tm, epilogue, tiles=(), out_dtype=F32, layer=None,
                 name="matmul_split"):
    n_parts = len(xs_prompt)
    mp, part_k = xs_prompt[0].shape
    ms = xs_sample[0].shape[0]
    k, n = w.shape[-2:]
    assert k == n_parts * part_k and mp % tm == 0 and ms % tm == 0 and n % tn == 0
    npt, nst = mp // tm, ms // tm
    in_specs = [pl.BlockSpec((tm, part_k), lambda j, i: (jnp.minimum(i, npt - 1), 0)) for _ in xs_prompt]
    in_specs += [pl.BlockSpec((tm, part_k), lambda j, i: (jnp.maximum(i - npt, 0), 0)) for _ in xs_sample]
    in_specs += [_weight_spec(w, layer, k, tn)]
    in_specs += [pl.BlockSpec((tm, tn), lambda j, i: (i, j)) for _ in tiles]
    kern = functools.partial(_mm_split_kernel, n_parts=n_parts, part_k=part_k, n_prompt_tiles=npt,
                             epilogue=epilogue, n_extra=len(tiles))
    return pl.pallas_call(
        kern,
        grid=(n // tn, npt + nst),
        in_specs=in_specs,
        out_specs=pl.BlockSpec((tm, tn), lambda j, i: (i, j)),
        out_shape=jax.ShapeDtypeStruct((mp + ms, n), out_dtype),
        scratch_shapes=[pltpu.VMEM((k, tn), BF16)],
        compiler_params=_cparams(("parallel", "arbitrary")),
        name=name,
    )(*xs_prompt, *xs_sample, w, *tiles)


def _ret_kernel(q_ref, k_ref, v_ref, g_ref, cos_ref, sin_ref, dmask_ref, qdec_ref, kdec_ref,
                cdec_ref, nw_ref, nb_ref, s0_ref, o_ref, sn_ref, s_scr, *, chunk, n_chunks):
    c = pl.program_id(1)

    @pl.when(c == 0)
    def _():
        s_scr[...] = s0_ref[...]

    cos = cos_ref[...]
    sin = sin_ref[...]
    for h in range(RET_HEADS):
        hs = slice(h * RET_HD, (h + 1) * RET_HD)
        qh = q_ref[:, hs]
        kh = k_ref[:, hs]
        qr = qh * cos + pltpu.roll(qh, RET_HD // 2, axis=1) * sin
        kr = (kh * cos + pltpu.roll(kh, RET_HD // 2, axis=1) * sin) * (RET_HD ** -0.5)
        vh = v_ref[:, hs]
        s = s_scr[h]
        outs = []
        for ci in range(n_chunks):
            rs = slice(ci * chunk, (ci + 1) * chunk)
            qc, kc, vc = qr[rs], kr[rs], vh[rs].astype(BF16)
            att = lax.dot_general(qc.astype(BF16), kc.astype(BF16), (((1,), (1,)), ((), ())),
                                  preferred_element_type=F32) * dmask_ref[h]
            o = jnp.dot(att.astype(BF16), vc, preferred_element_type=F32)
            o += jnp.dot((qc * qdec_ref[h]).astype(BF16), s.astype(BF16), preferred_element_type=F32)
            kd = (kc * kdec_ref[h]).astype(BF16)
            s = s * cdec_ref[h, 0:1, :] + lax.dot_general(kd, vc, (((0,), (0,)), ((), ())),
                                                  preferred_element_type=F32)
            outs.append(o)
        s_scr[h] = s
        o = outs[0] if n_chunks == 1 else jnp.concatenate(outs, axis=0)
        mu = jnp.mean(o, axis=-1, keepdims=True)
        var = jnp.mean(jnp.square(o - mu), axis=-1, keepdims=True)
        o = (o - mu) * lax.rsqrt(var + EPS) * nw_ref[h:h + 1, :] + nb_ref[h:h + 1, :]
        gh = g_ref[:, hs]
        o_ref[:, hs] = (o * (gh * jax.nn.sigmoid(gh))).astype(o_ref.dtype)

    @pl.when(c == pl.num_programs(1) - 1)
    def _():
        sn_ref[...] = s_scr[...]


def retention(proj, row0, n_seq, t_seq, cos, sin, s0, norm_w, norm_b, *, chunk, chunks_per_step, layer=None):
    rb = chunk * chunks_per_step
    steps = t_seq // rb
    assert row0 % rb == 0 and t_seq % rb == 0
    b0 = row0 // rb
    log_g = jnp.log1p(-jnp.exp2(-5.0 - jnp.arange(RET_HEADS, dtype=F32)))
    idx = jnp.arange(chunk, dtype=F32)
    rel = idx[:, None] - idx[None, :]
    dmask = jnp.where(rel >= 0, jnp.exp(log_g[:, None, None] * jnp.maximum(rel, 0.0)), 0.0)
    ones = jnp.ones((1, 1, RET_HD), F32)
    qdec = jnp.exp(log_g[:, None] * (idx + 1.0))[:, :, None] * ones
    kdec = jnp.exp(log_g[:, None] * (chunk - 1.0 - idx))[:, :, None] * ones
    cdec = jnp.exp(log_g * chunk)[:, None, None] * jnp.ones((1, SUBLANES, RET_HD), F32)

    def col(j):
        return pl.BlockSpec((rb, GROUP_W), lambda s, c: (b0 + s * steps + c, j))

    full = lambda shape: pl.BlockSpec(shape, lambda s, c: (0,) * len(shape))
    if s0.ndim == 5:
        s0_spec = pl.BlockSpec((None, None, RET_HEADS, RET_HD, RET_HD), lambda s, c: (layer, s, 0, 0, 0))
    else:
        s0_spec = pl.BlockSpec((None, RET_HEADS, RET_HD, RET_HD), lambda s, c: (s, 0, 0, 0))
    kern = functools.partial(_ret_kernel, chunk=chunk, n_chunks=chunks_per_step)
    return pl.pallas_call(
        kern,
        grid=(n_seq, steps),
        in_specs=[col(0), col(1), col(2), col(3),
                  pl.BlockSpec((rb, RET_HD), lambda s, c: (c, 0)),
                  pl.BlockSpec((rb, RET_HD), lambda s, c: (c, 0)),
                  full((RET_HEADS, chunk, chunk)),
                  full((RET_HEADS, chunk, RET_HD)),
                  full((RET_HEADS, chunk, RET_HD)),
                  full((RET_HEADS, SUBLANES, RET_HD)),
                  full((RET_HEADS, RET_HD)),
                  full((RET_HEADS, RET_HD)),
                  s0_spec],
        out_specs=[pl.BlockSpec((rb, GROUP_W), lambda s, c: (s * steps + c, 0)),
                   pl.BlockSpec((None, RET_HEADS, RET_HD, RET_HD), lambda s, c: (s, 0, 0, 0))],
        out_shape=[jax.ShapeDtypeStruct((n_seq * t_seq, GROUP_W), BF16),
                   jax.ShapeDtypeStruct((n_seq, RET_HEADS, RET_HD, RET_HD), F32)],
        scratch_shapes=[pltpu.VMEM((RET_HEADS, RET_HD, RET_HD), F32)],
        compiler_params=_cparams(("arbitrary", "arbitrary")),
        name="retention",
    )(proj, proj, proj, proj, cos, sin, dmask, qdec, kdec, cdec, norm_w, norm_b, s0)


def _diff_prep_kernel(q_ref, k_ref, v_ref, c_ref, s1_ref, s2_ref, q0_ref, q1_ref, kn_ref, kb_ref, vt_ref):
    c, s1, s2 = c_ref[...], s1_ref[...], s2_ref[...]
    lane = lax.broadcasted_iota(jnp.int32, c.shape, 1)
    lo = lane < DIFF_QD
    for j in range(GROUP_W // LANES):
        cs = slice(j * LANES, (j + 1) * LANES)
        q = q_ref[:, cs]
        k = k_ref[:, cs]
        qr = q * c + pltpu.roll(q, LANES - ROPE_DIM // 2, axis=1) * s1 + pltpu.roll(q, ROPE_DIM // 2, axis=1) * s2
        kr = k * c + pltpu.roll(k, LANES - ROPE_DIM // 2, axis=1) * s1 + pltpu.roll(k, ROPE_DIM // 2, axis=1) * s2
        qr = qr * (DIFF_QD ** -0.5 * LOG2_E)
        q0_ref[:, cs] = jnp.where(lo, qr, 0.0).astype(BF16)
        q1_ref[:, cs] = jnp.where(lo, 0.0, qr).astype(BF16)
        kn_ref[:, cs] = kr
        kb_ref[:, cs] = kr.astype(BF16)
    vt_ref[...] = v_ref[...].T.astype(BF16)


def diff_prep(proj, c, s1, s2, tm=512):
    m = proj.shape[0]
    col = lambda j: pl.BlockSpec((tm, GROUP_W), lambda i: (i, j))
    tab = pl.BlockSpec((tm, LANES), lambda i: (i, 0))
    out = pl.BlockSpec((tm, GROUP_W), lambda i: (i, 0))
    return pl.pallas_call(
        _diff_prep_kernel,
        grid=(m // tm,),
        in_specs=[col(0), col(1), col(2), tab, tab, tab],
        out_specs=[out] * 4 + [pl.BlockSpec((GROUP_W, tm), lambda i: (0, i))],
        out_shape=[jax.ShapeDtypeStruct((m, GROUP_W), BF16), jax.ShapeDtypeStruct((m, GROUP_W), BF16),
                   jax.ShapeDtypeStruct((m, GROUP_W), F32), jax.ShapeDtypeStruct((m, GROUP_W), BF16),
                   jax.ShapeDtypeStruct((GROUP_W, m), BF16)],
        compiler_params=_cparams(("parallel",)),
        name="diff_prep",
    )(proj, proj, proj, c, s1, s2)


def _diff_flash_kernel(qi_ref, ki_ref, lam_ref, q0_ref, q1_ref, k_ref, vt_ref, sub_ref, o_ref,
                       m0, l0, a0, m1, l1, a1, *, blk, out_scale):
    step = pl.program_id(1)
    qi = qi_ref[step]
    ki = ki_ref[step]

    @pl.when(ki == 0)
    def _():
        m0[...] = jnp.full(m0.shape, NEG_INF, F32)
        m1[...] = jnp.full(m1.shape, NEG_INF, F32)
        l0[...] = jnp.zeros(l0.shape, F32)
        l1[...] = jnp.zeros(l1.shape, F32)
        a0[...] = jnp.zeros(a0.shape, F32)
        a1[...] = jnp.zeros(a1.shape, F32)

    def update(q_ref, m_ref, l_ref, a_ref, diagonal):
        st = lax.dot_general(k_ref[...], q_ref[...], (((1,), (1,)), ((), ())), preferred_element_type=F32)
        if diagonal:
            kpos = lax.broadcasted_iota(jnp.int32, (blk, blk), 0)
            qpos = lax.broadcasted_iota(jnp.int32, (blk, blk), 1)
            st = jnp.where(kpos <= qpos, st, NEG_INF)
        m_prev = m_ref[...]
        m_new = jnp.maximum(m_prev, jnp.max(st, axis=0, keepdims=True))
        alpha = jnp.exp2(m_prev - m_new)
        pt = jnp.exp2(st - m_new)
        l_ref[...] = alpha * l_ref[...] + jnp.sum(pt, axis=0, keepdims=True)
        a_ref[...] = alpha * a_ref[...] + jnp.dot(vt_ref[...], pt.astype(BF16), preferred_element_type=F32)
        m_ref[...] = m_new

    @pl.when(ki < qi)
    def _():
        update(q0_ref, m0, l0, a0, False)
        update(q1_ref, m1, l1, a1, False)

    @pl.when(ki == qi)
    def _():
        update(q0_ref, m0, l0, a0, True)
        update(q1_ref, m1, l1, a1, True)
        ot = a0[...] / l0[...] - lam_ref[0, 0] * (a1[...] / l1[...])
        yt = ot * lax.rsqrt(jnp.mean(ot * ot, axis=0, keepdims=True) + EPS) * (sub_ref[...] * out_scale)
        o_ref[...] = yt.T.astype(o_ref.dtype)


def diff_attention_prompt(q0, q1, kb, vt, lam, subln, lam_init, *, t, blk=512):
    nb = t // blk
    qi_tbl = np.concatenate([np.full(i + 1, i, np.int32) for i in range(nb)])
    ki_tbl = np.concatenate([np.arange(i + 1, dtype=np.int32) for i in range(nb)])
    kern = functools.partial(_diff_flash_kernel, blk=blk, out_scale=1.0 - lam_init)
    grid_spec = pltpu.PrefetchScalarGridSpec(
        num_scalar_prefetch=2,
        grid=(DIFF_HEADS, len(qi_tbl)),
        in_specs=[pl.BlockSpec(memory_space=pltpu.SMEM),
                  pl.BlockSpec((blk, LANES), lambda h, s, qi, ki: (qi[s], h)),
                  pl.BlockSpec((blk, LANES), lambda h, s, qi, ki: (qi[s], h)),
                  pl.BlockSpec((blk, LANES), lambda h, s, qi, ki: (ki[s], h)),
                  pl.BlockSpec((LANES, blk), lambda h, s, qi, ki: (h, ki[s])),
                  pl.BlockSpec((DIFF_VD, 1), lambda h, s, qi, ki: (0, 0))],
        out_specs=pl.BlockSpec((blk, LANES), lambda h, s, qi, ki: (qi[s], h)),
        scratch_shapes=[pltpu.VMEM((1, blk), F32), pltpu.VMEM((1, blk), F32), pltpu.VMEM((DIFF_VD, blk), F32),
                        pltpu.VMEM((1, blk), F32), pltpu.VMEM((1, blk), F32), pltpu.VMEM((DIFF_VD, blk), F32)],
    )
    return pl.pallas_call(
        kern,
        grid_spec=grid_spec,
        out_shape=jax.ShapeDtypeStruct((t, GROUP_W), BF16),
        compiler_params=_cparams(("arbitrary", "arbitrary")),
        name="diff_attention_prompt",
    )(jnp.asarray(qi_tbl), jnp.asarray(ki_tbl), lam.reshape(1, 1), q0, q1, kb, vt, subln.reshape(DIFF_VD, 1))


def _diff_paged_kernel(pt_ref, lam_ref, q_ref, kn_ref, vn_ref, sub_ref, sel_ref, qmask_ref, *rest,
                       n_pages, t_new, out_scale):
    k_pages = rest[:n_pages]
    v_pages = rest[n_pages:2 * n_pages]
    o_ref = rest[2 * n_pages]
    kpad, vpad, qpad, s_scr = rest[2 * n_pages + 1:]
    half = DIFF_HEADS * t_new
    b = pl.program_id(0)

    @pl.when(b == 0)
    def _():
        kpad[...] = jnp.zeros(kpad.shape, kpad.dtype)
        vpad[...] = jnp.zeros(vpad.shape, vpad.dtype)
        qpad[...] = jnp.zeros(qpad.shape, qpad.dtype)

    kpad[0:t_new, :] = kn_ref[...]
    vpad[0:t_new, :] = vn_ref[...]
    qpad[0:t_new, :] = q_ref[...]
    qbd = (lax.dot_general(qpad[...].astype(BF16), sel_ref[...], (((0,), (0,)), ((), ())),
                           preferred_element_type=F32) * qmask_ref[...]).astype(BF16)

    def heads_to_lanes(ref):
        return jnp.concatenate([ref[pl.ds(h, PAGE_SIZE, stride=DIFF_HEADS), :] for h in range(DIFF_HEADS)],
                               axis=1).astype(BF16)

    key =lax.broadcasted_iota(jnp.int32, (PAGE_SIZE, LANES), 0)
    qpos = lax.broadcasted_iota(jnp.int32, (PAGE_SIZE, LANES), 1) & (t_new - 1)
    m = jnp.full((1, LANES), NEG_INF, F32)
    for p in range(n_pages + 1):
        if p < n_pages:
            s = jnp.dot(heads_to_lanes(k_pages[p]), qbd, preferred_element_type=F32)
        else:
            s = jnp.dot(kpad[...].astype(BF16), qbd, preferred_element_type=F32)
            s = jnp.where(key <= qpos, s, NEG_INF)
        s_scr[p] = s
        m = jnp.maximum(m, jnp.max(s, axis=0, keepdims=True))
    l = jnp.zeros((1, LANES), F32)
    acc = jnp.zeros((LANES, GROUP_W), F32)
    for p in range(n_pages + 1):
        e = jnp.exp2(s_scr[p] - m)
        l = l + jnp.sum(e, axis=0, keepdims=True)
        vsrc = heads_to_lanes(v_pages[p]) if p < n_pages else vpad[...].astype(BF16)
        acc += lax.dot_general(e.astype(BF16), vsrc, (((0,), (0,)), ((), ())), preferred_element_type=F32)
    inv_col = jnp.broadcast_to(1.0 / l, (LANES, LANES)).T
    lam = lam_ref[0, 0]
    outs = []
    for h in range(DIFF_HEADS):
        r0, r1 = h * t_new, half + h * t_new
        cs = slice(h * DIFF_VD, (h + 1) * DIFF_VD)
        o = acc[r0:r0 + t_new, cs] * inv_col[r0:r0 + t_new, :] - lam * (acc[r1:r1 + t_new, cs] * inv_col[r1:r1 + t_new, :])
        y = o * lax.rsqrt(jnp.mean(o * o, axis=-1, keepdims=True) + EPS)
        outs.append(y * sub_ref[...] * out_scale)
    o_ref[...] = jnp.concatenate(outs, axis=1).astype(o_ref.dtype)


def diff_attention_sample(qs, kn, vn, cache_k, cache_v, layer, page_table, lam, subln, lam_init):
    nb, t_new, _ = qs.shape
    n_pages = page_table.shape[1]
    rows = PAGE_SIZE * DIFF_HEADS
    assert t_new & (t_new - 1) == 0
    half = DIFF_HEADS * t_new
    col = np.arange(LANES)
    valid = col < 2 * half
    sel = ((np.arange(PAGE_SIZE)[:, None] == (col % t_new)[None, :]) & valid[None, :])
    blk_of_col = 2 * ((col % half) // t_new) + col // half
    qmask = ((np.arange(GROUP_W)[:, None] // DIFF_QD) == blk_of_col[None, :]) & valid[None, :]
    sel = jnp.asarray(sel, BF16)
    qmask = jnp.asarray(qmask, F32)

    def page_spec(p):
        return pl.BlockSpec((None, None, rows, DIFF_VD), lambda b, pt: (layer, pt[b, p], 0, 0))

    kern = functools.partial(_diff_paged_kernel, n_pages=n_pages, t_new=t_new, out_scale=1.0 - lam_init)
    grid_spec = pltpu.PrefetchScalarGridSpec(
        num_scalar_prefetch=1,
        grid=(nb,),
        in_specs=[pl.BlockSpec(memory_space=pltpu.SMEM),
                  pl.BlockSpec((t_new, GROUP_W), lambda b, pt: (b, 0)),
                  pl.BlockSpec((t_new, GROUP_W), lambda b, pt: (b, 0)),
                  pl.BlockSpec((t_new, GROUP_W), lambda b, pt: (b, 0)),
                  pl.BlockSpec((1, LANES), lambda b, pt: (0, 0)),
                  pl.BlockSpec((PAGE_SIZE, LANES), lambda b, pt: (0, 0)),
                  pl.BlockSpec((GROUP_W, LANES), lambda b, pt: (0, 0))]
                 + [page_spec(p) for p in range(n_pages)] * 2,
        out_specs=pl.BlockSpec((t_new, GROUP_W), lambda b, pt: (b, 0)),
        scratch_shapes=[pltpu.VMEM((PAGE_SIZE, GROUP_W), F32), pltpu.VMEM((PAGE_SIZE, GROUP_W), F32),
                        pltpu.VMEM((PAGE_SIZE, GROUP_W), F32),
                        pltpu.VMEM((n_pages + 1, PAGE_SIZE, LANES), F32)],
    )
    return pl.pallas_call(
        kern,
        grid_spec=grid_spec,
        out_shape=jax.ShapeDtypeStruct((nb * t_new, GROUP_W), BF16),
        compiler_params=_cparams(("arbitrary",)),
        name="diff_attention_sample",
    )(page_table, lam.reshape(1, 1), qs.reshape(nb * t_new, GROUP_W), kn, vn, subln.reshape(1, LANES), sel, qmask,
      *([cache_k] * n_pages), *([cache_v] * n_pages))


def _interleave64(x, y):
    lane = lax.broadcasted_iota(jnp.int32, (x.shape[0], LANES), 1)
    lo = lane < RWKV_HD
    blocks = []
    for c in range(GROUP_W // LANES):
        xc = x[:, c * LANES:(c + 1) * LANES]
        yc = y[:, c * LANES:(c + 1) * LANES]
        rx = pltpu.roll(xc, RWKV_HD, axis=1)
        ry = pltpu.roll(yc, RWKV_HD, axis=1)
        blocks.append(jnp.where(lo, xc, ry))
        blocks.append(jnp.where(lo, rx, yc))
    return jnp.concatenate(blocks, axis=1)


def _rwkv_prep_kernel(cols_ref, prev_ref, mu_ref, w0_ref, a0_ref, kk_ref, ka_ref, rk_ref,
                      w2h_ref, w2l_ref, a2h_ref, a2l_ref, g2h_ref, g2l_ref, e_ref,
                      ab_ref, kr2_ref, g8_ref, v_ref, br_ref, kr_ref, bon_ref, g_ref, *, shifted):
    cols = cols_ref[...]
    if shifted:
        before = jnp.where(pl.program_id(0) == 0, 0.0, prev_ref[SUBLANES - 1:SUBLANES, :])
        first = lax.broadcasted_iota(jnp.int32, cols.shape, 0) == 0
        prev = jnp.where(first, before, pltpu.roll(cols, 1, axis=0))
    else:
        prev = prev_ref[...]
    xm = cols + (prev - cols) * mu_ref[...]
    o1 = GROUP_W
    r, k, v = xm[:, 0:o1], xm[:, o1:2 * o1], xm[:, 2 * o1:3 * o1]
    lora = xm[:, 3 * o1:3 * o1 + LANES]
    gl = xm[:, 3 * o1 + LANES:3 * o1 + 2 * LANES]
    wterm = _dot3(jnp.tanh(lora), w2h_ref[...], w2l_ref[...])
    aterm = _dot3(lora, a2h_ref[...], a2l_ref[...])
    z = -(w0_ref[...] + wterm)
    softplus = jnp.maximum(z, 0.0) + jnp.log1p(jnp.exp(-jnp.abs(z)))
    w = -softplus - 0.5
    log_decay = -jnp.exp(w)
    a = jax.nn.sigmoid(a0_ref[...] + aterm)
    g = _dot3(jax.nn.sigmoid(gl), g2h_ref[...], g2l_ref[...])
    e = e_ref[...]
    kk = k * kk_ref[...]
    kk = kk * lax.rsqrt(jnp.maximum(_dot_hilo(kk * kk, e), 1e-24))
    k2 = k * (1.0 + (a - 1.0) * ka_ref[...])
    bv = kk * a
    n = cols.shape[0]
    sub = lax.broadcasted_iota(jnp.int32, log_decay.shape, 0) & (SUBLANES - 1)
    csum = log_decay
    rsum = log_decay
    for d in (1, 2, 4):
        csum = csum + jnp.where(sub >= d, pltpu.roll(csum, d, axis=0), 0.0)
        rsum = rsum + jnp.where(sub < SUBLANES - d, pltpu.roll(rsum, n - d, axis=0), 0.0)
    gamma = jnp.exp(csum)
    inv_gamma = jnp.exp(-csum)
    rnd = lambda t: t.astype(BF16).astype(F32)
    ab_ref[...] = _interleave64(rnd(jnp.exp(csum - log_decay) * (-kk)), rnd(bv * inv_gamma))
    kr2_ref[...] = _interleave64(rnd(k2 * inv_gamma), rnd(r * gamma))
    g8_ref[...] = jnp.exp(csum + rsum - log_decay)
    v_ref[...] = v
    br_ref[...] = _dot_hilo(bv * r, e)
    kr_ref[...] = _dot_hilo(k2 * r, e)
    bon_ref[...] = _dot_hilo(r * k2 * rk_ref[...], e)
    g_ref[...] = g


def rwkv_prep(cols, prev, p, e512, *, row0, m, tm=256):
    assert row0 % tm == 0 and m % tm == 0
    b0 = row0 // tm
    shifted = prev is None
    if shifted:
        prev_arr = cols
        per8 = tm // SUBLANES
        prev_spec = pl.BlockSpec((SUBLANES, RWKV_COLS), lambda i: (jnp.maximum((b0 + i) * per8 - 1, 0), 0))
    else:
        prev_arr = prev
        prev_spec = pl.BlockSpec((tm, RWKV_COLS), lambda i: (i, 0))
    row = lambda n: pl.BlockSpec((1, n), lambda i: (0, 0))
    full = lambda a: pl.BlockSpec(a.shape, lambda i: (0, 0))
    wide = pl.BlockSpec((tm, 2 * GROUP_W), lambda i: (i, 0))
    nar = pl.BlockSpec((tm, GROUP_W), lambda i: (i, 0))
    z64 = jnp.zeros((64, GROUP_W), F32)
    w2p = jnp.concatenate([p["w2"], z64], axis=0)
    a2p = jnp.concatenate([z64, p["a2"]], axis=0)
    w2h, w2l = _split_bf16(w2p)
    a2h, a2l = _split_bf16(a2p)
    g2h, g2l = _split_bf16(p["g2"])
    mats = [w2h, w2l, a2h, a2l, g2h, g2l, e512]
    return pl.pallas_call(
        functools.partial(_rwkv_prep_kernel, shifted=shifted),
        grid=(m // tm,),
        in_specs=[pl.BlockSpec((tm, RWKV_COLS), lambda i: (b0 + i, 0)),
                  prev_spec,
                  row(RWKV_COLS), row(GROUP_W), row(GROUP_W), row(GROUP_W), row(GROUP_W), row(GROUP_W)]
                 + [full(a) for a in mats],
        out_specs=[wide, wide, nar, nar, nar, nar, nar, nar],
        out_shape=[jax.ShapeDtypeStruct((m, 2 * GROUP_W), F32)] * 2
                  + [jax.ShapeDtypeStruct((m, GROUP_W), F32)] * 6,
        compiler_params=_cparams(("parallel",)),
        name="rwkv_prep",
    )(cols, prev_arr, p["mu"].reshape(1, -1), p["w0"].reshape(1, -1), p["a0"].reshape(1, -1),
      p["kk"].reshape(1, -1), p["ka"].reshape(1, -1), p["rk"].reshape(1, -1), *mats)


def _rwkv_scan_kernel(ab_ref, kr2_ref, g8_ref, v_ref, br_ref, kr_ref, bon_ref, g_ref, lnw_ref, lnb_ref,
                      e_ref, s0_ref, o_ref, sn_ref, s_scr, y_scr, *, n_seq, t_len, pairs):
    c = pl.program_id(2)
    half = RWKV_HD
    lane_lo = lax.broadcasted_iota(jnp.int32, (half, LANES), 1) < half

    def tokens8(g, states):
        r0 = pl.multiple_of(g * SUBLANES, SUBLANES)
        rows = pl.ds(r0, SUBLANES)
        ab8, kr28, g8 = ab_ref[rows, :], kr2_ref[rows, :], g8_ref[rows, :]
        v8, br8, kr8 = v_ref[rows, :], br_ref[rows, :], kr_ref[rows, :]
        states = list(states)
        ys = [[] for _ in range(pairs)]

        def col_forms(tile, k, pp):
            top = jnp.broadcast_to(tile[k:k + 1, 2 * pp * LANES:(2 * pp + 1) * LANES], (half, LANES))
            bot = jnp.broadcast_to(tile[k:k + 1, (2 * pp + 1) * LANES:(2 * pp + 2) * LANES], (half, LANES))
            t = jnp.concatenate([top, bot], axis=0).astype(BF16).T
            return t[0:half].astype(F32), t[half:2 * half].astype(F32)

        for k in range(SUBLANES):
            for pp in range(pairs):
                s = states[pp]
                ls = slice(pp * LANES, (pp + 1) * LANES)
                ac, bc = col_forms(ab8, k, pp)
                kc, rc = col_forms(kr28, k, pp)
                u = jnp.sum(s * ac, axis=0, keepdims=True)
                yp = jnp.sum(s * rc, axis=0, keepdims=True)
                vrow = v8[k:k + 1, ls]
                ys[pp].append(yp + u * br8[k:k + 1, ls] + vrow * kr8[k:k + 1, ls])
                states[pp] = s + bc * u + kc * vrow
        for pp in range(pairs):
            ls = slice(pp * LANES, (pp + 1) * LANES)
            y_scr[rows, ls] = jnp.concatenate(ys[pp], axis=0)
            gt = jnp.broadcast_to(g8[0:1, ls], (LANES, LANES)).T
            states[pp] = states[pp] * jnp.where(lane_lo, gt[0:half], gt[half:2 * half])
        return tuple(states)

    @pl.when(c == 0)
    def _():
        s_scr[...] = s0_ref[...]

    groups = t_len // SUBLANES

    def seq(si, carry):
        states = tuple(s_scr[si, pp] for pp in range(pairs))
        states = lax.fori_loop(0, groups, lambda g, st: tokens8(si * groups + g, st), states)
        for pp in range(pairs):
            s_scr[si, pp] = states[pp]
        return carry

    lax.fori_loop(0, n_seq, seq, 0)

    e = e_ref[...]
    for pp in range(pairs):
        ls = slice(pp * LANES, (pp + 1) * LANES)
        y = y_scr[:, ls]
        mu = _dot_hilo(y, e) * (1.0 / RWKV_HD)
        d = y - mu
        var = _dot_hilo(d * d, e) * (1.0 / RWKV_HD)
        yn = d * lax.rsqrt(var + RWKV_LN_EPS) * lnw_ref[:, ls] + lnb_ref[:, ls]
        o_ref[:, ls] = ((yn + bon_ref[:, ls] * v_ref[:, ls]) * g_ref[:, ls]).astype(o_ref.dtype)

    @pl.when(c == pl.num_programs(2) - 1)
    def _():
        sn_ref[...] = s_scr[...]


def rwkv_scan(ab, kr2, g8, v, br, kr, bon, g, ln_w, ln_b, e128, s0, *, t_seq, seq_per_step, t_step, pairs=2):
    n_seq = s0.shape[0]
    chunks = t_seq // t_step
    rb = seq_per_step * t_step
    assert seq_per_step == 1 or chunks == 1
    n_pairs = RWKV_HEADS // 2
    assert n_pairs % pairs == 0
    w = pairs * LANES

    def rows(width):
        return pl.BlockSpec((rb, width), lambda p, s, c: (s * chunks + c, p))

    kern = functools.partial(_rwkv_scan_kernel, n_seq=seq_per_step, t_len=t_step, pairs=pairs)
    st = pl.BlockSpec((seq_per_step, pairs, RWKV_HD, LANES), lambda p, s, c: (s, p, 0, 0))
    return pl.pallas_call(
        kern,
        grid=(n_pairs // pairs, n_seq // seq_per_step, chunks),
        in_specs=[rows(2 * w), rows(2 * w),
                  rows(w), rows(w), rows(w), rows(w), rows(w), rows(w),
                  pl.BlockSpec((1, w), lambda p, s, c: (0, p)),
                  pl.BlockSpec((1, w), lambda p, s, c: (0, p)),
                  pl.BlockSpec((LANES, LANES), lambda p, s, c: (0, 0)),
                  st],
        out_specs=[rows(w), st],
        out_shape=[jax.ShapeDtypeStruct((n_seq * t_seq, GROUP_W), BF16),
                   jax.ShapeDtypeStruct(s0.shape, F32)],
        scratch_shapes=[pltpu.VMEM((seq_per_step, pairs, RWKV_HD, LANES), F32),
                        pltpu.VMEM((rb, w), F32)],
        compiler_params=_cparams(("arbitrary", "arbitrary", "arbitrary")),
        name="rwkv_scan",
    )(ab, kr2, g8, v, br, kr, bon, g, ln_w.reshape(1, -1), ln_b.reshape(1, -1), e128, s0)


def _s5_kernel(u_ref, bb_ref, cc_ref, pw_ref, ad_ref, d_ref, wg_ref, bg_ref, nw_ref, sr0_ref, si0_ref,
               o_ref, srn_ref, sin_ref, bur, bui, cr, ci, wgb, *, tm, per_group_state):
    i = pl.program_id(0)
    nch = S5_GROUPS * S5_N
    ngrp = tm // SUBLANES

    @pl.when(i == 0)
    def _():
        wgb[...] = wg_ref[...].astype(BF16)
        if not per_group_state:
            cr[...] = sr0_ref[...]
            ci[...] = si0_ref[...]

    u = u_ref[...]
    bu = jnp.dot(u.astype(BF16), bb_ref[...], preferred_element_type=F32)
    bur[...] = bu[:, 0:nch]
    bui[...] = bu[:, nch:2 * nch]
    sub = lax.broadcasted_iota(jnp.int32, (SUBLANES, nch), 0)

    def group(r0, c_r, c_i):
        xr = bur[pl.ds(r0, SUBLANES), :]
        xi = bui[pl.ds(r0, SUBLANES), :]
        for di, dsh in enumerate((1, 2, 4)):
            ar = ad_ref[2 * di:2 * di + 1, :]
            ai = ad_ref[2 * di + 1:2 * di + 2, :]
            keep = sub >= dsh
            sr = jnp.where(keep, pltpu.roll(xr, dsh, axis=0), 0.0)
            si = jnp.where(keep, pltpu.roll(xi, dsh, axis=0), 0.0)
            xr, xi = xr + ar * sr - ai * si, xi + ar * si + ai * sr
        pr = pw_ref[0:SUBLANES, :]
        pi = pw_ref[SUBLANES:2 * SUBLANES, :]
        xr, xi = xr + pr * c_r - pi * c_i, xi + pr * c_i + pi * c_r
        bur[pl.ds(r0, SUBLANES), :] = xr
        bui[pl.ds(r0, SUBLANES), :] = xi
        return xr[SUBLANES - 1:SUBLANES, :], xi[SUBLANES - 1:SUBLANES, :]

    def block(bi, carry):
        g0 = pl.multiple_of(bi * SUBLANES, SUBLANES)
        if per_group_state:
            st_r = sr0_ref[pl.ds(g0, SUBLANES), :]
            st_i = si0_ref[pl.ds(g0, SUBLANES), :]
            lasts_r, lasts_i = [], []
        else:
            c_r, c_i = cr[...], ci[...]
        for k in range(SUBLANES):
            r0 = pl.multiple_of((g0 + k) * SUBLANES, SUBLANES)
            if per_group_state:
                l_r, l_i = group(r0, st_r[k:k + 1, :], st_i[k:k + 1, :])
                lasts_r.append(l_r)
                lasts_i.append(l_i)
            else:
                c_r, c_i = group(r0, c_r, c_i)
        if per_group_state:
            srn_ref[pl.ds(g0, SUBLANES), :] = jnp.concatenate(lasts_r, axis=0)
            sin_ref[pl.ds(g0, SUBLANES), :] = jnp.concatenate(lasts_i, axis=0)
        else:
            cr[...] = c_r
            ci[...] = c_i
        return carry

    lax.fori_loop(0, ngrp // SUBLANES, block, 0)

    if not per_group_state:
        srn_ref[...] = cr[...]
        sin_ref[...] = ci[...]

    ccv = cc_ref[...]
    y = (jnp.dot(bur[...].astype(BF16), ccv[0:nch], preferred_element_type=F32)
         + jnp.dot(bui[...].astype(BF16), ccv[nch:2 * nch], preferred_element_type=F32))
    y = y + d_ref[...] * u
    y = 0.5 * y * (1.0 + jnp.tanh(math.sqrt(2.0 / math.pi) * (y + 0.044715 * (y * y * y))))
    gate = jnp.dot(y.astype(BF16), wgb[...], preferred_element_type=F32) + bg_ref[...]
    y = y * jax.nn.sigmoid(gate)
    y = y * lax.rsqrt(jnp.mean(y * y, axis=-1, keepdims=True) + EPS)
    o_ref[...] = (y * nw_ref[...]).astype(o_ref.dtype)


def s5_mixer(u, p, sr0, si0, *, per_group_state, row0, m, tm=256):
    assert row0 % tm == 0 and m % tm == 0
    b0 = row0 // tm
    nch = S5_GROUPS * S5_N
    lr, li = p["lam_re"], p["lam_im"]
    dt = jnp.exp(p["log_step"])[:, None]
    mag = jnp.exp(lr * dt)
    ab_re, ab_im = mag * jnp.cos(li * dt), mag * jnp.sin(li * dt)
    den = lr * lr + li * li
    cf_re = ((ab_re - 1.0) * lr + ab_im * li) / den
    cf_im = (ab_im * lr - (ab_re - 1.0) * li) / den
    bb_re = cf_re[..., None] * p["b_re"] - cf_im[..., None] * p["b_im"]
    bb_im = cf_re[..., None] * p["b_im"] + cf_im[..., None] * p["b_re"]
    eye = jnp.eye(S5_GROUPS, dtype=F32)

    def bd_in(b):
        return jnp.einsum("gnc,gh->gchn", b, eye).reshape(GROUP_W, nch)

    def bd_out(cm):
        return jnp.einsum("gcn,gh->gnhc", cm, eye).reshape(nch, GROUP_W)

    bb = jnp.concatenate([bd_in(bb_re), bd_in(bb_im)], axis=1).astype(BF16)
    cc = jnp.concatenate([bd_out(p["c_re"]), -bd_out(p["c_im"])], axis=0).astype(BF16)
    ar, ai = ab_re.reshape(1, nch), ab_im.reshape(1, nch)
    pows = [(ar, ai)]
    for _ in range(SUBLANES - 1):
        pr, pi = pows[-1]
        pows.append((pr * ar - pi * ai, pr * ai + pi * ar))
    pw = jnp.concatenate([jnp.concatenate([q[0] for q in pows], axis=0),
                          jnp.concatenate([q[1] for q in pows], axis=0)], axis=0)
    ad = jnp.concatenate([pows[0][0], pows[0][1], pows[1][0], pows[1][1], pows[3][0], pows[3][1],
                          jnp.zeros((2, nch), F32)], axis=0)
    n_state = sr0.shape[0]
    full = lambda a: pl.BlockSpec(a.shape, lambda i: (0,) * a.ndim)
    row = lambda n: pl.BlockSpec((1, n), lambda i: (0, 0))
    if per_group_state:
        st = pl.BlockSpec((tm // SUBLANES, nch), lambda i: (i, 0))
    else:
        st = pl.BlockSpec((1, nch), lambda i: (0, 0))
    kern = functools.partial(_s5_kernel, tm=tm, per_group_state=per_group_state)
    return pl.pallas_call(
        kern,
        grid=(m // tm,),
        in_specs=[pl.BlockSpec((tm, GROUP_W), lambda i: (b0 + i, 0)), full(bb), full(cc), full(pw), full(ad),
                  row(GROUP_W), full(p["w_glu"]), row(GROUP_W), row(GROUP_W), st, st],
        out_specs=[pl.BlockSpec((tm, GROUP_W), lambda i: (i, 0)), st, st],
        out_shape=[jax.ShapeDtypeStruct((m, GROUP_W), BF16),
                   jax.ShapeDtypeStruct((n_state, nch), F32), jax.ShapeDtypeStruct((n_state, nch), F32)],
        scratch_shapes=[pltpu.VMEM((tm, nch), F32), pltpu.VMEM((tm, nch), F32),
                        pltpu.VMEM((1, nch), F32), pltpu.VMEM((1, nch), F32),
                        pltpu.VMEM((GROUP_W, GROUP_W), BF16)],
        compiler_params=_cparams(("arbitrary",)),
        name="s5_mixer",
    )(u, bb, cc, pw, ad, p["d"].reshape(1, -1), p["w_glu"], p["b_glu"].reshape(1, -1),
      p["norm"].reshape(1, -1), sr0, si0)


def _ffn_up_kernel(h_ref, wg_ref, wv_ref, cwg_ref, cwv_ref, cbg_ref, cbv_ref, c0ga_ref, c0gb_ref, c0va_ref,
                   c0vb_ref, act_ref, cnga_ref, cngb_ref, cnva_ref, cnvb_ref, wgb, wvb, hg, hv, *, shift, tm, off,
                   t_cols):
    mstep = pl.program_id(1)
    hist = 2 * shift

    @pl.when(mstep == 0)
    def _():
        wgb[...] = wg_ref[...].astype(BF16)
        wvb[...] = wv_ref[...].astype(BF16)
        hg[off - hist:off - shift, :] = c0ga_ref[...]
        hg[off - shift:off, :] = c0gb_ref[...]
        hv[off - hist:off - shift, :] = c0va_ref[...]
        hv[off - shift:off, :] = c0vb_ref[...]

    if t_cols:
        k = h_ref.shape[1] // t_cols
        hb = jnp.concatenate([h_ref[:, t * k:(t + 1) * k] for t in range(t_cols)], axis=0)
    else:
        hb = h_ref[...]
    hg[off:off + tm, :] = jnp.dot(hb, wgb[...], preferred_element_type=F32)
    hv[off:off + tm, :] = jnp.dot(hb, wvb[...], preferred_element_type=F32)

    def conv(hs, cw_ref, cb_ref):
        return (cb_ref[...] + cw_ref[0:1, :] * hs[off - hist:off - hist + tm, :]
                + cw_ref[1:2, :] * hs[off - shift:off - shift + tm, :]
                + cw_ref[2:3, :] * hs[off:off + tm, :])

    gate = conv(hg, cwg_ref, cbg_ref)
    val = conv(hv, cwv_ref, cbv_ref)
    act = (gate * jax.nn.sigmoid(gate) * val).astype(act_ref.dtype)
    if t_cols:
        tn = act.shape[1]
        for t in range(t_cols):
            act_ref[:, t * tn:(t + 1) * tn] = act[t * shift:(t + 1) * shift]
    else:
        act_ref[...] = act
    tail_g = hg[off + tm - hist:off + tm, :]
    tail_v = hv[off + tm - hist:off + tm, :]
    hg[off - hist:off, :] = tail_g
    hv[off - hist:off, :] = tail_v

    @pl.when(mstep == pl.num_programs(1) - 1)
    def _():
        cnga_ref[...] = tail_g[0:shift]
        cngb_ref[...] = tail_g[shift:hist]
        cnva_ref[...] = tail_v[0:shift]
        cnvb_ref[...] = tail_v[shift:hist]


def ffn_up(h, w_up, conv_w, conv_b, c0, *, time_major, row0=0, m=None, tm=None, tn=512, layer=None):
    n_seq = c0.shape[0]
    nj = D_FF // tn
    if time_major:
        _, t_cols, k = h.shape
        shift, steps, tm = n_seq, 1, n_seq * t_cols
        h = h.reshape(n_seq, t_cols * k)
        h_spec = pl.BlockSpec((n_seq, t_cols * k), lambda j, i: (0, 0))
        act_spec = pl.BlockSpec((n_seq, t_cols * tn), lambda j, i: (0, j))
        act_shape = jax.ShapeDtypeStruct((n_seq, nj * t_cols * tn), BF16)
    else:
        k = h.shape[1]
        assert n_seq == 1 and row0 % tm == 0 and m % tm == 0
        shift, steps, b0, t_cols = 1, m // tm, row0 // tm, 0
        h_spec = pl.BlockSpec((tm, k), lambda j, i: (b0 + i, 0))
        act_spec = pl.BlockSpec((tm, tn), lambda j, i: (i, j))
        act_shape = jax.ShapeDtypeStruct((m, D_FF), BF16)
    hist = 2 * shift
    off = max(SUBLANES, hist)
    kern = functools.partial(_ffn_up_kernel, shift=shift, tm=tm, off=off, t_cols=t_cols)
    cw = jnp.concatenate([conv_w, jnp.zeros((SUBLANES - conv_w.shape[0], conv_w.shape[1]), F32)], axis=0)
    cb = conv_b.reshape(1, -1)

    c0 = c0.reshape(n_seq, 4 * D_FF)

    def c0_spec(tap, half):
        return pl.BlockSpec((n_seq, tn), lambda j, i: (0, (2 * tap + half) * nj + j))

    tap_out = pl.BlockSpec((shift, tn), lambda j, i: (0, j))
    tap_shape = jax.ShapeDtypeStruct((shift, D_FF), F32)
    outs = pl.pallas_call(
        kern,
        grid=(nj, steps),
        in_specs=[h_spec,
                  _weight_spec(w_up, layer, k, tn),
                  _weight_spec(w_up, layer, k, tn, nj),
                  pl.BlockSpec((SUBLANES, tn), lambda j, i: (0, j)),
                  pl.BlockSpec((SUBLANES, tn), lambda j, i: (0, j + nj)),
                  pl.BlockSpec((1, tn), lambda j, i: (0, j)),
                  pl.BlockSpec((1, tn), lambda j, i: (0, j + nj)),
                  c0_spec(0, 0), c0_spec(1, 0), c0_spec(0, 1), c0_spec(1, 1)],
        out_specs=[act_spec, tap_out, tap_out, tap_out, tap_out],
        out_shape=[act_shape, tap_shape, tap_shape, tap_shape, tap_shape],
        scratch_shapes=[pltpu.VMEM((k, tn), BF16), pltpu.VMEM((k, tn), BF16),
                        pltpu.VMEM((off + tm, tn), F32), pltpu.VMEM((off + tm, tn), F32)],
        compiler_params=_cparams(("parallel", "arbitrary")),
        name="ffn_up",
    )(h, w_up, w_up, cw, cw, cb, cb, c0, c0, c0, c0)
    act = outs[0]
    if time_major:
        act = jnp.transpose(act.reshape(n_seq, nj, t_cols, tn), (0, 2, 1, 3)).reshape(n_seq * t_cols, D_FF)
    return act, outs[1:]


def conv_state_from_taps(taps):
    ga, gb, va, vb = taps
    return jnp.stack([jnp.concatenate([ga, va], axis=1), jnp.concatenate([gb, vb], axis=1)], axis=1)


def _ret_rope_tables(pos):
    half = RET_HD // 2
    inv = jnp.power(RET_ROPE_BASE, -jnp.arange(half, dtype=F32) / half)
    ang = pos.astype(F32)[:, None] * inv[None, :]
    cos, sin = jnp.cos(ang), jnp.sin(ang)
    return jnp.concatenate([cos, cos], axis=1), jnp.concatenate([-sin, sin], axis=1)


def _diff_rope_tables(pos):
    half = ROPE_DIM // 2
    inv = jnp.power(ROPE_THETA, -jnp.arange(half, dtype=F32) / half)
    ang = pos.astype(F32)[:, None] * inv[None, :]
    cos, sin = jnp.cos(ang), jnp.sin(ang)
    n = pos.shape[0]
    rest = DIFF_QD - ROPE_DIM
    c = jnp.concatenate([cos, cos, jnp.ones((n, rest), F32)], axis=1)
    s1 = jnp.concatenate([-sin, jnp.zeros((n, half + rest), F32)], axis=1)
    s2 = jnp.concatenate([jnp.zeros((n, half), F32), sin, jnp.zeros((n, rest), F32)], axis=1)
    rep = LANES // DIFF_QD
    return jnp.tile(c, (1, rep)), jnp.tile(s1, (1, rep)), jnp.tile(s2, (1, rep))


def _rwkv_state_to_pairs(s):
    b = s.shape[0]
    s = s.reshape(b, RWKV_HEADS // 2, 2, RWKV_HD, RWKV_HD)
    return jnp.transpose(s, (0, 1, 4, 2, 3)).reshape(b, RWKV_HEADS // 2, RWKV_HD, LANES)


def _rwkv_state_from_pairs(s):
    b = s.shape[0]
    s = s.reshape(b, RWKV_HEADS // 2, RWKV_HD, 2, RWKV_HD)
    return jnp.transpose(s, (0, 1, 3, 4, 2)).reshape(b, RWKV_HEADS, RWKV_HD, RWKV_HD)


def kernel(x_prompt, x_sample, p_prompt, p_sample, cache_k, cache_v, page_table, state_ret, state_rwkv, state_rwkv_shift, state_s5_re, state_s5_im, state_ffn_conv, norm_mix, w_in, w_out, ret_norm_w, ret_norm_b, diff_lq1, diff_lk1, diff_lq2, diff_lk2, diff_subln, rwkv_mu, rwkv_w0, rwkv_w2, rwkv_a0, rwkv_a2, rwkv_g2, rwkv_kk, rwkv_ka, rwkv_rk, rwkv_ln_w, rwkv_ln_b, s5_lam_re, s5_lam_im, s5_log_step, s5_b_re, s5_b_im, s5_c_re, s5_c_im, s5_d, s5_w_glu, s5_b_glu, s5_norm, norm_ffn, ffn_w_up, ffn_conv_w, ffn_conv_b, ffn_w_down, norm_ple, ple_w_proj, ple_norm_e, ple_w_gate, norm_final):
    depth = w_in.shape[0]
    bp, tp, d = x_prompt.shape
    nb, ts, _ = x_sample.shape
    assert bp == 1
    mp, ms = bp * tp, nb * ts
    past_len = page_table.shape[1] * PAGE_SIZE
    nch = S5_GROUPS * S5_N

    x = jnp.concatenate([x_prompt.reshape(mp, d), x_sample.reshape(ms, d)], axis=0)
    p_all = jnp.concatenate([p_prompt.reshape(depth, mp, -1), p_sample.reshape(depth, ms, -1)], axis=1)
    pos_p = jnp.arange(tp, dtype=jnp.int32)
    pos_s = past_len + jnp.arange(ts, dtype=jnp.int32)
    ret_cos_p, ret_sin_p = _ret_rope_tables(pos_p)
    ret_cos_s, ret_sin_s = _ret_rope_tables(pos_s)
    pos_all = jnp.concatenate([pos_p, jnp.tile(pos_s, nb)])
    dc, ds1, ds2 = _diff_rope_tables(pos_all)
    cache_k2 = cache_k.reshape(cache_k.shape[0], cache_k.shape[1], PAGE_SIZE * DIFF_HEADS, DIFF_VD)
    cache_v2 = cache_v.reshape(cache_v.shape[0], cache_v.shape[1], PAGE_SIZE * DIFF_HEADS, DIFF_VD)
    head_of = jnp.arange(GROUP_W) // RWKV_HD
    e512 = (head_of[:, None] == head_of[None, :]).astype(BF16)
    e128 = e512[:LANES, :LANES]

    outs = {k: [] for k in ("k_p", "v_p", "k_s", "v_s", "ret_p", "ret_s", "rwkv_p", "rwkv_s", "sh_p", "sh_s",
                            "s5r_p", "s5i_p", "s5r_s", "s5i_s", "conv_p", "conv_s")}
    for i in range(depth):
        h = rmsnorm_rows(x, norm_mix[i], BF16)
        c_ret = matmul(h, w_in, layer=i, tn=1024, tm=1024, col0=0, ncols=RET_COLS, name="proj_ret")
        c_diff = matmul(h, w_in, layer=i, tn=512, tm=1024, col0=RET_COLS, ncols=DIFF_COLS, name="proj_diff")
        c_rwkv = matmul(h, w_in, layer=i, tn=896, tm=1024, col0=RET_COLS + DIFF_COLS, ncols=RWKV_COLS,
                        name="proj_rwkv")
        c_s5 = matmul(h, w_in, layer=i, tn=256, tm=1024, col0=RET_COLS + DIFF_COLS + RWKV_COLS, ncols=S5_COLS,
                      name="proj_s5")

        o_ret_p, s_ret_p = retention(c_ret, 0, bp, tp, ret_cos_p, ret_sin_p,
                                     jnp.zeros((bp, RET_HEADS, RET_HD, RET_HD), F32),
                                     ret_norm_w[i], ret_norm_b[i], chunk=RET_CHUNK, chunks_per_step=4)
        o_ret_s, s_ret_s = retention(c_ret, mp, nb, ts, ret_cos_s, ret_sin_s, state_ret,
                                     ret_norm_w[i], ret_norm_b[i], chunk=ts, chunks_per_step=1, layer=i)

        lam_init = 0.8 - 0.6 * math.exp(-0.3 * i)
        lam = (jnp.exp(jnp.sum(diff_lq1[i] * diff_lk1[i])) - jnp.exp(jnp.sum(diff_lq2[i] * diff_lk2[i])) + lam_init)
        q0, q1, k_new, kb, vt = diff_prep(c_diff, dc, ds1, ds2)
        v_new = c_diff[:, 2 * GROUP_W:]
        o_diff_p = diff_attention_prompt(q0, q1, kb, vt, lam, diff_subln[i], lam_init, t=mp)
        qs = (q0[mp:].astype(F32) + q1[mp:].astype(F32)).reshape(nb, ts, GROUP_W)
        o_diff_s = diff_attention_sample(qs, k_new[mp:], v_new[mp:], cache_k2, cache_v2, i, page_table,
                                         lam, diff_subln[i], lam_init)

        cr_s = c_rwkv[mp:].reshape(nb, ts, RWKV_COLS)
        prev_s = jnp.concatenate([state_rwkv_shift[i][:, None], cr_s[:, :-1]], axis=1).reshape(ms, RWKV_COLS)
        rp = dict(mu=rwkv_mu[i], w0=rwkv_w0[i], w2=rwkv_w2[i], a0=rwkv_a0[i], a2=rwkv_a2[i], g2=rwkv_g2[i],
                  kk=rwkv_kk[i], ka=rwkv_ka[i], rk=rwkv_rk[i].reshape(-1))
        lnw, lnb = rwkv_ln_w[i].reshape(-1), rwkv_ln_b[i].reshape(-1)
        o_rwkv_p, s_rwkv_p = rwkv_scan(*rwkv_prep(c_rwkv, None, rp, e512, row0=0, m=mp), lnw, lnb, e128,
                                       jnp.zeros((bp, RWKV_HEADS // 2, RWKV_HD, LANES), F32),
                                       t_seq=tp, seq_per_step=1, t_step=256, pairs=4)
        o_rwkv_s, s_rwkv_s = rwkv_scan(*rwkv_prep(c_rwkv, prev_s, rp, e512, row0=mp, m=ms), lnw, lnb, e128,
                                       _rwkv_state_to_pairs(state_rwkv[i]),
                                       t_seq=ts, seq_per_step=16, t_step=ts, pairs=4)

        sp = dict(lam_re=s5_lam_re[i], lam_im=s5_lam_im[i], log_step=s5_log_step[i], b_re=s5_b_re[i], b_im=s5_b_im[i],
                  c_re=s5_c_re[i], c_im=s5_c_im[i], d=s5_d[i], w_glu=s5_w_glu[i], b_glu=s5_b_glu[i], norm=s5_norm[i])
        o_s5_p, s5r_p, s5i_p = s5_mixer(c_s5, sp, jnp.zeros((1, nch), F32), jnp.zeros((1, nch), F32),
                                        per_group_state=False, row0=0, m=mp)
        o_s5_s, s5r_s, s5i_s = s5_mixer(c_s5, sp, state_s5_re[i].reshape(nb, nch), state_s5_im[i].reshape(nb, nch),
                                        per_group_state=True, row0=mp, m=ms)

        x = matmul_split([o_ret_p, o_diff_p, o_rwkv_p, o_s5_p], [o_ret_s, o_diff_s, o_rwkv_s, o_s5_s], w_out,
                         layer=i, tn=512, tm=1024, tiles=(x,), epilogue=lambda acc, res: res + acc, name="w_out")

        h2 = rmsnorm_rows(x, norm_ffn[i], BF16)
        act_p, taps_p = ffn_up(h2, ffn_w_up, ffn_conv_w[i], ffn_conv_b[i], jnp.zeros((bp, 2, 2 * D_FF), F32),
                               layer=i, time_major=False, row0=0, m=mp, tm=1024)
        act_s, taps_s = ffn_up(h2[mp:].reshape(nb, ts, d), ffn_w_up, ffn_conv_w[i], ffn_conv_b[i],
                               state_ffn_conv[i], layer=i, time_major=True)
        x = matmul_split([act_p], [act_s.reshape(ms, D_FF)], ffn_w_down, layer=i, tn=512, tm=256, tiles=(x,),
                         epilogue=lambda acc, res: res + acc, name="ffn_down")

        e = matmul(p_all[i], ple_w_proj, layer=i, tn=d, tm=256, rows=(ple_norm_e[i],),
                   epilogue=lambda acc, g: acc * lax.rsqrt(jnp.mean(acc * acc, axis=-1, keepdims=True) + EPS) * g,
                   name="ple_proj")
        h3 = rmsnorm_rows(x, norm_ple[i], BF16)
        x = matmul(h3, ple_w_gate, layer=i, tn=512, tm=1024, tiles=(x, e),
                   epilogue=lambda acc, res, ee: res + ee * jax.nn.sigmoid(acc), name="ple_gate")

        outs["k_p"].append(k_new[:mp].reshape(bp, tp, DIFF_HEADS, 2 * DIFF_QD))
        outs["v_p"].append(v_new[:mp].reshape(bp, tp, DIFF_HEADS, DIFF_VD))
        outs["k_s"].append(k_new[mp:].reshape(nb, ts, DIFF_HEADS, 2 * DIFF_QD))
        outs["v_s"].append(v_new[mp:].reshape(nb, ts, DIFF_HEADS, DIFF_VD))
        outs["ret_p"].append(s_ret_p)
        outs["ret_s"].append(s_ret_s)
        outs["rwkv_p"].append(_rwkv_state_from_pairs(s_rwkv_p))
        outs["rwkv_s"].append(_rwkv_state_from_pairs(s_rwkv_s))
        outs["sh_p"].append(c_rwkv[mp - 1:mp].reshape(bp, RWKV_COLS))
        outs["sh_s"].append(cr_s[:, -1])
        outs["s5r_p"].append(s5r_p.reshape(bp, S5_GROUPS, S5_N))
        outs["s5i_p"].append(s5i_p.reshape(bp, S5_GROUPS, S5_N))
        outs["s5r_s"].append(s5r_s.reshape(nb, S5_GROUPS, S5_N))
        outs["s5i_s"].append(s5i_s.reshape(nb, S5_GROUPS, S5_N))
        outs["conv_p"].append(conv_state_from_taps(taps_p))
        outs["conv_s"].append(conv_state_from_taps(taps_s))

    y = rmsnorm_rows(x, norm_final, F32)
    st = lambda k: jnp.stack(outs[k])
    return (y[:mp].reshape(bp, tp, d), y[mp:].reshape(nb, ts, d),
            st("k_p"), st("v_p"), st("k_s"), st("v_s"), st("ret_p"), st("ret_s"), st("rwkv_p"), st("rwkv_s"),
            st("sh_p"), st("sh_s"), st("s5r_p"), st("s5i_p"), st("s5r_s"), st("s5i_s"), st("conv_p"), st("conv_s"))
```

```python
import functools
import math

import jax
import jax.numpy as jnp
import numpy as np
from jax import lax
from jax.experimental import pallas as pl
from jax.experimental.pallas import tpu as pltpu

F32 = jnp.float32
BF16 = jnp.bfloat16

D_MODEL = 2048
GROUP_W = 512
RET_HEADS = 4
RET_HD = 128
RET_CHUNK = 128
RET_ROPE_BASE = 10000.0
DIFF_HEADS = 4
DIFF_VD = 128
DIFF_QD = 64
ROPE_THETA = 500000.0
ROPE_DIM = 16
PAGE_SIZE = 128
RWKV_HD = 64
RWKV_HEADS = 8
RWKV_LN_EPS = 64e-5
S5_CH = 16
S5_GROUPS = 32
S5_N = 64
D_FF = 5632
EPS = 1e-6
NEG_INF = -1e30
LOG2_E = math.log2(math.e)

RET_COLS = 4 * GROUP_W
DIFF_COLS = 3 * GROUP_W
RWKV_COLS = 3 * GROUP_W + 64 + 64 + 128
S5_COLS = GROUP_W

LANES = 128
SUBLANES = 8
VMEM_LIMIT_BYTES = 52 * 1024 * 1024


def _cparams(sem, vmem=VMEM_LIMIT_BYTES):
    return pltpu.CompilerParams(dimension_semantics=sem, vmem_limit_bytes=vmem)


def _split_bf16(x):
    hi = x.astype(BF16)
    lo = (x - hi.astype(F32)).astype(BF16)
    return hi, lo


def _dot_hilo(x, w_bf16):
    hi, lo = _split_bf16(x)
    return (jnp.dot(hi, w_bf16, preferred_element_type=F32)
            + jnp.dot(lo, w_bf16, preferred_element_type=F32))


def _dot3(x, w_hi, w_lo):
    hi, lo = _split_bf16(x)
    return (jnp.dot(hi, w_hi, preferred_element_type=F32)
            + jnp.dot(hi, w_lo, preferred_element_type=F32)
            + jnp.dot(lo, w_hi, preferred_element_type=F32))


def _rmsnorm_kernel(x_ref, g_ref, o_ref):
    x = x_ref[...]
    y = x * lax.rsqrt(jnp.mean(x * x, axis=-1, keepdims=True) + EPS)
    o_ref[...] = (y * g_ref[...]).astype(o_ref.dtype)


def rmsnorm_rows(x, g, out_dtype, tm=512):
    m, d = x.shape
    return pl.pallas_call(
        _rmsnorm_kernel,
        grid=(m // tm,),
        in_specs=[pl.BlockSpec((tm, d), lambda i: (i, 0)),
                  pl.BlockSpec((1, d), lambda i: (0, 0))],
        out_specs=pl.BlockSpec((tm, d), lambda i: (i, 0)),
        out_shape=jax.ShapeDtypeStruct((m, d), out_dtype),
        compiler_params=_cparams(("parallel",)),
        name="rmsnorm_rows",
    )(x, g.reshape(1, d))


def _rmsnorm_split_kernel(x_ref, g_ref, op_ref, os_ref, *, n_prompt_tiles):
    x = x_ref[...]
    y = (x * lax.rsqrt(jnp.mean(x * x, axis=-1, keepdims=True) + EPS)) * g_ref[...]

    @pl.when(pl.program_id(0) < n_prompt_tiles)
    def _():
        op_ref[...] = y

    @pl.when(pl.program_id(0) >= n_prompt_tiles)
    def _():
        os_ref[...] = y


def rmsnorm_rows_split(x, g, mp, tm=512):
    m, d = x.shape
    ms = m - mp
    assert mp % tm == 0 and ms % tm == 0
    npt = mp // tm
    return pl.pallas_call(
        functools.partial(_rmsnorm_split_kernel, n_prompt_tiles=npt),
        grid=(m // tm,),
        in_specs=[pl.BlockSpec((tm, d), lambda i: (i, 0)),
                  pl.BlockSpec((1, d), lambda i: (0, 0))],
        out_specs=[pl.BlockSpec((tm, d), lambda i: (jnp.minimum(i, npt - 1), 0)),
                   pl.BlockSpec((tm, d), lambda i: (jnp.maximum(i - npt, 0), 0))],
        out_shape=[jax.ShapeDtypeStruct((mp, d), F32), jax.ShapeDtypeStruct((ms, d), F32)],
        compiler_params=_cparams(("arbitrary",)),
        name="rmsnorm_final",
    )(x, g.reshape(1, d))


def _mm_kernel(x_ref, w_ref, *rest, epilogue, n_extra):
    extra = rest[:n_extra]
    o_ref, wb_ref = rest[n_extra], rest[n_extra + 1]

    @pl.when(pl.program_id(1) == 0)
    def _():
        wb_ref[...] = w_ref[...].astype(BF16)

    acc = jnp.dot(x_ref[...].astype(BF16), wb_ref[...], preferred_element_type=F32)
    if epilogue is not None:
        acc = epilogue(acc, *[e[...] for e in extra])
    o_ref[...] = acc.astype(o_ref.dtype)


def _weight_spec(w, layer, k, tn, joff=0):
    if w.ndim == 3:
        return pl.BlockSpec((None, k, tn), lambda j, i: (layer, 0, j + joff))
    return pl.BlockSpec((k, tn), lambda j, i: (0, j + joff))


def matmul(x, w, *, tn, tm=512, col0=0, ncols=None, epilogue=None, tiles=(), rows=(),
           out_dtype=F32, layer=None, name="matmul"):
    m, k = x.shape
    n = w.shape[-1] - col0 if ncols is None else ncols
    assert col0 % tn == 0 and n % tn == 0 and m % tm == 0
    joff = col0 // tn
    in_specs = [pl.BlockSpec((tm, k), lambda j, i: (i, 0)), _weight_spec(w, layer, k, tn, joff)]
    in_specs += [pl.BlockSpec((tm, tn), lambda j, i: (i, j)) for _ in tiles]
    in_specs += [pl.BlockSpec((1, tn), lambda j, i: (0, j)) for _ in rows]
    kern = functools.partial(_mm_kernel, epilogue=epilogue, n_extra=len(tiles) + len(rows))
    return pl.pallas_call(
        kern,
        grid=(n // tn, m // tm),
        in_specs=in_specs,
        out_specs=pl.BlockSpec((tm, tn), lambda j, i: (i, j)),
        out_shape=jax.ShapeDtypeStruct((m, n), out_dtype),
        scratch_shapes=[pltpu.VMEM((k, tn), BF16)],
        compiler_params=_cparams(("parallel", "arbitrary")),
        name=name,
    )(x, w, *tiles, *[r.reshape(1, -1) for r in rows])


def _mm_split_kernel(*refs, n_parts, part_k, n_prompt_tiles, epilogue, n_extra):
    xp = refs[:n_parts]
    xs = refs[n_parts:2 * n_parts]
    w_ref = refs[2 * n_parts]
    extra = refs[2 * n_parts + 1:2 * n_parts + 1 + n_extra]
    o_ref, wb_ref = refs[2 * n_parts + 1 + n_extra], refs[2 * n_parts + 2 + n_extra]
    i = pl.program_id(1)

    @pl.when(i == 0)
    def _():
        wb_ref[...] = w_ref[...].astype(BF16)

    def body(parts):
        acc = None
        for g, r in enumerate(parts):
            d = jnp.dot(r[...], wb_ref[g * part_k:(g + 1) * part_k, :], preferred_element_type=F32)
            acc = d if acc is None else acc + d
        o_ref[...] = epilogue(acc, *[e[...] for e in extra]).astype(o_ref.dtype)

    @pl.when(i < n_prompt_tiles)
    def _():
        body(xp)

    @pl.when(i >= n_prompt_tiles)
    def _():
        body(xs)


def matmul_split(xs_prompt, xs_sample, w, *, tn, tm, epilogue, tiles=(), out_dtype=F32, layer=None,
                 name="matmul_split"):
    n_parts = len(xs_prompt)
    mp, part_k = xs_prompt[0].shape
    ms = xs_sample[0].shape[0]
    k, n = w.shape[-2:]
    assert k == n_parts * part_k and mp % tm == 0 and ms % tm == 0 and n % tn == 0
    npt, nst = mp // tm, ms // tm
    in_specs = [pl.BlockSpec((tm, part_k), lambda j, i: (jnp.minimum(i, npt - 1), 0)) for _ in xs_prompt]
    in_specs += [pl.BlockSpec((tm, part_k), lambda j, i: (jnp.maximum(i - npt, 0), 0)) for _ in xs_sample]
    in_specs += [_weight_spec(w, layer, k, tn)]
    in_specs += [pl.BlockSpec((tm, tn), lambda j, i: (i, j)) for _ in tiles]
    kern = functools.partial(_mm_split_kernel, n_parts=n_parts, part_k=part_k, n_prompt_tiles=npt,
                             epilogue=epilogue, n_extra=len(tiles))
    return pl.pallas_call(
        kern,
        grid=(n // tn, npt + nst),
        in_specs=in_specs,
        out_specs=pl.BlockSpec((tm, tn), lambda j, i: (i, j)),
        out_shape=jax.ShapeDtypeStruct((mp + ms, n), out_dtype),
        scratch_shapes=[pltpu.VMEM((k, tn), BF16)],
        compiler_params=_cparams(("parallel", "arbitrary")),
        name=name,
    )(*xs_prompt, *xs_sample, w, *tiles)


def _ret_kernel(q_ref, k_ref, v_ref, g_ref, cos_ref, sin_ref, dmask_ref, qdec_ref, kdec_ref,
                cdec_ref, nw_ref, nb_ref, s0_ref, o_ref, sn_ref, s_scr, *, chunk, n_chunks):
    c = pl.program_id(1)

    @pl.when(c == 0)
    def _():
        s_scr[...] = s0_ref[...]

    cos = cos_ref[...]
    sin = sin_ref[...]
    for h in range(RET_HEADS):
        hs = slice(h * RET_HD, (h + 1) * RET_HD)
        qh = q_ref[:, hs]
        kh = k_ref[:, hs]
        qr = qh * cos + pltpu.roll(qh, RET_HD // 2, axis=1) * sin
        kr = (kh * cos + pltpu.roll(kh, RET_HD // 2, axis=1) * sin) * (RET_HD ** -0.5)
        vh = v_ref[:, hs]
        s = s_scr[h]
        outs = []
        for ci in range(n_chunks):
            rs = slice(ci * chunk, (ci + 1) * chunk)
            qc, kc, vc = qr[rs], kr[rs], vh[rs].astype(BF16)
            att = lax.dot_general(qc.astype(BF16), kc.astype(BF16), (((1,), (1,)), ((), ())),
                                  preferred_element_type=F32) * dmask_ref[h]
            o = jnp.dot(att.astype(BF16), vc, preferred_element_type=F32)
            o += jnp.dot((qc * qdec_ref[h]).astype(BF16), s.astype(BF16), preferred_element_type=F32)
            kd = (kc * kdec_ref[h]).astype(BF16)
            s = s * cdec_ref[h, 0:1, :] + lax.dot_general(kd, vc, (((0,), (0,)), ((), ())),
                                                  preferred_element_type=F32)
            outs.append(o)
        s_scr[h] = s
        o = outs[0] if n_chunks == 1 else jnp.concatenate(outs, axis=0)
        mu = jnp.mean(o, axis=-1, keepdims=True)
        var = jnp.mean(jnp.square(o - mu), axis=-1, keepdims=True)
        o = (o - mu) * lax.rsqrt(var + EPS) * nw_ref[h:h + 1, :] + nb_ref[h:h + 1, :]
        gh = g_ref[:, hs]
        o_ref[:, hs] = (o * (gh * jax.nn.sigmoid(gh))).astype(o_ref.dtype)

    @pl.when(c == pl.num_programs(1) - 1)
    def _():
        sn_ref[...] = s_scr[...]


def retention(proj, row0, n_seq, t_seq, cos, sin, s0, norm_w, norm_b, *, chunk, chunks_per_step, layer=None):
    rb = chunk * chunks_per_step
    steps = t_seq // rb
    assert row0 % rb == 0 and t_seq % rb == 0
    b0 = row0 // rb
    log_g = jnp.log1p(-jnp.exp2(-5.0 - jnp.arange(RET_HEADS, dtype=F32)))
    idx = jnp.arange(chunk, dtype=F32)
    rel = idx[:, None] - idx[None, :]
    dmask = jnp.where(rel >= 0, jnp.exp(log_g[:, None, None] * jnp.maximum(rel, 0.0)), 0.0)
    ones = jnp.ones((1, 1, RET_HD), F32)
    qdec = jnp.exp(log_g[:, None] * (idx + 1.0))[:, :, None] * ones
    kdec = jnp.exp(log_g[:, None] * (chunk - 1.0 - idx))[:, :, None] * ones
    cdec = jnp.exp(log_g * chunk)[:, None, None] * jnp.ones((1, SUBLANES, RET_HD), F32)

    def col(j):
        return pl.BlockSpec((rb, GROUP_W), lambda s, c: (b0 + s * steps + c, j))

    full = lambda shape: pl.BlockSpec(shape, lambda s, c: (0,) * len(shape))
    if s0.ndim == 5:
        s0_spec = pl.BlockSpec((None, None, RET_HEADS, RET_HD, RET_HD), lambda s, c: (layer, s, 0, 0, 0))
    else:
        s0_spec = pl.BlockSpec((None, RET_HEADS, RET_HD, RET_HD), lambda s, c: (s, 0, 0, 0))
    kern = functools.partial(_ret_kernel, chunk=chunk, n_chunks=chunks_per_step)
    return pl.pallas_call(
        kern,
        grid=(n_seq, steps),
        in_specs=[col(0), col(1), col(2), col(3),
                  pl.BlockSpec((rb, RET_HD), lambda s, c: (c, 0)),
                  pl.BlockSpec((rb, RET_HD), lambda s, c: (c, 0)),
                  full((RET_HEADS, chunk, chunk)),
                  full((RET_HEADS, chunk, RET_HD)),
                  full((RET_HEADS, chunk, RET_HD)),
                  full((RET_HEADS, SUBLANES, RET_HD)),
                  full((RET_HEADS, RET_HD)),
                  full((RET_HEADS, RET_HD)),
                  s0_spec],
        out_specs=[pl.BlockSpec((rb, GROUP_W), lambda s, c: (s * steps + c, 0)),
                   pl.BlockSpec((None, RET_HEADS, RET_HD, RET_HD), lambda s, c: (s, 0, 0, 0))],
        out_shape=[jax.ShapeDtypeStruct((n_seq * t_seq, GROUP_W), BF16),
                   jax.ShapeDtypeStruct((n_seq, RET_HEADS, RET_HD, RET_HD), F32)],
        scratch_shapes=[pltpu.VMEM((RET_HEADS, RET_HD, RET_HD), F32)],
        compiler_params=_cparams(("arbitrary", "arbitrary")),
        name="retention",
    )(proj, proj, proj, proj, cos, sin, dmask, qdec, kdec, cdec, norm_w, norm_b, s0)


def _diff_prep_kernel(q_ref, k_ref, v_ref, c_ref, s1_ref, s2_ref, *rest, n_prompt_tiles, n_alias):
    q0_ref, q1_ref, kn_ref, kb_ref, vt_ref, kp_ref, ks_ref, vp_ref, vs_ref = rest[n_alias:]
    i = pl.program_id(0)
    tm = q_ref.shape[0]
    c, s1, s2 = c_ref[...], s1_ref[...], s2_ref[...]
    lane = lax.broadcasted_iota(jnp.int32, c.shape, 1)
    lo = lane < DIFF_QD
    krs = []
    for j in range(GROUP_W // LANES):
        cs = slice(j * LANES, (j + 1) * LANES)
        q = q_ref[:, cs]
        k = k_ref[:, cs]
        qr = q * c + pltpu.roll(q, LANES - ROPE_DIM // 2, axis=1) * s1 + pltpu.roll(q, ROPE_DIM // 2, axis=1) * s2
        kr = k * c + pltpu.roll(k, LANES - ROPE_DIM // 2, axis=1) * s1 + pltpu.roll(k, ROPE_DIM // 2, axis=1) * s2
        qr = qr * (DIFF_QD ** -0.5 * LOG2_E)
        q0_ref[:, cs] = jnp.where(lo, qr, 0.0).astype(BF16)
        q1_ref[:, cs] = jnp.where(lo, 0.0, qr).astype(BF16)
        kn_ref[:, cs] = kr
        kb_ref[:, cs] = kr.astype(BF16)
        krs.append(kr)
    v = v_ref[...]
    vt_ref[...] = v.T.astype(BF16)

    def emit(k_out, v_out):
        for j in range(DIFF_HEADS):
            k_out[pl.ds(j, tm, stride=DIFF_HEADS), :] = krs[j]
            v_out[pl.ds(j, tm, stride=DIFF_HEADS), :] = v[:, j * DIFF_VD:(j + 1) * DIFF_VD]

    @pl.when(i < n_prompt_tiles)
    def _():
        emit(kp_ref, vp_ref)

    @pl.when(i >= n_prompt_tiles)
    def _():
        emit(ks_ref, vs_ref)


def diff_prep(proj, c, s1, s2, *, layer, depth, mp, kv_prev=None, tm=512):
    m = proj.shape[0]
    ms = m - mp
    assert mp % tm == 0 and ms % tm == 0
    npt = mp // tm
    col = lambda j: pl.BlockSpec((tm, GROUP_W), lambda i: (i, j))
    tab = pl.BlockSpec((tm, LANES), lambda i: (i, 0))
    out = pl.BlockSpec((tm, GROUP_W), lambda i: (i, 0))
    kv_p = pl.BlockSpec((None, tm * DIFF_HEADS, DIFF_VD), lambda i: (layer, jnp.minimum(i, npt - 1), 0))
    kv_s = pl.BlockSpec((None, tm * DIFF_HEADS, DIFF_VD), lambda i: (layer, jnp.maximum(i - npt, 0), 0))
    shape_p = jax.ShapeDtypeStruct((depth, mp * DIFF_HEADS, DIFF_VD), F32)
    shape_s = jax.ShapeDtypeStruct((depth, ms * DIFF_HEADS, DIFF_VD), F32)
    prev = () if kv_prev is None else tuple(kv_prev)
    n_alias = len(prev)
    kern = functools.partial(_diff_prep_kernel, n_prompt_tiles=npt, n_alias=n_alias)
    outs = pl.pallas_call(
        kern,
        grid=(m // tm,),
        in_specs=[col(0), col(1), col(2), tab, tab, tab] + [pl.BlockSpec(memory_space=pl.ANY)] * n_alias,
        out_specs=[out] * 4 + [pl.BlockSpec((GROUP_W, tm), lambda i: (0, i)), kv_p, kv_s, kv_p, kv_s],
        out_shape=[jax.ShapeDtypeStruct((m, GROUP_W), BF16), jax.ShapeDtypeStruct((m, GROUP_W), BF16),
                   jax.ShapeDtypeStruct((m, GROUP_W), F32), jax.ShapeDtypeStruct((m, GROUP_W), BF16),
                   jax.ShapeDtypeStruct((GROUP_W, m), BF16), shape_p, shape_s, shape_p, shape_s],
        input_output_aliases={6 + a: 5 + a for a in range(n_alias)},
        compiler_params=_cparams(("arbitrary",)),
        name="diff_prep",
    )(proj, proj, proj, c, s1, s2, *prev)
    return outs[:5], outs[5:]


def _diff_flash_kernel(qi_ref, ki_ref, lam_ref, q0_ref, q1_ref, k_ref, vt_ref, sub_ref, o_ref,
                       m0, l0, a0, m1, l1, a1, *, blk, out_scale):
    step = pl.program_id(1)
    qi = qi_ref[step]
    ki = ki_ref[step]

    @pl.when(ki == 0)
    def _():
        m0[...] = jnp.full(m0.shape, NEG_INF, F32)
        m1[...] = jnp.full(m1.shape, NEG_INF, F32)
        l0[...] = jnp.zeros(l0.shape, F32)
        l1[...] = jnp.zeros(l1.shape, F32)
        a0[...] = jnp.zeros(a0.shape, F32)
        a1[...] = jnp.zeros(a1.shape, F32)

    def update(q_ref, m_ref, l_ref, a_ref, diagonal):
        st = lax.dot_general(k_ref[...], q_ref[...], (((1,), (1,)), ((), ())), preferred_element_type=F32)
        if diagonal:
            kpos = lax.broadcasted_iota(jnp.int32, (blk, blk), 0)
            qpos = lax.broadcasted_iota(jnp.int32, (blk, blk), 1)
            st = jnp.where(kpos <= qpos, st, NEG_INF)
        m_prev = m_ref[...]
        m_new = jnp.maximum(m_prev, jnp.max(st, axis=0, keepdims=True))
        alpha = jnp.exp2(m_prev - m_new)
        pt = jnp.exp2(st - m_new)
        l_ref[...] = alpha * l_ref[...] + jnp.sum(pt, axis=0, keepdims=True)
        a_ref[...] = alpha * a_ref[...] + jnp.dot(vt_ref[...], pt.astype(BF16), preferred_element_type=F32)
        m_ref[...] = m_new

    @pl.when(ki < qi)
    def _():
        update(q0_ref, m0, l0, a0, False)
        update(q1_ref, m1, l1, a1, False)

    @pl.when(ki == qi)
    def _():
        update(q0_ref, m0, l0, a0, True)
        update(q1_ref, m1, l1, a1, True)
        ot = a0[...] / l0[...] - lam_ref[0, 0] * (a1[...] / l1[...])
        yt = ot * lax.rsqrt(jnp.mean(ot * ot, axis=0, keepdims=True) + EPS) * (sub_ref[...] * out_scale)
        o_ref[...] = yt.T.astype(o_ref.dtype)


def diff_attention_prompt(q0, q1, kb, vt, lam, subln, lam_init, *, t, blk=512):
    nb = t // blk
    qi_tbl = np.concatenate([np.full(i + 1, i, np.int32) for i in range(nb)])
    ki_tbl = np.concatenate([np.arange(i + 1, dtype=np.int32) for i in range(nb)])
    kern = functools.partial(_diff_flash_kernel, blk=blk, out_scale=1.0 - lam_init)
    grid_spec = pltpu.PrefetchScalarGridSpec(
        num_scalar_prefetch=2,
        grid=(DIFF_HEADS, len(qi_tbl)),
        in_specs=[pl.BlockSpec(memory_space=pltpu.SMEM),
                  pl.BlockSpec((blk, LANES), lambda h, s, qi, ki: (qi[s], h)),
                  pl.BlockSpec((blk, LANES), lambda h, s, qi, ki: (qi[s], h)),
                  pl.BlockSpec((blk, LANES), lambda h, s, qi, ki: (ki[s], h)),
                  pl.BlockSpec((LANES, blk), lambda h, s, qi, ki: (h, ki[s])),
                  pl.BlockSpec((DIFF_VD, 1), lambda h, s, qi, ki: (0, 0))],
        out_specs=pl.BlockSpec((blk, LANES), lambda h, s, qi, ki: (qi[s], h)),
        scratch_shapes=[pltpu.VMEM((1, blk), F32), pltpu.VMEM((1, blk), F32), pltpu.VMEM((DIFF_VD, blk), F32),
                        pltpu.VMEM((1, blk), F32), pltpu.VMEM((1, blk), F32), pltpu.VMEM((DIFF_VD, blk), F32)],
    )
    return pl.pallas_call(
        kern,
        grid_spec=grid_spec,
        out_shape=jax.ShapeDtypeStruct((t, GROUP_W), BF16),
        compiler_params=_cparams(("arbitrary", "arbitrary")),
        name="diff_attention_prompt",
    )(jnp.asarray(qi_tbl), jnp.asarray(ki_tbl), lam.reshape(1, 1), q0, q1, kb, vt, subln.reshape(DIFF_VD, 1))


def _diff_paged_kernel(pt_ref, lam_ref, q_ref, kn_ref, vn_ref, sub_ref, sel_ref, qmask_ref, *rest,
                       n_pages, t_new, out_scale):
    k_pages = rest[:n_pages]
    v_pages = rest[n_pages:2 * n_pages]
    o_ref = rest[2 * n_pages]
    kpad, vpad, qpad, s_scr = rest[2 * n_pages + 1:]
    half = DIFF_HEADS * t_new
    b = pl.program_id(0)

    @pl.when(b == 0)
    def _():
        kpad[...] = jnp.zeros(kpad.shape, kpad.dtype)
        vpad[...] = jnp.zeros(vpad.shape, vpad.dtype)
        qpad[...] = jnp.zeros(qpad.shape, qpad.dtype)

    kpad[0:t_new, :] = kn_ref[...]
    vpad[0:t_new, :] = vn_ref[...]
    qpad[0:t_new, :] = q_ref[...]
    qbd = (lax.dot_general(qpad[...].astype(BF16), sel_ref[...], (((0,), (0,)), ((), ())),
                           preferred_element_type=F32) * qmask_ref[...]).astype(BF16)

    def heads_to_lanes(ref):
        return jnp.concatenate([ref[pl.ds(h, PAGE_SIZE, stride=DIFF_HEADS), :] for h in range(DIFF_HEADS)],
                               axis=1).astype(BF16)

    key =lax.broadcasted_iota(jnp.int32, (PAGE_SIZE, LANES), 0)
    qpos = lax.broadcasted_iota(jnp.int32, (PAGE_SIZE, LANES), 1) & (t_new - 1)
    m = jnp.full((1, LANES), NEG_INF, F32)
    for p in range(n_pages + 1):
        if p < n_pages:
            s = jnp.dot(heads_to_lanes(k_pages[p]), qbd, preferred_element_type=F32)
        else:
            s = jnp.dot(kpad[...].astype(BF16), qbd, preferred_element_type=F32)
            s = jnp.where(key <= qpos, s, NEG_INF)
        s_scr[p] = s
        m = jnp.maximum(m, jnp.max(s, axis=0, keepdims=True))
    l = jnp.zeros((1, LANES), F32)
    acc = jnp.zeros((LANES, GROUP_W), F32)
    for p in range(n_pages + 1):
        e = jnp.exp2(s_scr[p] - m)
        l = l + jnp.sum(e, axis=0, keepdims=True)
        vsrc = heads_to_lanes(v_pages[p]) if p < n_pages else vpad[...].astype(BF16)
        acc += lax.dot_general(e.astype(BF16), vsrc, (((0,), (0,)), ((), ())), preferred_element_type=F32)
    inv_col = jnp.broadcast_to(1.0 / l, (LANES, LANES)).T
    lam = lam_ref[0, 0]
    outs = []
    for h in range(DIFF_HEADS):
        r0, r1 = h * t_new, half + h * t_new
        cs = slice(h * DIFF_VD, (h + 1) * DIFF_VD)
        o = acc[r0:r0 + t_new, cs] * inv_col[r0:r0 + t_new, :] - lam * (acc[r1:r1 + t_new, cs] * inv_col[r1:r1 + t_new, :])
        y = o * lax.rsqrt(jnp.mean(o * o, axis=-1, keepdims=True) + EPS)
        outs.append(y * sub_ref[...] * out_scale)
    o_ref[...] = jnp.concatenate(outs, axis=1).astype(o_ref.dtype)


def diff_attention_sample(qs, kn, proj, row0, cache_k, cache_v, layer, page_table, lam, subln, lam_init):
    nb, t_new, _ = qs.shape
    assert row0 % t_new == 0
    b0 = row0 // t_new
    n_pages = page_table.shape[1]
    rows = PAGE_SIZE * DIFF_HEADS
    assert t_new & (t_new - 1) == 0
    half = DIFF_HEADS * t_new
    col = np.arange(LANES)
    valid = col < 2 * half
    sel = ((np.arange(PAGE_SIZE)[:, None] == (col % t_new)[None, :]) & valid[None, :])
    blk_of_col = 2 * ((col % half) // t_new) + col // half
    qmask = ((np.arange(GROUP_W)[:, None] // DIFF_QD) == blk_of_col[None, :]) & valid[None, :]
    sel = jnp.asarray(sel, BF16)
    qmask = jnp.asarray(qmask, F32)

    def page_spec(p):
        return pl.BlockSpec((None, None, rows, DIFF_VD), lambda b, pt: (layer, pt[b, p], 0, 0))

    kern = functools.partial(_diff_paged_kernel, n_pages=n_pages, t_new=t_new, out_scale=1.0 - lam_init)
    grid_spec = pltpu.PrefetchScalarGridSpec(
        num_scalar_prefetch=1,
        grid=(nb,),
        in_specs=[pl.BlockSpec(memory_space=pltpu.SMEM),
                  pl.BlockSpec((t_new, GROUP_W), lambda b, pt: (b, 0)),
                  pl.BlockSpec((t_new, GROUP_W), lambda b, pt: (b0 + b, 0)),
                  pl.BlockSpec((t_new, GROUP_W), lambda b, pt: (b0 + b, 2)),
                  pl.BlockSpec((1, LANES), lambda b, pt: (0, 0)),
                  pl.BlockSpec((PAGE_SIZE, LANES), lambda b, pt: (0, 0)),
                  pl.BlockSpec((GROUP_W, LANES), lambda b, pt: (0, 0))]
                 + [page_spec(p) for p in range(n_pages)] * 2,
        out_specs=pl.BlockSpec((t_new, GROUP_W), lambda b, pt: (b, 0)),
        scratch_shapes=[pltpu.VMEM((PAGE_SIZE, GROUP_W), F32), pltpu.VMEM((PAGE_SIZE, GROUP_W), F32),
                        pltpu.VMEM((PAGE_SIZE, GROUP_W), F32),
                        pltpu.VMEM((n_pages + 1, PAGE_SIZE, LANES), F32)],
    )
    return pl.pallas_call(
        kern,
        grid_spec=grid_spec,
        out_shape=jax.ShapeDtypeStruct((nb * t_new, GROUP_W), BF16),
        compiler_params=_cparams(("arbitrary",)),
        name="diff_attention_sample",
    )(page_table, lam.reshape(1, 1), qs.reshape(nb * t_new, GROUP_W), kn, proj, subln.reshape(1, LANES), sel, qmask,
      *([cache_k] * n_pages), *([cache_v] * n_pages))


def _interleave64(x, y):
    lane = lax.broadcasted_iota(jnp.int32, (x.shape[0], LANES), 1)
    lo = lane < RWKV_HD
    blocks = []
    for c in range(GROUP_W // LANES):
        xc = x[:, c * LANES:(c + 1) * LANES]
        yc = y[:, c * LANES:(c + 1) * LANES]
        rx = pltpu.roll(xc, RWKV_HD, axis=1)
        ry = pltpu.roll(yc, RWKV_HD, axis=1)
        blocks.append(jnp.where(lo, xc, ry))
        blocks.append(jnp.where(lo, rx, yc))
    return jnp.concatenate(blocks, axis=1)


def _rwkv_prep_kernel(cols_ref, prev_ref, mu_ref, w0_ref, a0_ref, kk_ref, ka_ref, rk_ref,
                      w2h_ref, w2l_ref, a2h_ref, a2l_ref, g2h_ref, g2l_ref, e_ref,
                      ab_ref, kr2_ref, g8_ref, v_ref, br_ref, kr_ref, bon_ref, g_ref, *, shifted):
    cols = cols_ref[...]
    if shifted:
        before = jnp.where(pl.program_id(0) == 0, 0.0, prev_ref[SUBLANES - 1:SUBLANES, :])
        first = lax.broadcasted_iota(jnp.int32, cols.shape, 0) == 0
        prev = jnp.where(first, before, pltpu.roll(cols, 1, axis=0))
    else:
        prev = prev_ref[...]
    xm = cols + (prev - cols) * mu_ref[...]
    o1 = GROUP_W
    r, k, v = xm[:, 0:o1], xm[:, o1:2 * o1], xm[:, 2 * o1:3 * o1]
    lora = xm[:, 3 * o1:3 * o1 + LANES]
    gl = xm[:, 3 * o1 + LANES:3 * o1 + 2 * LANES]
    wterm = _dot3(jnp.tanh(lora), w2h_ref[...], w2l_ref[...])
    aterm = _dot3(lora, a2h_ref[...], a2l_ref[...])
    z = -(w0_ref[...] + wterm)
    softplus = jnp.maximum(z, 0.0) + jnp.log1p(jnp.exp(-jnp.abs(z)))
    w = -softplus - 0.5
    log_decay = -jnp.exp(w)
    a = jax.nn.sigmoid(a0_ref[...] + aterm)
    g = _dot3(jax.nn.sigmoid(gl), g2h_ref[...], g2l_ref[...])
    e = e_ref[...]
    kk = k * kk_ref[...]
    kk = kk * lax.rsqrt(jnp.maximum(_dot_hilo(kk * kk, e), 1e-24))
    k2 = k * (1.0 + (a - 1.0) * ka_ref[...])
    bv = kk * a
    n = cols.shape[0]
    sub = lax.broadcasted_iota(jnp.int32, log_decay.shape, 0) & (SUBLANES - 1)
    csum = log_decay
    rsum = log_decay
    for d in (1, 2, 4):
        csum = csum + jnp.where(sub >= d, pltpu.roll(csum, d, axis=0), 0.0)
        rsum = rsum + jnp.where(sub < SUBLANES - d, pltpu.roll(rsum, n - d, axis=0), 0.0)
    gamma = jnp.exp(csum)
    inv_gamma = jnp.exp(-csum)
    rnd = lambda t: t.astype(BF16).astype(F32)
    ab_ref[...] = _interleave64(rnd(jnp.exp(csum - log_decay) * (-kk)), rnd(bv * inv_gamma))
    kr2_ref[...] = _interleave64(rnd(k2 * inv_gamma), rnd(r * gamma))
    g8_ref[...] = jnp.exp(csum + rsum - log_decay)
    v_ref[...] = v
    br_ref[...] = _dot_hilo(bv * r, e)
    kr_ref[...] = _dot_hilo(k2 * r, e)
    bon_ref[...] = _dot_hilo(r * k2 * rk_ref[...], e)
    g_ref[...] = g


def rwkv_prep(cols, prev, p, e512, *, row0, m, tm=256):
    assert row0 % tm == 0 and m % tm == 0
    b0 = row0 // tm
    shifted = prev is None
    if shifted:
        prev_arr = cols
        per8 = tm // SUBLANES
        prev_spec = pl.BlockSpec((SUBLANES, RWKV_COLS), lambda i: (jnp.maximum((b0 + i) * per8 - 1, 0), 0))
    else:
        prev_arr = prev
        prev_spec = pl.BlockSpec((tm, RWKV_COLS), lambda i: (i, 0))
    row = lambda n: pl.BlockSpec((1, n), lambda i: (0, 0))
    full = lambda a: pl.BlockSpec(a.shape, lambda i: (0, 0))
    wide = pl.BlockSpec((tm, 2 * GROUP_W), lambda i: (i, 0))
    nar = pl.BlockSpec((tm, GROUP_W), lambda i: (i, 0))
    z64 = jnp.zeros((64, GROUP_W), F32)
    w2p = jnp.concatenate([p["w2"], z64], axis=0)
    a2p = jnp.concatenate([z64, p["a2"]], axis=0)
    w2h, w2l = _split_bf16(w2p)
    a2h, a2l = _split_bf16(a2p)
    g2h, g2l = _split_bf16(p["g2"])
    mats = [w2h, w2l, a2h, a2l, g2h, g2l, e512]
    return pl.pallas_call(
        functools.partial(_rwkv_prep_kernel, shifted=shifted),
        grid=(m // tm,),
        in_specs=[pl.BlockSpec((tm, RWKV_COLS), lambda i: (b0 + i, 0)),
                  prev_spec,
                  row(RWKV_COLS), row(GROUP_W), row(GROUP_W), row(GROUP_W), row(GROUP_W), row(GROUP_W)]
                 + [full(a) for a in mats],
        out_specs=[wide, wide, nar, nar, nar, nar, nar, nar],
        out_shape=[jax.ShapeDtypeStruct((m, 2 * GROUP_W), F32)] * 2
                  + [jax.ShapeDtypeStruct((m, GROUP_W), F32)] * 6,
        compiler_params=_cparams(("parallel",)),
        name="rwkv_prep",
    )(cols, prev_arr, p["mu"].reshape(1, -1), p["w0"].reshape(1, -1), p["a0"].reshape(1, -1),
      p["kk"].reshape(1, -1), p["ka"].reshape(1, -1), p["rk"].reshape(1, -1), *mats)


def _rwkv_scan_kernel(ab_ref, kr2_ref, g8_ref, v_ref, br_ref, kr_ref, bon_ref, g_ref, lnw_ref, lnb_ref,
                      e_ref, s0_ref, o_ref, sn_ref, s_scr, y_scr, *, n_seq, t_len, pairs):
    c = pl.program_id(2)
    half = RWKV_HD
    lane_lo = lax.broadcasted_iota(jnp.int32, (half, LANES), 1) < half

    def tokens8(g, states):
        r0 = pl.multiple_of(g * SUBLANES, SUBLANES)
        rows = pl.ds(r0, SUBLANES)
        ab8, kr28, g8 = ab_ref[rows, :], kr2_ref[rows, :], g8_ref[rows, :]
        v8, br8, kr8 = v_ref[rows, :], br_ref[rows, :], kr_ref[rows, :]
        states = list(states)
        ys = [[] for _ in range(pairs)]

        def col_forms(tile, k, pp):
            top = jnp.broadcast_to(tile[k:k + 1, 2 * pp * LANES:(2 * pp + 1) * LANES], (half, LANES))
            bot = jnp.broadcast_to(tile[k:k + 1, (2 * pp + 1) * LANES:(2 * pp + 2) * LANES], (half, LANES))
            t = jnp.concatenate([top, bot], axis=0).astype(BF16).T
            return t[0:half].astype(F32), t[half:2 * half].astype(F32)

        for k in range(SUBLANES):
            for pp in range(pairs):
                s = states[pp]
                ls = slice(pp * LANES, (pp + 1) * LANES)
                ac, bc = col_forms(ab8, k, pp)
                kc, rc = col_forms(kr28, k, pp)
                u = jnp.sum(s * ac, axis=0, keepdims=True)
                yp = jnp.sum(s * rc, axis=0, keepdims=True)
                vrow = v8[k:k + 1, ls]
                ys[pp].append(yp + u * br8[k:k + 1, ls] + vrow * kr8[k:k + 1, ls])
                states[pp] = s + bc * u + kc * vrow
        for pp in range(pairs):
            ls = slice(pp * LANES, (pp + 1) * LANES)
            y_scr[rows, ls] = jnp.concatenate(ys[pp], axis=0)
            gt = jnp.broadcast_to(g8[0:1, ls], (LANES, LANES)).T
            states[pp] = states[pp] * jnp.where(lane_lo, gt[0:half], gt[half:2 * half])
        return tuple(states)

    @pl.when(c == 0)
    def _():
        s_scr[...] = s0_ref[...]

    groups = t_len // SUBLANES

    def seq(si, carry):
        states = tuple(s_scr[si, pp] for pp in range(pairs))
        states = lax.fori_loop(0, groups, lambda g, st: tokens8(si * groups + g, st), states)
        for pp in range(pairs):
            s_scr[si, pp] = states[pp]
        return carry

    lax.fori_loop(0, n_seq, seq, 0)

    e = e_ref[...]
    for pp in range(pairs):
        ls = slice(pp * LANES, (pp + 1) * LANES)
        y = y_scr[:, ls]
        mu = _dot_hilo(y, e) * (1.0 / RWKV_HD)
        d = y - mu
        var = _dot_hilo(d * d, e) * (1.0 / RWKV_HD)
        yn = d * lax.rsqrt(var + RWKV_LN_EPS) * lnw_ref[:, ls] + lnb_ref[:, ls]
        o_ref[:, ls] = ((yn + bon_ref[:, ls] * v_ref[:, ls]) * g_ref[:, ls]).astype(o_ref.dtype)

    @pl.when(c == pl.num_programs(2) - 1)
    def _():
        sn_ref[...] = s_scr[...]


def rwkv_scan(ab, kr2, g8, v, br, kr, bon, g, ln_w, ln_b, e128, s0, *, t_seq, seq_per_step, t_step, pairs=2):
    n_seq = s0.shape[0]
    chunks = t_seq // t_step
    rb = seq_per_step * t_step
    assert seq_per_step == 1 or chunks == 1
    n_pairs = RWKV_HEADS // 2
    assert n_pairs % pairs == 0
    w = pairs * LANES

    def rows(width):
        return pl.BlockSpec((rb, width), lambda p, s, c: (s * chunks + c, p))

    kern = functools.partial(_rwkv_scan_kernel, n_seq=seq_per_step, t_len=t_step, pairs=pairs)
    st = pl.BlockSpec((seq_per_step, pairs, RWKV_HD, LANES), lambda p, s, c: (s, p, 0, 0))
    return pl.pallas_call(
        kern,
        grid=(n_pairs // pairs, n_seq // seq_per_step, chunks),
        in_specs=[rows(2 * w), rows(2 * w),
                  rows(w), rows(w), rows(w), rows(w), rows(w), rows(w),
                  pl.BlockSpec((1, w), lambda p, s, c: (0, p)),
                  pl.BlockSpec((1, w), lambda p, s, c: (0, p)),
                  pl.BlockSpec((LANES, LANES), lambda p, s, c: (0, 0)),
                  st],
        out_specs=[rows(w), st],
        out_shape=[jax.ShapeDtypeStruct((n_seq * t_seq, GROUP_W), BF16),
                   jax.ShapeDtypeStruct(s0.shape, F32)],
        scratch_shapes=[pltpu.VMEM((seq_per_step, pairs, RWKV_HD, LANES), F32),
                        pltpu.VMEM((rb, w), F32)],
        compiler_params=_cparams(("arbitrary", "arbitrary", "arbitrary")),
        name="rwkv_scan",
    )(ab, kr2, g8, v, br, kr, bon, g, ln_w.reshape(1, -1), ln_b.reshape(1, -1), e128, s0)


def _s5_kernel(u_ref, bb_ref, cc_ref, pw_ref, ad_ref, d_ref, wg_ref, bg_ref, nw_ref, sr0_ref, si0_ref,
               o_ref, srn_ref, sin_ref, bur, bui, cr, ci, wgb, *, tm, per_group_state):
    i = pl.program_id(0)
    nch = S5_GROUPS * S5_N
    ngrp = tm // SUBLANES

    @pl.when(i == 0)
    def _():
        wgb[...] = wg_ref[...].astype(BF16)
        if not per_group_state:
            cr[...] = sr0_ref[...]
            ci[...] = si0_ref[...]

    u = u_ref[...]
    bu = jnp.dot(u.astype(BF16), bb_ref[...], preferred_element_type=F32)
    bur[...] = bu[:, 0:nch]
    bui[...] = bu[:, nch:2 * nch]
    sub = lax.broadcasted_iota(jnp.int32, (SUBLANES, nch), 0)

    def group(r0, c_r, c_i):
        xr = bur[pl.ds(r0, SUBLANES), :]
        xi = bui[pl.ds(r0, SUBLANES), :]
        for di, dsh in enumerate((1, 2, 4)):
            ar = ad_ref[2 * di:2 * di + 1, :]
            ai = ad_ref[2 * di + 1:2 * di + 2, :]
            keep = sub >= dsh
            sr = jnp.where(keep, pltpu.roll(xr, dsh, axis=0), 0.0)
            si = jnp.where(keep, pltpu.roll(xi, dsh, axis=0), 0.0)
            xr, xi = xr + ar * sr - ai * si, xi + ar * si + ai * sr
        pr = pw_ref[0:SUBLANES, :]
        pi = pw_ref[SUBLANES:2 * SUBLANES, :]
        xr, xi = xr + pr * c_r - pi * c_i, xi + pr * c_i + pi * c_r
        bur[pl.ds(r0, SUBLANES), :] = xr
        bui[pl.ds(r0, SUBLANES), :] = xi
        return xr[SUBLANES - 1:SUBLANES, :], xi[SUBLANES - 1:SUBLANES, :]

    def block(bi, carry):
        g0 = pl.multiple_of(bi * SUBLANES, SUBLANES)
        if per_group_state:
            st_r = sr0_ref[pl.ds(g0, SUBLANES), :]
            st_i = si0_ref[pl.ds(g0, SUBLANES), :]
            lasts_r, lasts_i = [], []
        else:
            c_r, c_i = cr[...], ci[...]
        for k in range(SUBLANES):
            r0 = pl.multiple_of((g0 + k) * SUBLANES, SUBLANES)
            if per_group_state:
                l_r, l_i = group(r0, st_r[k:k + 1, :], st_i[k:k + 1, :])
                lasts_r.append(l_r)
                lasts_i.append(l_i)
            else:
                c_r, c_i = group(r0, c_r, c_i)
        if per_group_state:
            srn_ref[pl.ds(g0, SUBLANES), :] = jnp.concatenate(lasts_r, axis=0)
            sin_ref[pl.ds(g0, SUBLANES), :] = jnp.concatenate(lasts_i, axis=0)
        else:
            cr[...] = c_r
            ci[...] = c_i
        return carry

    lax.fori_loop(0, ngrp // SUBLANES, block, 0)

    if not per_group_state:
        srn_ref[...] = cr[...]
        sin_ref[...] = ci[...]

    ccv = cc_ref[...]
    y = (jnp.dot(bur[...].astype(BF16), ccv[0:nch], preferred_element_type=F32)
         + jnp.dot(bui[...].astype(BF16), ccv[nch:2 * nch], preferred_element_type=F32))
    y = y + d_ref[...] * u
    y = 0.5 * y * (1.0 + jnp.tanh(math.sqrt(2.0 / math.pi) * (y + 0.044715 * (y * y * y))))
    gate = jnp.dot(y.astype(BF16), wgb[...], preferred_element_type=F32) + bg_ref[...]
    y = y * jax.nn.sigmoid(gate)
    y = y * lax.rsqrt(jnp.mean(y * y, axis=-1, keepdims=True) + EPS)
    o_ref[...] = (y * nw_ref[...]).astype(o_ref.dtype)


def s5_mixer(u, p, sr0, si0, *, per_group_state, row0, m, tm=256):
    assert row0 % tm == 0 and m % tm == 0
    b0 = row0 // tm
    nch = S5_GROUPS * S5_N
    lr, li = p["lam_re"], p["lam_im"]
    dt = jnp.exp(p["log_step"])[:, None]
    mag = jnp.exp(lr * dt)
    ab_re, ab_im = mag * jnp.cos(li * dt), mag * jnp.sin(li * dt)
    den = lr * lr + li * li
    cf_re = ((ab_re - 1.0) * lr + ab_im * li) / den
    cf_im = (ab_im * lr - (ab_re - 1.0) * li) / den
    bb_re = cf_re[..., None] * p["b_re"] - cf_im[..., None] * p["b_im"]
    bb_im = cf_re[..., None] * p["b_im"] + cf_im[..., None] * p["b_re"]
    eye = jnp.eye(S5_GROUPS, dtype=F32)

    def bd_in(b):
        return jnp.einsum("gnc,gh->gchn", b, eye).reshape(GROUP_W, nch)

    def bd_out(cm):
        return jnp.einsum("gcn,gh->gnhc", cm, eye).reshape(nch, GROUP_W)

    bb = jnp.concatenate([bd_in(bb_re), bd_in(bb_im)], axis=1).astype(BF16)
    cc = jnp.concatenate([bd_out(p["c_re"]), -bd_out(p["c_im"])], axis=0).astype(BF16)
    ar, ai = ab_re.reshape(1, nch), ab_im.reshape(1, nch)
    pows = [(ar, ai)]
    for _ in range(SUBLANES - 1):
        pr, pi = pows[-1]
        pows.append((pr * ar - pi * ai, pr * ai + pi * ar))
    pw = jnp.concatenate([jnp.concatenate([q[0] for q in pows], axis=0),
                          jnp.concatenate([q[1] for q in pows], axis=0)], axis=0)
    ad = jnp.concatenate([pows[0][0], pows[0][1], pows[1][0], pows[1][1], pows[3][0], pows[3][1],
                          jnp.zeros((2, nch), F32)], axis=0)
    n_state = sr0.shape[0]
    full = lambda a: pl.BlockSpec(a.shape, lambda i: (0,) * a.ndim)
    row = lambda n: pl.BlockSpec((1, n), lambda i: (0, 0))
    if per_group_state:
        st = pl.BlockSpec((tm // SUBLANES, nch), lambda i: (i, 0))
    else:
        st = pl.BlockSpec((1, nch), lambda i: (0, 0))
    kern = functools.partial(_s5_kernel, tm=tm, per_group_state=per_group_state)
    return pl.pallas_call(
        kern,
        grid=(m // tm,),
        in_specs=[pl.BlockSpec((tm, GROUP_W), lambda i: (b0 + i, 0)), full(bb), full(cc), full(pw), full(ad),
                  row(GROUP_W), full(p["w_glu"]), row(GROUP_W), row(GROUP_W), st, st],
        out_specs=[pl.BlockSpec((tm, GROUP_W), lambda i: (i, 0)), st, st],
        out_shape=[jax.ShapeDtypeStruct((m, GROUP_W), BF16),
                   jax.ShapeDtypeStruct((n_state, nch), F32), jax.ShapeDtypeStruct((n_state, nch), F32)],
        scratch_shapes=[pltpu.VMEM((tm, nch), F32), pltpu.VMEM((tm, nch), F32),
                        pltpu.VMEM((1, nch), F32), pltpu.VMEM((1, nch), F32),
                        pltpu.VMEM((GROUP_W, GROUP_W), BF16)],
        compiler_params=_cparams(("arbitrary",)),
        name="s5_mixer",
    )(u, bb, cc, pw, ad, p["d"].reshape(1, -1), p["w_glu"], p["b_glu"].reshape(1, -1),
      p["norm"].reshape(1, -1), sr0, si0)


def _ffn_up_kernel(h_ref, wg_ref, wv_ref, cwg_ref, cwv_ref, cbg_ref, cbv_ref, c0ga_ref, c0gb_ref, c0va_ref,
                   c0vb_ref, act_ref, cnga_ref, cngb_ref, cnva_ref, cnvb_ref, wgb, wvb, hg, hv, *, shift, tm, off,
                   t_cols):
    mstep = pl.program_id(1)
    hist = 2 * shift

    @pl.when(mstep == 0)
    def _():
        wgb[...] = wg_ref[...].astype(BF16)
        wvb[...] = wv_ref[...].astype(BF16)
        hg[off - hist:off - shift, :] = c0ga_ref[...]
        hg[off - shift:off, :] = c0gb_ref[...]
        hv[off - hist:off - shift, :] = c0va_ref[...]
        hv[off - shift:off, :] = c0vb_ref[...]

    if t_cols:
        k = h_ref.shape[1] // t_cols
        hb = jnp.concatenate([h_ref[:, t * k:(t + 1) * k] for t in range(t_cols)], axis=0)
    else:
        hb = h_ref[...]
    hg[off:off + tm, :] = jnp.dot(hb, wgb[...], preferred_element_type=F32)
    hv[off:off + tm, :] = jnp.dot(hb, wvb[...], preferred_element_type=F32)

    def conv(hs, cw_ref, cb_ref):
        return (cb_ref[...] + cw_ref[0:1, :] * hs[off - hist:off - hist + tm, :]
                + cw_ref[1:2, :] * hs[off - shift:off - shift + tm, :]
                + cw_ref[2:3, :] * hs[off:off + tm, :])

    gate = conv(hg, cwg_ref, cbg_ref)
    val = conv(hv, cwv_ref, cbv_ref)
    act = (gate * jax.nn.sigmoid(gate) * val).astype(act_ref.dtype)
    if t_cols:
        tn = act.shape[1]
        for t in range(t_cols):
            act_ref[:, t * tn:(t + 1) * tn] = act[t * shift:(t + 1) * shift]
    else:
        act_ref[...] = act
    tail_g = hg[off + tm - hist:off + tm, :]
    tail_v = hv[off + tm - hist:off + tm, :]
    hg[off - hist:off, :] = tail_g
    hv[off - hist:off, :] = tail_v

    @pl.when(mstep == pl.num_programs(1) - 1)
    def _():
        cnga_ref[...] = tail_g[0:shift]
        cngb_ref[...] = tail_g[shift:hist]
        cnva_ref[...] = tail_v[0:shift]
        cnvb_ref[...] = tail_v[shift:hist]


def ffn_up(h, w_up, conv_w, conv_b, c0, *, time_major, row0=0, m=None, tm=None, tn=512, layer=None,
           out_rows=None):
    n_seq = c0.shape[0]
    nj = D_FF // tn
    if time_major:
        _, t_cols, k = h.shape
        shift, steps, tm = n_seq, 1, n_seq * t_cols
        h = h.reshape(n_seq, t_cols * k)
        h_spec = pl.BlockSpec((n_seq, t_cols * k), lambda j, i: (0, 0))
        act_spec = pl.BlockSpec((n_seq, t_cols * tn), lambda j, i: (0, j))
        act_shape = jax.ShapeDtypeStruct((n_seq, nj * t_cols * tn), BF16)
    else:
        k = h.shape[1]
        assert n_seq == 1 and row0 % tm == 0 and m % tm == 0
        shift, steps, b0, t_cols = 1, m // tm, row0 // tm, 0
        h_spec = pl.BlockSpec((tm, k), lambda j, i: (b0 + i, 0))
        act_spec = pl.BlockSpec((tm, tn), lambda j, i: (i, j))
        act_shape = jax.ShapeDtypeStruct((m if out_rows is None else out_rows, D_FF), BF16)
    hist = 2 * shift
    off = max(SUBLANES, hist)
    kern = functools.partial(_ffn_up_kernel, shift=shift, tm=tm, off=off, t_cols=t_cols)
    cw = jnp.concatenate([conv_w, jnp.zeros((SUBLANES - conv_w.shape[0], conv_w.shape[1]), F32)], axis=0)
    cb = conv_b.reshape(1, -1)

    c0 = c0.reshape(n_seq, 4 * D_FF)

    def c0_spec(tap, half):
        return pl.BlockSpec((n_seq, tn), lambda j, i: (0, (2 * tap + half) * nj + j))

    tap_out = pl.BlockSpec((shift, tn), lambda j, i: (0, j))
    tap_shape = jax.ShapeDtypeStruct((shift, D_FF), F32)
    outs = pl.pallas_call(
        kern,
        grid=(nj, steps),
        in_specs=[h_spec,
                  _weight_spec(w_up, layer, k, tn),
                  _weight_spec(w_up, layer, k, tn, nj),
                  pl.BlockSpec((SUBLANES, tn), lambda j, i: (0, j)),
                  pl.BlockSpec((SUBLANES, tn), lambda j, i: (0, j + nj)),
                  pl.BlockSpec((1, tn), lambda j, i: (0, j)),
                  pl.BlockSpec((1, tn), lambda j, i: (0, j + nj)),
                  c0_spec(0, 0), c0_spec(1, 0), c0_spec(0, 1), c0_spec(1, 1)],
        out_specs=[act_spec, tap_out, tap_out, tap_out, tap_out],
        out_shape=[act_shape, tap_shape, tap_shape, tap_shape, tap_shape],
        scratch_shapes=[pltpu.VMEM((k, tn), BF16), pltpu.VMEM((k, tn), BF16),
                        pltpu.VMEM((off + tm, tn), F32), pltpu.VMEM((off + tm, tn), F32)],
        compiler_params=_cparams(("parallel", "arbitrary")),
        name="ffn_up",
    )(h, w_up, w_up, cw, cw, cb, cb, c0, c0, c0, c0)
    act = outs[0]
    if time_major:
        act = jnp.transpose(act.reshape(n_seq, nj, t_cols, tn), (0, 2, 1, 3)).reshape(n_seq * t_cols, D_FF)
    return act, outs[1:]


def conv_state_from_taps(taps):
    ga, gb, va, vb = taps
    return jnp.stack([jnp.concatenate([ga, va], axis=1), jnp.concatenate([gb, vb], axis=1)], axis=1)


def _ret_rope_tables(pos):
    half = RET_HD // 2
    inv = jnp.power(RET_ROPE_BASE, -jnp.arange(half, dtype=F32) / half)
    ang = pos.astype(F32)[:, None] * inv[None, :]
    cos, sin = jnp.cos(ang), jnp.sin(ang)
    return jnp.concatenate([cos, cos], axis=1), jnp.concatenate([-sin, sin], axis=1)


def _diff_rope_tables(pos):
    half = ROPE_DIM // 2
    inv = jnp.power(ROPE_THETA, -jnp.arange(half, dtype=F32) / half)
    ang = pos.astype(F32)[:, None] * inv[None, :]
    cos, sin = jnp.cos(ang), jnp.sin(ang)
    n = pos.shape[0]
    rest = DIFF_QD - ROPE_DIM
    c = jnp.concatenate([cos, cos, jnp.ones((n, rest), F32)], axis=1)
    s1 = jnp.concatenate([-sin, jnp.zeros((n, half + rest), F32)], axis=1)
    s2 = jnp.concatenate([jnp.zeros((n, half), F32), sin, jnp.zeros((n, rest), F32)], axis=1)
    rep = LANES // DIFF_QD
    return jnp.tile(c, (1, rep)), jnp.tile(s1, (1, rep)), jnp.tile(s2, (1, rep))


def _rwkv_state_to_pairs(s):
    b = s.shape[0]
    s = s.reshape(b, RWKV_HEADS // 2, 2, RWKV_HD, RWKV_HD)
    return jnp.transpose(s, (0, 1, 4, 2, 3)).reshape(b, RWKV_HEADS // 2, RWKV_HD, LANES)


def _rwkv_state_from_pairs(s):
    b = s.shape[0]
    s = s.reshape(b, RWKV_HEADS // 2, RWKV_HD, 2, RWKV_HD)
    return jnp.transpose(s, (0, 1, 3, 4, 2)).reshape(b, RWKV_HEADS, RWKV_HD, RWKV_HD)


def kernel(x_prompt, x_sample, p_prompt, p_sample, cache_k, cache_v, page_table, state_ret, state_rwkv, state_rwkv_shift, state_s5_re, state_s5_im, state_ffn_conv, norm_mix, w_in, w_out, ret_norm_w, ret_norm_b, diff_lq1, diff_lk1, diff_lq2, diff_lk2, diff_subln, rwkv_mu, rwkv_w0, rwkv_w2, rwkv_a0, rwkv_a2, rwkv_g2, rwkv_kk, rwkv_ka, rwkv_rk, rwkv_ln_w, rwkv_ln_b, s5_lam_re, s5_lam_im, s5_log_step, s5_b_re, s5_b_im, s5_c_re, s5_c_im, s5_d, s5_w_glu, s5_b_glu, s5_norm, norm_ffn, ffn_w_up, ffn_conv_w, ffn_conv_b, ffn_w_down, norm_ple, ple_w_proj, ple_norm_e, ple_w_gate, norm_final):
    depth = w_in.shape[0]
    bp, tp, d = x_prompt.shape
    nb, ts, _ = x_sample.shape
    assert bp == 1
    mp, ms = bp * tp, nb * ts
    past_len = page_table.shape[1] * PAGE_SIZE
    nch = S5_GROUPS * S5_N

    x = jnp.concatenate([x_prompt.reshape(mp, d), x_sample.reshape(ms, d)], axis=0)
    p_all = jnp.concatenate([p_prompt.reshape(depth, mp, -1), p_sample.reshape(depth, ms, -1)], axis=1)
    pos_p = jnp.arange(tp, dtype=jnp.int32)
    pos_s = past_len + jnp.arange(ts, dtype=jnp.int32)
    ret_cos_p, ret_sin_p = _ret_rope_tables(pos_p)
    ret_cos_s, ret_sin_s = _ret_rope_tables(pos_s)
    pos_all = jnp.concatenate([pos_p, jnp.tile(pos_s, nb)])
    dc, ds1, ds2 = _diff_rope_tables(pos_all)
    cache_k2 = cache_k.reshape(cache_k.shape[0], cache_k.shape[1], PAGE_SIZE * DIFF_HEADS, DIFF_VD)
    cache_v2 = cache_v.reshape(cache_v.shape[0], cache_v.shape[1], PAGE_SIZE * DIFF_HEADS, DIFF_VD)
    head_of = jnp.arange(GROUP_W) // RWKV_HD
    e512 = (head_of[:, None] == head_of[None, :]).astype(BF16)
    e128 = e512[:LANES, :LANES]

    outs = {k: [] for k in ("ret_p", "ret_s", "rwkv_p", "rwkv_s", "sh_p", "sh_s",
                            "s5r_p", "s5i_p", "s5r_s", "s5i_s", "conv_p", "conv_s")}
    kv_out = None
    for i in range(depth):
        h = rmsnorm_rows(x, norm_mix[i], BF16)
        c_ret = matmul(h, w_in, layer=i, tn=1024, tm=1024, col0=0, ncols=RET_COLS, name="proj_ret")
        c_diff = matmul(h, w_in, layer=i, tn=512, tm=1024, col0=RET_COLS, ncols=DIFF_COLS, name="proj_diff")
        c_rwkv = matmul(h, w_in, layer=i, tn=896, tm=1024, col0=RET_COLS + DIFF_COLS, ncols=RWKV_COLS,
                        name="proj_rwkv")
        c_s5 = matmul(h, w_in, layer=i, tn=256, tm=1024, col0=RET_COLS + DIFF_COLS + RWKV_COLS, ncols=S5_COLS,
                      name="proj_s5")

        o_ret_p, s_ret_p = retention(c_ret, 0, bp, tp, ret_cos_p, ret_sin_p,
                                     jnp.zeros((bp, RET_HEADS, RET_HD, RET_HD), F32),
                                     ret_norm_w[i], ret_norm_b[i], chunk=RET_CHUNK, chunks_per_step=4)
        o_ret_s, s_ret_s = retention(c_ret, mp, nb, ts, ret_cos_s, ret_sin_s, state_ret,
                                     ret_norm_w[i], ret_norm_b[i], chunk=ts, chunks_per_step=1, layer=i)

        lam_init = 0.8 - 0.6 * math.exp(-0.3 * i)
        lam = (jnp.exp(jnp.sum(diff_lq1[i] * diff_lk1[i])) - jnp.exp(jnp.sum(diff_lq2[i] * diff_lk2[i])) + lam_init)
        (q0, q1, k_new, kb, vt), kv_out = diff_prep(c_diff, dc, ds1, ds2, layer=i, depth=depth, mp=mp,
                                                    kv_prev=kv_out)
        o_diff_p = diff_attention_prompt(q0, q1, kb, vt, lam, diff_subln[i], lam_init, t=mp)
        qs = (q0[mp:].astype(F32) + q1[mp:].astype(F32)).reshape(nb, ts, GROUP_W)
        o_diff_s = diff_attention_sample(qs, k_new, c_diff, mp, cache_k2, cache_v2, i, page_table,
                                         lam, diff_subln[i], lam_init)

        cr_s = c_rwkv[mp:].reshape(nb, ts, RWKV_COLS)
        prev_s = jnp.concatenate([state_rwkv_shift[i][:, None], cr_s[:, :-1]], axis=1).reshape(ms, RWKV_COLS)
        rp = dict(mu=rwkv_mu[i], w0=rwkv_w0[i], w2=rwkv_w2[i], a0=rwkv_a0[i], a2=rwkv_a2[i], g2=rwkv_g2[i],
                  kk=rwkv_kk[i], ka=rwkv_ka[i], rk=rwkv_rk[i].reshape(-1))
        lnw, lnb = rwkv_ln_w[i].reshape(-1), rwkv_ln_b[i].reshape(-1)
        o_rwkv_p, s_rwkv_p = rwkv_scan(*rwkv_prep(c_rwkv, None, rp, e512, row0=0, m=mp), lnw, lnb, e128,
                                       jnp.zeros((bp, RWKV_HEADS // 2, RWKV_HD, LANES), F32),
                                       t_seq=tp, seq_per_step=1, t_step=256, pairs=4)
        o_rwkv_s, s_rwkv_s = rwkv_scan(*rwkv_prep(c_rwkv, prev_s, rp, e512, row0=mp, m=ms), lnw, lnb, e128,
                                       _rwkv_state_to_pairs(state_rwkv[i]),
                                       t_seq=ts, seq_per_step=16, t_step=ts, pairs=4)

        sp = dict(lam_re=s5_lam_re[i], lam_im=s5_lam_im[i], log_step=s5_log_step[i], b_re=s5_b_re[i], b_im=s5_b_im[i],
                  c_re=s5_c_re[i], c_im=s5_c_im[i], d=s5_d[i], w_glu=s5_w_glu[i], b_glu=s5_b_glu[i], norm=s5_norm[i])
        o_s5_p, s5r_p, s5i_p = s5_mixer(c_s5, sp, jnp.zeros((1, nch), F32), jnp.zeros((1, nch), F32),
                                        per_group_state=False, row0=0, m=mp)
        o_s5_s, s5r_s, s5i_s = s5_mixer(c_s5, sp, state_s5_re[i].reshape(nb, nch), state_s5_im[i].reshape(nb, nch),
                                        per_group_state=True, row0=mp, m=ms)

        x = matmul_split([o_ret_p, o_diff_p, o_rwkv_p, o_s5_p], [o_ret_s, o_diff_s, o_rwkv_s, o_s5_s], w_out,
                         layer=i, tn=512, tm=1024, tiles=(x,), epilogue=lambda acc, res: res + acc, name="w_out")

        h2 = rmsnorm_rows(x, norm_ffn[i], BF16)
        act_p, taps_p = ffn_up(h2, ffn_w_up, ffn_conv_w[i], ffn_conv_b[i], jnp.zeros((bp, 2, 2 * D_FF), F32),
                               layer=i, time_major=False, row0=0, m=mp, tm=1024, out_rows=mp + ms)
        act_s, taps_s = ffn_up(h2[mp:].reshape(nb, ts, d), ffn_w_up, ffn_conv_w[i], ffn_conv_b[i],
                               state_ffn_conv[i], layer=i, time_major=True)
        act = lax.dynamic_update_slice(act_p, act_s, (mp, 0))
        x = matmul(act, ffn_w_down, layer=i, tn=512, tm=512, tiles=(x,),
                   epilogue=lambda acc, res: res + acc, name="ffn_down")

        e = matmul(p_all[i], ple_w_proj, layer=i, tn=d, tm=256, rows=(ple_norm_e[i],),
                   epilogue=lambda acc, g: acc * lax.rsqrt(jnp.mean(acc * acc, axis=-1, keepdims=True) + EPS) * g,
                   name="ple_proj")
        h3 = rmsnorm_rows(x, norm_ple[i], BF16)
        x = matmul(h3, ple_w_gate, layer=i, tn=512, tm=1024, tiles=(x, e),
                   epilogue=lambda acc, res, ee: res + ee * jax.nn.sigmoid(acc), name="ple_gate")

        outs["ret_p"].append(s_ret_p)
        outs["ret_s"].append(s_ret_s)
        outs["rwkv_p"].append(_rwkv_state_from_pairs(s_rwkv_p))
        outs["rwkv_s"].append(_rwkv_state_from_pairs(s_rwkv_s))
        outs["sh_p"].append(c_rwkv[mp - 1:mp].reshape(bp, RWKV_COLS))
        outs["sh_s"].append(cr_s[:, -1])
        outs["s5r_p"].append(s5r_p.reshape(bp, S5_GROUPS, S5_N))
        outs["s5i_p"].append(s5i_p.reshape(bp, S5_GROUPS, S5_N))
        outs["s5r_s"].append(s5r_s.reshape(nb, S5_GROUPS, S5_N))
        outs["s5i_s"].append(s5i_s.reshape(nb, S5_GROUPS, S5_N))
        outs["conv_p"].append(conv_state_from_taps(taps_p))
        outs["conv_s"].append(conv_state_from_taps(taps_s))

    y_p, y_s = rmsnorm_rows_split(x, norm_final, mp)
    st = lambda k: jnp.stack(outs[k])
    k_p, k_s, v_p, v_s = kv_out
    return (y_p.reshape(bp, tp, d), y_s.reshape(nb, ts, d),
            k_p.reshape(depth, bp, tp, DIFF_HEADS, DIFF_VD), v_p.reshape(depth, bp, tp, DIFF_HEADS, DIFF_VD),
            k_s.reshape(depth, nb, ts, DIFF_HEADS, DIFF_VD), v_s.reshape(depth, nb, ts, DIFF_HEADS, DIFF_VD),
            st("ret_p"), st("ret_s"), st("rwkv_p"), st("rwkv_s"),
            st("sh_p"), st("sh_s"), st("s5r_p"), st("s5i_p"), st("s5r_s"), st("s5i_s"), st("conv_p"), st("conv_s"))
```

```python
import functools
import math

import jax
import jax.numpy as jnp
import numpy as np
from jax import lax
from jax.experimental import pallas as pl
from jax.experimental.pallas import tpu as pltpu

F32 = jnp.float32
BF16 = jnp.bfloat16

D_MODEL = 2048
GROUP_W = 512
RET_HEADS = 4
RET_HD = 128
RET_CHUNK = 128
RET_ROPE_BASE = 10000.0
DIFF_HEADS = 4
DIFF_VD = 128
DIFF_QD = 64
ROPE_THETA = 500000.0
ROPE_DIM = 16
PAGE_SIZE = 128
RWKV_HD = 64
RWKV_HEADS = 8
RWKV_LN_EPS = 64e-5
S5_CH = 16
S5_GROUPS = 32
S5_N = 64
D_FF = 5632
EPS = 1e-6
NEG_INF = -1e30
LOG2_E = math.log2(math.e)

RET_COLS = 4 * GROUP_W
DIFF_COLS = 3 * GROUP_W
RWKV_COLS = 3 * GROUP_W + 64 + 64 + 128
S5_COLS = GROUP_W

LANES = 128
SUBLANES = 8
VMEM_LIMIT_BYTES = 52 * 1024 * 1024


def _cparams(sem, vmem=VMEM_LIMIT_BYTES):
    return pltpu.CompilerParams(dimension_semantics=sem, vmem_limit_bytes=vmem)


def _split_bf16(x):
    hi = x.astype(BF16)
    lo = (x - hi.astype(F32)).astype(BF16)
    return hi, lo


def _dot_hilo(x, w_bf16):
    hi, lo = _split_bf16(x)
    return (jnp.dot(hi, w_bf16, preferred_element_type=F32)
            + jnp.dot(lo, w_bf16, preferred_element_type=F32))


def _dot3(x, w_hi, w_lo):
    hi, lo = _split_bf16(x)
    return (jnp.dot(hi, w_hi, preferred_element_type=F32)
            + jnp.dot(hi, w_lo, preferred_element_type=F32)
            + jnp.dot(lo, w_hi, preferred_element_type=F32))


def _rmsnorm_kernel(x_ref, g_ref, o_ref):
    x = x_ref[...]
    y = x * lax.rsqrt(jnp.mean(x * x, axis=-1, keepdims=True) + EPS)
    o_ref[...] = (y * g_ref[...]).astype(o_ref.dtype)


def rmsnorm_rows(x, g, out_dtype, tm=512):
    m, d = x.shape
    return pl.pallas_call(
        _rmsnorm_kernel,
        grid=(m // tm,),
        in_specs=[pl.BlockSpec((tm, d), lambda i: (i, 0)),
                  pl.BlockSpec((1, d), lambda i: (0, 0))],
        out_specs=pl.BlockSpec((tm, d), lambda i: (i, 0)),
        out_shape=jax.ShapeDtypeStruct((m, d), out_dtype),
        compiler_params=_cparams(("parallel",)),
        name="rmsnorm_rows",
    )(x, g.reshape(1, d))


def _rmsnorm_split_kernel(x_ref, g_ref, op_ref, os_ref, *, n_prompt_tiles):
    x = x_ref[...]
    y = (x * lax.rsqrt(jnp.mean(x * x, axis=-1, keepdims=True) + EPS)) * g_ref[...]

    @pl.when(pl.program_id(0) < n_prompt_tiles)
    def _():
        op_ref[...] = y

    @pl.when(pl.program_id(0) >= n_prompt_tiles)
    def _():
        os_ref[...] = y


def rmsnorm_rows_split(x, g, mp, tm=512):
    m, d = x.shape
    ms = m - mp
    assert mp % tm == 0 and ms % tm == 0
    npt = mp // tm
    return pl.pallas_call(
        functools.partial(_rmsnorm_split_kernel, n_prompt_tiles=npt),
        grid=(m // tm,),
        in_specs=[pl.BlockSpec((tm, d), lambda i: (i, 0)),
                  pl.BlockSpec((1, d), lambda i: (0, 0))],
        out_specs=[pl.BlockSpec((tm, d), lambda i: (jnp.minimum(i, npt - 1), 0)),
                   pl.BlockSpec((tm, d), lambda i: (jnp.maximum(i - npt, 0), 0))],
        out_shape=[jax.ShapeDtypeStruct((mp, d), F32), jax.ShapeDtypeStruct((ms, d), F32)],
        compiler_params=_cparams(("arbitrary",)),
        name="rmsnorm_final",
    )(x, g.reshape(1, d))


def _mm_kernel(x_ref, w_ref, *rest, epilogue, n_extra):
    extra = rest[:n_extra]
    o_ref, wb_ref = rest[n_extra], rest[n_extra + 1]

    @pl.when(pl.program_id(1) == 0)
    def _():
        wb_ref[...] = w_ref[...].astype(BF16)

    acc = jnp.dot(x_ref[...].astype(BF16), wb_ref[...], preferred_element_type=F32)
    if epilogue is not None:
        acc = epilogue(acc, *[e[...] for e in extra])
    o_ref[...] = acc.astype(o_ref.dtype)


def _weight_spec(w, layer, k, tn, joff=0):
    if w.ndim == 3:
        return pl.BlockSpec((None, k, tn), lambda j, i: (layer, 0, j + joff))
    return pl.BlockSpec((k, tn), lambda j, i: (0, j + joff))


def matmul(x, w, *, tn, tm=512, col0=0, ncols=None, epilogue=None, tiles=(), rows=(),
           out_dtype=F32, layer=None, name="matmul"):
    m, k = x.shape
    n = w.shape[-1] - col0 if ncols is None else ncols
    assert col0 % tn == 0 and n % tn == 0 and m % tm == 0
    joff = col0 // tn
    in_specs = [pl.BlockSpec((tm, k), lambda j, i: (i, 0)), _weight_spec(w, layer, k, tn, joff)]
    in_specs += [pl.BlockSpec((tm, tn), lambda j, i: (i, j)) for _ in tiles]
    in_specs += [pl.BlockSpec((1, tn), lambda j, i: (0, j)) for _ in rows]
    kern = functools.partial(_mm_kernel, epilogue=epilogue, n_extra=len(tiles) + len(rows))
    return pl.pallas_call(
        kern,
        grid=(n // tn, m // tm),
        in_specs=in_specs,
        out_specs=pl.BlockSpec((tm, tn), lambda j, i: (i, j)),
        out_shape=jax.ShapeDtypeStruct((m, n), out_dtype),
        scratch_shapes=[pltpu.VMEM((k, tn), BF16)],
        compiler_params=_cparams(("parallel", "arbitrary")),
        name=name,
    )(x, w, *tiles, *[r.reshape(1, -1) for r in rows])


def _mm_split_kernel(*refs, n_parts, part_k, n_prompt_tiles, epilogue, n_extra):
    xp = refs[:n_parts]
    xs = refs[n_parts:2 * n_parts]
    w_ref = refs[2 * n_parts]
    extra = refs[2 * n_parts + 1:2 * n_parts + 1 + n_extra]
    o_ref, wb_ref = refs[2 * n_parts + 1 + n_extra], refs[2 * n_parts + 2 + n_extra]
    i = pl.program_id(1)

    @pl.when(i == 0)
    def _():
        wb_ref[...] = w_ref[...].astype(BF16)

    def body(parts):
        acc = None
        for g, r in enumerate(parts):
            d = jnp.dot(r[...], wb_ref[g * part_k:(g + 1) * part_k, :], preferred_element_type=F32)
            acc = d if acc is None else acc + d
        o_ref[...] = epilogue(acc, *[e[...] for e in extra]).astype(o_ref.dtype)

    @pl.when(i < n_prompt_tiles)
    def _():
        body(xp)

    @pl.when(i >= n_prompt_tiles)
    def _():
        body(xs)


def matmul_split(xs_prompt, xs_sample, w, *, tn, tm, epilogue, tiles=(), out_dtype=F32, layer=None,
                 name="matmul_split"):
    n_parts = len(xs_prompt)
    mp, part_k = xs_prompt[0].shape
    ms = xs_sample[0].shape[0]
    k, n = w.shape[-2:]
    assert k == n_parts * part_k and mp % tm == 0 and ms % tm == 0 and n % tn == 0
    npt, nst = mp // tm, ms // tm
    in_specs = [pl.BlockSpec((tm, part_k), lambda j, i: (jnp.minimum(i, npt - 1), 0)) for _ in xs_prompt]
    in_specs += [pl.BlockSpec((tm, part_k), lambda j, i: (jnp.maximum(i - npt, 0), 0)) for _ in xs_sample]
    in_specs += [_weight_spec(w, layer, k, tn)]
    in_specs += [pl.BlockSpec((tm, tn), lambda j, i: (i, j)) for _ in tiles]
    kern = functools.partial(_mm_split_kernel, n_parts=n_parts, part_k=part_k, n_prompt_tiles=npt,
                             epilogue=epilogue, n_extra=len(tiles))
    return pl.pallas_call(
        kern,
        grid=(n // tn, npt + nst),
        in_specs=in_specs,
        out_specs=pl.BlockSpec((tm, tn), lambda j, i: (i, j)),
        out_shape=jax.ShapeDtypeStruct((mp + ms, n), out_dtype),
        scratch_shapes=[pltpu.VMEM((k, tn), BF16)],
        compiler_params=_cparams(("parallel", "arbitrary")),
        name=name,
    )(*xs_prompt, *xs_sample, w, *tiles)


def _ret_kernel(q_ref, k_ref, v_ref, g_ref, cos_ref, sin_ref, dmask_ref, qdec_ref, kdec_ref,
                cdec_ref, nw_ref, nb_ref, s0_ref, o_ref, sn_ref, s_scr, *, chunk, n_chunks):
    c = pl.program_id(1)

    @pl.when(c == 0)
    def _():
        s_scr[...] = s0_ref[...]

    cos = cos_ref[...]
    sin = sin_ref[...]
    for h in range(RET_HEADS):
        hs = slice(h * RET_HD, (h + 1) * RET_HD)
        qh = q_ref[:, hs]
        kh = k_ref[:, hs]
        qr = qh * cos + pltpu.roll(qh, RET_HD // 2, axis=1) * sin
        kr = (kh * cos + pltpu.roll(kh, RET_HD // 2, axis=1) * sin) * (RET_HD ** -0.5)
        vh = v_ref[:, hs]
        s = s_scr[h]
        outs = []
        for ci in range(n_chunks):
            rs = slice(ci * chunk, (ci + 1) * chunk)
            qc, kc, vc = qr[rs], kr[rs], vh[rs].astype(BF16)
            att = lax.dot_general(qc.astype(BF16), kc.astype(BF16), (((1,), (1,)), ((), ())),
                                  preferred_element_type=F32) * dmask_ref[h]
            o = jnp.dot(att.astype(BF16), vc, preferred_element_type=F32)
            o += jnp.dot((qc * qdec_ref[h]).astype(BF16), s.astype(BF16), preferred_element_type=F32)
            kd = (kc * kdec_ref[h]).astype(BF16)
            s = s * cdec_ref[h, 0:1, :] + lax.dot_general(kd, vc, (((0,), (0,)), ((), ())),
                                                  preferred_element_type=F32)
            outs.append(o)
        s_scr[h] = s
        o = outs[0] if n_chunks == 1 else jnp.concatenate(outs, axis=0)
        mu = jnp.mean(o, axis=-1, keepdims=True)
        var = jnp.mean(jnp.square(o - mu), axis=-1, keepdims=True)
        o = (o - mu) * lax.rsqrt(var + EPS) * nw_ref[h:h + 1, :] + nb_ref[h:h + 1, :]
        gh = g_ref[:, hs]
        o_ref[:, hs] = (o * (gh * jax.nn.sigmoid(gh))).astype(o_ref.dtype)

    @pl.when(c == pl.num_programs(1) - 1)
    def _():
        sn_ref[...] = s_scr[...]


def retention(proj, row0, n_seq, t_seq, cos, sin, s0, norm_w, norm_b, *, chunk, chunks_per_step, layer=None):
    rb = chunk * chunks_per_step
    steps = t_seq // rb
    assert row0 % rb == 0 and t_seq % rb == 0
    b0 = row0 // rb
    log_g = jnp.log1p(-jnp.exp2(-5.0 - jnp.arange(RET_HEADS, dtype=F32)))
    idx = jnp.arange(chunk, dtype=F32)
    rel = idx[:, None] - idx[None, :]
    dmask = jnp.where(rel >= 0, jnp.exp(log_g[:, None, None] * jnp.maximum(rel, 0.0)), 0.0)
    ones = jnp.ones((1, 1, RET_HD), F32)
    qdec = jnp.exp(log_g[:, None] * (idx + 1.0))[:, :, None] * ones
    kdec = jnp.exp(log_g[:, None] * (chunk - 1.0 - idx))[:, :, None] * ones
    cdec = jnp.exp(log_g * chunk)[:, None, None] * jnp.ones((1, SUBLANES, RET_HD), F32)

    def col(j):
        return pl.BlockSpec((rb, GROUP_W), lambda s, c: (b0 + s * steps + c, j))

    full = lambda shape: pl.BlockSpec(shape, lambda s, c: (0,) * len(shape))
    if s0.ndim == 5:
        s0_spec = pl.BlockSpec((None, None, RET_HEADS, RET_HD, RET_HD), lambda s, c: (layer, s, 0, 0, 0))
    else:
        s0_spec = pl.BlockSpec((None, RET_HEADS, RET_HD, RET_HD), lambda s, c: (s, 0, 0, 0))
    kern = functools.partial(_ret_kernel, chunk=chunk, n_chunks=chunks_per_step)
    return pl.pallas_call(
        kern,
        grid=(n_seq, steps),
        in_specs=[col(0), col(1), col(2), col(3),
                  pl.BlockSpec((rb, RET_HD), lambda s, c: (c, 0)),
                  pl.BlockSpec((rb, RET_HD), lambda s, c: (c, 0)),
                  full((RET_HEADS, chunk, chunk)),
                  full((RET_HEADS, chunk, RET_HD)),
                  full((RET_HEADS, chunk, RET_HD)),
                  full((RET_HEADS, SUBLANES, RET_HD)),
                  full((RET_HEADS, RET_HD)),
                  full((RET_HEADS, RET_HD)),
                  s0_spec],
        out_specs=[pl.BlockSpec((rb, GROUP_W), lambda s, c: (s * steps + c, 0)),
                   pl.BlockSpec((None, RET_HEADS, RET_HD, RET_HD), lambda s, c: (s, 0, 0, 0))],
        out_shape=[jax.ShapeDtypeStruct((n_seq * t_seq, GROUP_W), BF16),
                   jax.ShapeDtypeStruct((n_seq, RET_HEADS, RET_HD, RET_HD), F32)],
        scratch_shapes=[pltpu.VMEM((RET_HEADS, RET_HD, RET_HD), F32)],
        compiler_params=_cparams(("arbitrary", "arbitrary")),
        name="retention",
    )(proj, proj, proj, proj, cos, sin, dmask, qdec, kdec, cdec, norm_w, norm_b, s0)


def _diff_prep_kernel(q_ref, k_ref, v_ref, c_ref, s1_ref, s2_ref, *rest, n_prompt_tiles, n_alias):
    q0_ref, q1_ref, kn_ref, kb_ref, vt_ref, kp_ref, ks_ref, vp_ref, vs_ref = rest[n_alias:]
    i = pl.program_id(0)
    tm = q_ref.shape[0]
    c, s1, s2 = c_ref[...], s1_ref[...], s2_ref[...]
    lane = lax.broadcasted_iota(jnp.int32, c.shape, 1)
    lo = lane < DIFF_QD
    krs = []
    for j in range(GROUP_W // LANES):
        cs = slice(j * LANES, (j + 1) * LANES)
        q = q_ref[:, cs]
        k = k_ref[:, cs]
        qr = q * c + pltpu.roll(q, LANES - ROPE_DIM // 2, axis=1) * s1 + pltpu.roll(q, ROPE_DIM // 2, axis=1) * s2
        kr = k * c + pltpu.roll(k, LANES - ROPE_DIM // 2, axis=1) * s1 + pltpu.roll(k, ROPE_DIM // 2, axis=1) * s2
        qr = qr * (DIFF_QD ** -0.5 * LOG2_E)
        q0_ref[:, cs] = jnp.where(lo, qr, 0.0).astype(BF16)
        q1_ref[:, cs] = jnp.where(lo, 0.0, qr).astype(BF16)
        kn_ref[:, cs] = kr
        kb_ref[:, cs] = kr.astype(BF16)
        krs.append(kr)
    v = v_ref[...]
    vt_ref[...] = v.T.astype(BF16)

    def emit(k_out, v_out):
        for j in range(DIFF_HEADS):
            k_out[pl.ds(j, tm, stride=DIFF_HEADS), :] = krs[j]
            v_out[pl.ds(j, tm, stride=DIFF_HEADS), :] = v[:, j * DIFF_VD:(j + 1) * DIFF_VD]

    @pl.when(i < n_prompt_tiles)
    def _():
        emit(kp_ref, vp_ref)

    @pl.when(i >= n_prompt_tiles)
    def _():
        emit(ks_ref, vs_ref)


def diff_prep(proj, c, s1, s2, *, layer, depth, mp, kv_prev=None, tm=512):
    m = proj.shape[0]
    ms = m - mp
    assert mp % tm == 0 and ms % tm == 0
    npt = mp // tm
    col = lambda j: pl.BlockSpec((tm, GROUP_W), lambda i: (i, j))
    tab = pl.BlockSpec((tm, LANES), lambda i: (i, 0))
    out = pl.BlockSpec((tm, GROUP_W), lambda i: (i, 0))
    kv_p = pl.BlockSpec((None, tm * DIFF_HEADS, DIFF_VD), lambda i: (layer, jnp.minimum(i, npt - 1), 0))
    kv_s = pl.BlockSpec((None, tm * DIFF_HEADS, DIFF_VD), lambda i: (layer, jnp.maximum(i - npt, 0), 0))
    shape_p = jax.ShapeDtypeStruct((depth, mp * DIFF_HEADS, DIFF_VD), F32)
    shape_s = jax.ShapeDtypeStruct((depth, ms * DIFF_HEADS, DIFF_VD), F32)
    prev = () if kv_prev is None else tuple(kv_prev)
    n_alias = len(prev)
    kern = functools.partial(_diff_prep_kernel, n_prompt_tiles=npt, n_alias=n_alias)
    outs = pl.pallas_call(
        kern,
        grid=(m // tm,),
        in_specs=[col(0), col(1), col(2), tab, tab, tab] + [pl.BlockSpec(memory_space=pl.ANY)] * n_alias,
        out_specs=[out] * 4 + [pl.BlockSpec((GROUP_W, tm), lambda i: (0, i)), kv_p, kv_s, kv_p, kv_s],
        out_shape=[jax.ShapeDtypeStruct((m, GROUP_W), BF16), jax.ShapeDtypeStruct((m, GROUP_W), BF16),
                   jax.ShapeDtypeStruct((m, GROUP_W), F32), jax.ShapeDtypeStruct((m, GROUP_W), BF16),
                   jax.ShapeDtypeStruct((GROUP_W, m), BF16), shape_p, shape_s, shape_p, shape_s],
        input_output_aliases={6 + a: 5 + a for a in range(n_alias)},
        compiler_params=_cparams(("arbitrary",)),
        name="diff_prep",
    )(proj, proj, proj, c, s1, s2, *prev)
    return outs[:5], outs[5:]


def _diff_flash_kernel(qi_ref, ki_ref, lam_ref, q0_ref, q1_ref, k_ref, vt_ref, sub_ref, o_ref,
                       m0, l0, a0, m1, l1, a1, *, bq, bk, out_scale):
    step = pl.program_id(1)
    qi = qi_ref[step]
    ki = ki_ref[step]
    ratio = bq // bk

    @pl.when(ki == 0)
    def _():
        m0[...] = jnp.full(m0.shape, NEG_INF, F32)
        m1[...] = jnp.full(m1.shape, NEG_INF, F32)
        l0[...] = jnp.zeros(l0.shape, F32)
        l1[...] = jnp.zeros(l1.shape, F32)
        a0[...] = jnp.zeros(a0.shape, F32)
        a1[...] = jnp.zeros(a1.shape, F32)

    def update(q_ref, m_ref, l_ref, a_ref, diagonal):
        st = lax.dot_general(k_ref[...], q_ref[...], (((1,), (1,)), ((), ())), preferred_element_type=F32)
        if diagonal:
            kpos = ki * bk + lax.broadcasted_iota(jnp.int32, (bk, bq), 0)
            qpos = qi * bq + lax.broadcasted_iota(jnp.int32, (bk, bq), 1)
            st = jnp.where(kpos <= qpos, st, NEG_INF)
        m_prev = m_ref[...]
        m_new = jnp.maximum(m_prev, jnp.max(st, axis=0, keepdims=True))
        alpha = jnp.exp2(m_prev - m_new)
        pt = jnp.exp2(st - m_new)
        l_ref[...] = alpha * l_ref[...] + jnp.sum(pt, axis=0, keepdims=True)
        a_ref[...] = alpha * a_ref[...] + jnp.dot(vt_ref[...], pt.astype(BF16), preferred_element_type=F32)
        m_ref[...] = m_new

    @pl.when(ki < ratio * qi)
    def _():
        update(q0_ref, m0, l0, a0, False)
        update(q1_ref, m1, l1, a1, False)

    @pl.when(ki >= ratio * qi)
    def _():
        update(q0_ref, m0, l0, a0, True)
        update(q1_ref, m1, l1, a1, True)

    @pl.when(ki == ratio * qi + ratio - 1)
    def _():
        ot = a0[...] / l0[...] - lam_ref[0, 0] * (a1[...] / l1[...])
        yt = ot * lax.rsqrt(jnp.mean(ot * ot, axis=0, keepdims=True) + EPS) * (sub_ref[...] * out_scale)
        o_ref[...] = yt.T.astype(o_ref.dtype)


def diff_attention_prompt(q0, q1, kb, vt, lam, subln, lam_init, *, t, bq=1024, bk=512):
    assert bq % bk == 0 and t % bq == 0
    nq, ratio = t // bq, bq // bk
    qi_tbl = np.concatenate([np.full(ratio * (i + 1), i, np.int32) for i in range(nq)])
    ki_tbl = np.concatenate([np.arange(ratio * (i + 1), dtype=np.int32) for i in range(nq)])
    kern = functools.partial(_diff_flash_kernel, bq=bq, bk=bk, out_scale=1.0 - lam_init)
    grid_spec = pltpu.PrefetchScalarGridSpec(
        num_scalar_prefetch=2,
        grid=(DIFF_HEADS, len(qi_tbl)),
        in_specs=[pl.BlockSpec(memory_space=pltpu.SMEM),
                  pl.BlockSpec((bq, LANES), lambda h, s, qi, ki: (qi[s], h)),
                  pl.BlockSpec((bq, LANES), lambda h, s, qi, ki: (qi[s], h)),
                  pl.BlockSpec((bk, LANES), lambda h, s, qi, ki: (ki[s], h)),
                  pl.BlockSpec((LANES, bk), lambda h, s, qi, ki: (h, ki[s])),
                  pl.BlockSpec((DIFF_VD, 1), lambda h, s, qi, ki: (0, 0))],
        out_specs=pl.BlockSpec((bq, LANES), lambda h, s, qi, ki: (qi[s], h)),
        scratch_shapes=[pltpu.VMEM((1, bq), F32), pltpu.VMEM((1, bq), F32), pltpu.VMEM((DIFF_VD, bq), F32),
                        pltpu.VMEM((1, bq), F32), pltpu.VMEM((1, bq), F32), pltpu.VMEM((DIFF_VD, bq), F32)],
    )
    return pl.pallas_call(
        kern,
        grid_spec=grid_spec,
        out_shape=jax.ShapeDtypeStruct((t, GROUP_W), BF16),
        compiler_params=_cparams(("arbitrary", "arbitrary")),
        name="diff_attention_prompt",
    )(jnp.asarray(qi_tbl), jnp.asarray(ki_tbl), lam.reshape(1, 1), q0, q1, kb, vt, subln.reshape(DIFF_VD, 1))


def _diff_paged_kernel(pt_ref, lam_ref, q_ref, kn_ref, vn_ref, sub_ref, sel_ref, qmask_ref, *rest,
                       n_pages, t_new, out_scale):
    k_pages = rest[:n_pages]
    v_pages = rest[n_pages:2 * n_pages]
    o_ref = rest[2 * n_pages]
    kpad, vpad, qpad, s_scr = rest[2 * n_pages + 1:]
    half = DIFF_HEADS * t_new
    b = pl.program_id(0)

    @pl.when(b == 0)
    def _():
        kpad[...] = jnp.zeros(kpad.shape, kpad.dtype)
        vpad[...] = jnp.zeros(vpad.shape, vpad.dtype)
        qpad[...] = jnp.zeros(qpad.shape, qpad.dtype)

    kpad[0:t_new, :] = kn_ref[...]
    vpad[0:t_new, :] = vn_ref[...]
    qpad[0:t_new, :] = q_ref[...]
    qbd = (lax.dot_general(qpad[...].astype(BF16), sel_ref[...], (((0,), (0,)), ((), ())),
                           preferred_element_type=F32) * qmask_ref[...]).astype(BF16)

    def heads_to_lanes(ref):
        return jnp.concatenate([ref[pl.ds(h, PAGE_SIZE, stride=DIFF_HEADS), :] for h in range(DIFF_HEADS)],
                               axis=1).astype(BF16)

    key =lax.broadcasted_iota(jnp.int32, (PAGE_SIZE, LANES), 0)
    qpos = lax.broadcasted_iota(jnp.int32, (PAGE_SIZE, LANES), 1) & (t_new - 1)
    m = jnp.full((1, LANES), NEG_INF, F32)
    for p in range(n_pages + 1):
        if p < n_pages:
            s = jnp.dot(heads_to_lanes(k_pages[p]), qbd, preferred_element_type=F32)
        else:
            s = jnp.dot(kpad[...].astype(BF16), qbd, preferred_element_type=F32)
            s = jnp.where(key <= qpos, s, NEG_INF)
        s_scr[p] = s
        m = jnp.maximum(m, jnp.max(s, axis=0, keepdims=True))
    l = jnp.zeros((1, LANES), F32)
    acc = jnp.zeros((LANES, GROUP_W), F32)
    for p in range(n_pages + 1):
        e = jnp.exp2(s_scr[p] - m)
        l = l + jnp.sum(e, axis=0, keepdims=True)
        vsrc = heads_to_lanes(v_pages[p]) if p < n_pages else vpad[...].astype(BF16)
        acc += lax.dot_general(e.astype(BF16), vsrc, (((0,), (0,)), ((), ())), preferred_element_type=F32)
    inv_col = jnp.broadcast_to(1.0 / l, (LANES, LANES)).T
    lam = lam_ref[0, 0]
    outs = []
    for h in range(DIFF_HEADS):
        r0, r1 = h * t_new, half + h * t_new
        cs = slice(h * DIFF_VD, (h + 1) * DIFF_VD)
        o = acc[r0:r0 + t_new, cs] * inv_col[r0:r0 + t_new, :] - lam * (acc[r1:r1 + t_new, cs] * inv_col[r1:r1 + t_new, :])
        y = o * lax.rsqrt(jnp.mean(o * o, axis=-1, keepdims=True) + EPS)
        outs.append(y * sub_ref[...] * out_scale)
    o_ref[...] = jnp.concatenate(outs, axis=1).astype(o_ref.dtype)


def diff_attention_sample(qs, kn, proj, row0, cache_k, cache_v, layer, page_table, lam, subln, lam_init):
    nb, t_new, _ = qs.shape
    assert row0 % t_new == 0
    b0 = row0 // t_new
    n_pages = page_table.shape[1]
    rows = PAGE_SIZE * DIFF_HEADS
    assert t_new & (t_new - 1) == 0
    half = DIFF_HEADS * t_new
    col = np.arange(LANES)
    valid = col < 2 * half
    sel = ((np.arange(PAGE_SIZE)[:, None] == (col % t_new)[None, :]) & valid[None, :])
    blk_of_col = 2 * ((col % half) // t_new) + col // half
    qmask = ((np.arange(GROUP_W)[:, None] // DIFF_QD) == blk_of_col[None, :]) & valid[None, :]
    sel = jnp.asarray(sel, BF16)
    qmask = jnp.asarray(qmask, F32)

    def page_spec(p):
        return pl.BlockSpec((None, None, rows, DIFF_VD), lambda b, pt: (layer, pt[b, p], 0, 0))

    kern = functools.partial(_diff_paged_kernel, n_pages=n_pages, t_new=t_new, out_scale=1.0 - lam_init)
    grid_spec = pltpu.PrefetchScalarGridSpec(
        num_scalar_prefetch=1,
        grid=(nb,),
        in_specs=[pl.BlockSpec(memory_space=pltpu.SMEM),
                  pl.BlockSpec((t_new, GROUP_W), lambda b, pt: (b, 0)),
                  pl.BlockSpec((t_new, GROUP_W), lambda b, pt: (b0 + b, 0)),
                  pl.BlockSpec((t_new, GROUP_W), lambda b, pt: (b0 + b, 2)),
                  pl.BlockSpec((1, LANES), lambda b, pt: (0, 0)),
                  pl.BlockSpec((PAGE_SIZE, LANES), lambda b, pt: (0, 0)),
                  pl.BlockSpec((GROUP_W, LANES), lambda b, pt: (0, 0))]
                 + [page_spec(p) for p in range(n_pages)] * 2,
        out_specs=pl.BlockSpec((t_new, GROUP_W), lambda b, pt: (b, 0)),
        scratch_shapes=[pltpu.VMEM((PAGE_SIZE, GROUP_W), F32), pltpu.VMEM((PAGE_SIZE, GROUP_W), F32),
                        pltpu.VMEM((PAGE_SIZE, GROUP_W), F32),
                        pltpu.VMEM((n_pages + 1, PAGE_SIZE, LANES), F32)],
    )
    return pl.pallas_call(
        kern,
        grid_spec=grid_spec,
        out_shape=jax.ShapeDtypeStruct((nb * t_new, GROUP_W), BF16),
        compiler_params=_cparams(("arbitrary",)),
        name="diff_attention_sample",
    )(page_table, lam.reshape(1, 1), qs.reshape(nb * t_new, GROUP_W), kn, proj, subln.reshape(1, LANES), sel, qmask,
      *([cache_k] * n_pages), *([cache_v] * n_pages))


def _interleave64(x, y):
    lane = lax.broadcasted_iota(jnp.int32, (x.shape[0], LANES), 1)
    lo = lane < RWKV_HD
    blocks = []
    for c in range(GROUP_W // LANES):
        xc = x[:, c * LANES:(c + 1) * LANES]
        yc = y[:, c * LANES:(c + 1) * LANES]
        rx = pltpu.roll(xc, RWKV_HD, axis=1)
        ry = pltpu.roll(yc, RWKV_HD, axis=1)
        blocks.append(jnp.where(lo, xc, ry))
        blocks.append(jnp.where(lo, rx, yc))
    return jnp.concatenate(blocks, axis=1)


def _rwkv_prep_kernel(cols_ref, prev_ref, mu_ref, w0_ref, a0_ref, kk_ref, ka_ref, rk_ref,
                      w2h_ref, w2l_ref, a2h_ref, a2l_ref, g2h_ref, g2l_ref, e_ref,
                      ar_ref, bt_ref, kt_ref, g8_ref, v_ref, br_ref, kr_ref, bon_ref, g_ref, *, shifted):
    cols = cols_ref[...]
    if shifted:
        before = jnp.where(pl.program_id(0) == 0, 0.0, prev_ref[SUBLANES - 1:SUBLANES, :])
        first = lax.broadcasted_iota(jnp.int32, cols.shape, 0) == 0
        prev = jnp.where(first, before, pltpu.roll(cols, 1, axis=0))
    else:
        prev = prev_ref[...]
    xm = cols + (prev - cols) * mu_ref[...]
    o1 = GROUP_W
    r, k, v = xm[:, 0:o1], xm[:, o1:2 * o1], xm[:, 2 * o1:3 * o1]
    lora = xm[:, 3 * o1:3 * o1 + LANES]
    gl = xm[:, 3 * o1 + LANES:3 * o1 + 2 * LANES]
    wterm = _dot3(jnp.tanh(lora), w2h_ref[...], w2l_ref[...])
    aterm = _dot3(lora, a2h_ref[...], a2l_ref[...])
    z = -(w0_ref[...] + wterm)
    softplus = jnp.maximum(z, 0.0) + jnp.log1p(jnp.exp(-jnp.abs(z)))
    w = -softplus - 0.5
    log_decay = -jnp.exp(w)
    a = jax.nn.sigmoid(a0_ref[...] + aterm)
    g = _dot3(jax.nn.sigmoid(gl), g2h_ref[...], g2l_ref[...])
    e = e_ref[...]
    kk = k * kk_ref[...]
    kk = kk * lax.rsqrt(jnp.maximum(_dot_hilo(kk * kk, e), 1e-24))
    k2 = k * (1.0 + (a - 1.0) * ka_ref[...])
    bv = kk * a
    n = cols.shape[0]
    sub = lax.broadcasted_iota(jnp.int32, log_decay.shape, 0) & (SUBLANES - 1)
    csum = log_decay
    rsum = log_decay
    for d in (1, 2, 4):
        csum = csum + jnp.where(sub >= d, pltpu.roll(csum, d, axis=0), 0.0)
        rsum = rsum + jnp.where(sub < SUBLANES - d, pltpu.roll(rsum, n - d, axis=0), 0.0)
    gamma = jnp.exp(csum)
    inv_gamma = jnp.exp(-csum)
    rnd = lambda t: t.astype(BF16).astype(F32)
    ar_ref[...] = _interleave64(rnd(jnp.exp(csum - log_decay) * (-kk)), rnd(r * gamma))
    bt_ref[...] = rnd(bv * inv_gamma)
    kt_ref[...] = rnd(k2 * inv_gamma)
    g8_ref[...] = jnp.exp(csum + rsum - log_decay)
    v_ref[...] = v
    br_ref[...] = _dot_hilo(bv * r, e)
    kr_ref[...] = _dot_hilo(k2 * r, e)
    bon_ref[...] = _dot_hilo(r * k2 * rk_ref[...], e)
    g_ref[...] = g


def rwkv_prep(cols, prev, p, e512, *, row0, m, tm=256):
    assert row0 % tm == 0 and m % tm == 0
    b0 = row0 // tm
    shifted = prev is None
    if shifted:
        prev_arr = cols
        per8 = tm // SUBLANES
        prev_spec = pl.BlockSpec((SUBLANES, RWKV_COLS), lambda i: (jnp.maximum((b0 + i) * per8 - 1, 0), 0))
    else:
        prev_arr = prev
        prev_spec = pl.BlockSpec((tm, RWKV_COLS), lambda i: (i, 0))
    row = lambda n: pl.BlockSpec((1, n), lambda i: (0, 0))
    full = lambda a: pl.BlockSpec(a.shape, lambda i: (0, 0))
    wide = pl.BlockSpec((tm, 2 * GROUP_W), lambda i: (i, 0))
    nar = pl.BlockSpec((tm, GROUP_W), lambda i: (i, 0))
    z64 = jnp.zeros((64, GROUP_W), F32)
    w2p = jnp.concatenate([p["w2"], z64], axis=0)
    a2p = jnp.concatenate([z64, p["a2"]], axis=0)
    w2h, w2l = _split_bf16(w2p)
    a2h, a2l = _split_bf16(a2p)
    g2h, g2l = _split_bf16(p["g2"])
    mats = [w2h, w2l, a2h, a2l, g2h, g2l, e512]
    return pl.pallas_call(
        functools.partial(_rwkv_prep_kernel, shifted=shifted),
        grid=(m // tm,),
        in_specs=[pl.BlockSpec((tm, RWKV_COLS), lambda i: (b0 + i, 0)),
                  prev_spec,
                  row(RWKV_COLS), row(GROUP_W), row(GROUP_W), row(GROUP_W), row(GROUP_W), row(GROUP_W)]
                 + [full(a) for a in mats],
        out_specs=[wide, nar, nar, nar, nar, nar, nar, nar, nar],
        out_shape=[jax.ShapeDtypeStruct((m, 2 * GROUP_W), F32)]
                  + [jax.ShapeDtypeStruct((m, GROUP_W), F32)] * 8,
        compiler_params=_cparams(("parallel",)),
        name="rwkv_prep",
    )(cols, prev_arr, p["mu"].reshape(1, -1), p["w0"].reshape(1, -1), p["a0"].reshape(1, -1),
      p["kk"].reshape(1, -1), p["ka"].reshape(1, -1), p["rk"].reshape(1, -1), *mats)


def _rwkv_scan_kernel(ar_ref, bt_ref, kt_ref, g8_ref, v_ref, br_ref, kr_ref, bon_ref, g_ref, lnw_ref, lnb_ref,
                      e_ref, s0_ref, o_ref, sn_ref, s_scr, y_scr, *, n_seq, t_len, pairs):
    c = pl.program_id(2)
    half = RWKV_HD
    lane_lo = lax.broadcasted_iota(jnp.int32, (half, LANES), 1) < half
    diag = (lax.broadcasted_iota(jnp.int32, (half, LANES), 1) & (half - 1)) == \
        lax.broadcasted_iota(jnp.int32, (half, LANES), 0)
    ones_blocks = e_ref[...]

    def tokens8(g, states):
        r0 = pl.multiple_of(g * SUBLANES, SUBLANES)
        rows = pl.ds(r0, SUBLANES)
        ar8, bt8, kt8, g8 = ar_ref[rows, :], bt_ref[rows, :], kt_ref[rows, :], g8_ref[rows, :]
        v8, br8, kr8 = v_ref[rows, :], br_ref[rows, :], kr_ref[rows, :]
        states = list(states)
        ys = [[] for _ in range(pairs)]

        def by_transpose(tile, k, pp):
            top = jnp.broadcast_to(tile[k:k + 1, 2 * pp * LANES:(2 * pp + 1) * LANES], (half, LANES))
            bot = jnp.broadcast_to(tile[k:k + 1, (2 * pp + 1) * LANES:(2 * pp + 2) * LANES], (half, LANES))
            t = jnp.concatenate([top, bot], axis=0).astype(BF16).T
            return t[0:half].astype(F32), t[half:2 * half].astype(F32)

        def by_matmul(x8, y8, k, ls):
            zx = jnp.where(diag, jnp.broadcast_to(x8[k:k + 1, ls], (half, LANES)), 0.0)
            zy = jnp.where(diag, jnp.broadcast_to(y8[k:k + 1, ls], (half, LANES)), 0.0)
            t = jnp.dot(jnp.concatenate([zx, zy], axis=0).astype(BF16), ones_blocks, preferred_element_type=F32)
            return t[0:half], t[half:2 * half]

        for k in range(SUBLANES):
            for pp in range(pairs):
                s = states[pp]
                ls = slice(pp * LANES, (pp + 1) * LANES)
                ac, rc = by_transpose(ar8, k, pp)
                bc, kc = by_matmul(bt8, kt8, k, ls)
                u = jnp.sum(s * ac, axis=0, keepdims=True)
                yp = jnp.sum(s * rc, axis=0, keepdims=True)
                vrow = v8[k:k + 1, ls]
                ys[pp].append(yp + u * br8[k:k + 1, ls] + vrow * kr8[k:k + 1, ls])
                states[pp] = s + bc * u + kc * vrow
        for pp in range(pairs):
            ls = slice(pp * LANES, (pp + 1) * LANES)
            y_scr[rows, ls] = jnp.concatenate(ys[pp], axis=0)
            gt = jnp.broadcast_to(g8[0:1, ls], (LANES, LANES)).T
            states[pp] = states[pp] * jnp.where(lane_lo, gt[0:half], gt[half:2 * half])
        return tuple(states)

    @pl.when(c == 0)
    def _():
        s_scr[...] = s0_ref[...]

    groups = t_len // SUBLANES

    def seq(si, carry):
        states = tuple(s_scr[si, pp] for pp in range(pairs))
        states = lax.fori_loop(0, groups, lambda g, st: tokens8(si * groups + g, st), states)
        for pp in range(pairs):
            s_scr[si, pp] = states[pp]
        return carry

    lax.fori_loop(0, n_seq, seq, 0)

    e = e_ref[...]
    for pp in range(pairs):
        ls = slice(pp * LANES, (pp + 1) * LANES)
        y = y_scr[:, ls]
        mu = _dot_hilo(y, e) * (1.0 / RWKV_HD)
        d = y - mu
        var = _dot_hilo(d * d, e) * (1.0 / RWKV_HD)
        yn = d * lax.rsqrt(var + RWKV_LN_EPS) * lnw_ref[:, ls] + lnb_ref[:, ls]
        o_ref[:, ls] = ((yn + bon_ref[:, ls] * v_ref[:, ls]) * g_ref[:, ls]).astype(o_ref.dtype)

    @pl.when(c == pl.num_programs(2) - 1)
    def _():
        sn_ref[...] = s_scr[...]


def rwkv_scan(ar, bt, kt, g8, v, br, kr, bon, g, ln_w, ln_b, e128, s0, *, t_seq, seq_per_step, t_step, pairs=2):
    n_seq = s0.shape[0]
    chunks = t_seq // t_step
    rb = seq_per_step * t_step
    assert seq_per_step == 1 or chunks == 1
    n_pairs = RWKV_HEADS // 2
    assert n_pairs % pairs == 0
    w = pairs * LANES

    def rows(width):
        return pl.BlockSpec((rb, width), lambda p, s, c: (s * chunks + c, p))

    kern = functools.partial(_rwkv_scan_kernel, n_seq=seq_per_step, t_len=t_step, pairs=pairs)
    st = pl.BlockSpec((seq_per_step, pairs, RWKV_HD, LANES), lambda p, s, c: (s, p, 0, 0))
    return pl.pallas_call(
        kern,
        grid=(n_pairs // pairs, n_seq // seq_per_step, chunks),
        in_specs=[rows(2 * w),
                  rows(w), rows(w), rows(w), rows(w), rows(w), rows(w), rows(w), rows(w),
                  pl.BlockSpec((1, w), lambda p, s, c: (0, p)),
                  pl.BlockSpec((1, w), lambda p, s, c: (0, p)),
                  pl.BlockSpec((LANES, LANES), lambda p, s, c: (0, 0)),
                  st],
        out_specs=[rows(w), st],
        out_shape=[jax.ShapeDtypeStruct((n_seq * t_seq, GROUP_W), BF16),
                   jax.ShapeDtypeStruct(s0.shape, F32)],
        scratch_shapes=[pltpu.VMEM((seq_per_step, pairs, RWKV_HD, LANES), F32),
                        pltpu.VMEM((rb, w), F32)],
        compiler_params=_cparams(("arbitrary", "arbitrary", "arbitrary")),
        name="rwkv_scan",
    )(ar, bt, kt, g8, v, br, kr, bon, g, ln_w.reshape(1, -1), ln_b.reshape(1, -1), e128, s0)


def _s5_kernel(u_ref, bb_ref, cc_ref, pw_ref, ad_ref, d_ref, wg_ref, bg_ref, nw_ref, sr0_ref, si0_ref,
               o_ref, srn_ref, sin_ref, bur, bui, cr, ci, wgb, *, tm, per_group_state):
    i = pl.program_id(0)
    nch = S5_GROUPS * S5_N
    ngrp = tm // SUBLANES

    @pl.when(i == 0)
    def _():
        wgb[...] = wg_ref[...].astype(BF16)
        if not per_group_state:
            cr[...] = sr0_ref[...]
            ci[...] = si0_ref[...]

    u = u_ref[...]
    bu = jnp.dot(u.astype(BF16), bb_ref[...], preferred_element_type=F32)
    bur[...] = bu[:, 0:nch]
    bui[...] = bu[:, nch:2 * nch]
    sub = lax.broadcasted_iota(jnp.int32, (SUBLANES, nch), 0)

    def group(r0, c_r, c_i):
        xr = bur[pl.ds(r0, SUBLANES), :]
        xi = bui[pl.ds(r0, SUBLANES), :]
        for di, dsh in enumerate((1, 2, 4)):
            ar = ad_ref[2 * di:2 * di + 1, :]
            ai = ad_ref[2 * di + 1:2 * di + 2, :]
            keep = sub >= dsh
            sr = jnp.where(keep, pltpu.roll(xr, dsh, axis=0), 0.0)
            si = jnp.where(keep, pltpu.roll(xi, dsh, axis=0), 0.0)
            xr, xi = xr + ar * sr - ai * si, xi + ar * si + ai * sr
        pr = pw_ref[0:SUBLANES, :]
        pi = pw_ref[SUBLANES:2 * SUBLANES, :]
        xr, xi = xr + pr * c_r - pi * c_i, xi + pr * c_i + pi * c_r
        bur[pl.ds(r0, SUBLANES), :] = xr
        bui[pl.ds(r0, SUBLANES), :] = xi
        return xr[SUBLANES - 1:SUBLANES, :], xi[SUBLANES - 1:SUBLANES, :]

    def block(bi, carry):
        g0 = pl.multiple_of(bi * SUBLANES, SUBLANES)
        if per_group_state:
            st_r = sr0_ref[pl.ds(g0, SUBLANES), :]
            st_i = si0_ref[pl.ds(g0, SUBLANES), :]
            lasts_r, lasts_i = [], []
        else:
            c_r, c_i = cr[...], ci[...]
        for k in range(SUBLANES):
            r0 = pl.multiple_of((g0 + k) * SUBLANES, SUBLANES)
            if per_group_state:
                l_r, l_i = group(r0, st_r[k:k + 1, :], st_i[k:k + 1, :])
                lasts_r.append(l_r)
                lasts_i.append(l_i)
            else:
                c_r, c_i = group(r0, c_r, c_i)
        if per_group_state:
            srn_ref[pl.ds(g0, SUBLANES), :] = jnp.concatenate(lasts_r, axis=0)
            sin_ref[pl.ds(g0, SUBLANES), :] = jnp.concatenate(lasts_i, axis=0)
        else:
            cr[...] = c_r
            ci[...] = c_i
        return carry

    lax.fori_loop(0, ngrp // SUBLANES, block, 0)

    if not per_group_state:
        srn_ref[...] = cr[...]
        sin_ref[...] = ci[...]

    ccv = cc_ref[...]
    y = (jnp.dot(bur[...].astype(BF16), ccv[0:nch], preferred_element_type=F32)
         + jnp.dot(bui[...].astype(BF16), ccv[nch:2 * nch], preferred_element_type=F32))
    y = y + d_ref[...] * u
    y = 0.5 * y * (1.0 + jnp.tanh(math.sqrt(2.0 / math.pi) * (y + 0.044715 * (y * y * y))))
    gate = jnp.dot(y.astype(BF16), wgb[...], preferred_element_type=F32) + bg_ref[...]
    y = y * jax.nn.sigmoid(gate)
    y = y * lax.rsqrt(jnp.mean(y * y, axis=-1, keepdims=True) + EPS)
    o_ref[...] = (y * nw_ref[...]).astype(o_ref.dtype)


def s5_mixer(u, p, sr0, si0, *, per_group_state, row0, m, tm=256):
    assert row0 % tm == 0 and m % tm == 0
    b0 = row0 // tm
    nch = S5_GROUPS * S5_N
    lr, li = p["lam_re"], p["lam_im"]
    dt = jnp.exp(p["log_step"])[:, None]
    mag = jnp.exp(lr * dt)
    ab_re, ab_im = mag * jnp.cos(li * dt), mag * jnp.sin(li * dt)
    den = lr * lr + li * li
    cf_re = ((ab_re - 1.0) * lr + ab_im * li) / den
    cf_im = (ab_im * lr - (ab_re - 1.0) * li) / den
    bb_re = cf_re[..., None] * p["b_re"] - cf_im[..., None] * p["b_im"]
    bb_im = cf_re[..., None] * p["b_im"] + cf_im[..., None] * p["b_re"]
    eye = jnp.eye(S5_GROUPS, dtype=F32)

    def bd_in(b):
        return jnp.einsum("gnc,gh->gchn", b, eye).reshape(GROUP_W, nch)

    def bd_out(cm):
        return jnp.einsum("gcn,gh->gnhc", cm, eye).reshape(nch, GROUP_W)

    bb = jnp.concatenate([bd_in(bb_re), bd_in(bb_im)], axis=1).astype(BF16)
    cc = jnp.concatenate([bd_out(p["c_re"]), -bd_out(p["c_im"])], axis=0).astype(BF16)
    ar, ai = ab_re.reshape(1, nch), ab_im.reshape(1, nch)
    pows = [(ar, ai)]
    for _ in range(SUBLANES - 1):
        pr, pi = pows[-1]
        pows.append((pr * ar - pi * ai, pr * ai + pi * ar))
    pw = jnp.concatenate([jnp.concatenate([q[0] for q in pows], axis=0),
                          jnp.concatenate([q[1] for q in pows], axis=0)], axis=0)
    ad = jnp.concatenate([pows[0][0], pows[0][1], pows[1][0], pows[1][1], pows[3][0], pows[3][1],
                          jnp.zeros((2, nch), F32)], axis=0)
    n_state = sr0.shape[0]
    full = lambda a: pl.BlockSpec(a.shape, lambda i: (0,) * a.ndim)
    row = lambda n: pl.BlockSpec((1, n), lambda i: (0, 0))
    if per_group_state:
        st = pl.BlockSpec((tm // SUBLANES, nch), lambda i: (i, 0))
    else:
        st = pl.BlockSpec((1, nch), lambda i: (0, 0))
    kern = functools.partial(_s5_kernel, tm=tm, per_group_state=per_group_state)
    return pl.pallas_call(
        kern,
        grid=(m // tm,),
        in_specs=[pl.BlockSpec((tm, GROUP_W), lambda i: (b0 + i, 0)), full(bb), full(cc), full(pw), full(ad),
                  row(GROUP_W), full(p["w_glu"]), row(GROUP_W), row(GROUP_W), st, st],
        out_specs=[pl.BlockSpec((tm, GROUP_W), lambda i: (i, 0)), st, st],
        out_shape=[jax.ShapeDtypeStruct((m, GROUP_W), BF16),
                   jax.ShapeDtypeStruct((n_state, nch), F32), jax.ShapeDtypeStruct((n_state, nch), F32)],
        scratch_shapes=[pltpu.VMEM((tm, nch), F32), pltpu.VMEM((tm, nch), F32),
                        pltpu.VMEM((1, nch), F32), pltpu.VMEM((1, nch), F32),
                        pltpu.VMEM((GROUP_W, GROUP_W), BF16)],
        compiler_params=_cparams(("arbitrary",)),
        name="s5_mixer",
    )(u, bb, cc, pw, ad, p["d"].reshape(1, -1), p["w_glu"], p["b_glu"].reshape(1, -1),
      p["norm"].reshape(1, -1), sr0, si0)


def _ffn_up_kernel(h_ref, wg_ref, wv_ref, cwg_ref, cwv_ref, cbg_ref, cbv_ref, c0ga_ref, c0gb_ref, c0va_ref,
                   c0vb_ref, act_ref, cnga_ref, cngb_ref, cnva_ref, cnvb_ref, wgb, wvb, hg, hv, *, shift, tm, off,
                   t_cols):
    mstep = pl.program_id(1)
    hist = 2 * shift

    @pl.when(mstep == 0)
    def _():
        wgb[...] = wg_ref[...].astype(BF16)
        wvb[...] = wv_ref[...].astype(BF16)
        hg[off - hist:off - shift, :] = c0ga_ref[...]
        hg[off - shift:off, :] = c0gb_ref[...]
        hv[off - hist:off - shift, :] = c0va_ref[...]
        hv[off - shift:off, :] = c0vb_ref[...]

    if t_cols:
        k = h_ref.shape[1] // t_cols
        hb = jnp.concatenate([h_ref[:, t * k:(t + 1) * k] for t in range(t_cols)], axis=0)
    else:
        hb = h_ref[...]
    hg[off:off + tm, :] = jnp.dot(hb, wgb[...], preferred_element_type=F32)
    hv[off:off + tm, :] = jnp.dot(hb, wvb[...], preferred_element_type=F32)

    def conv(hs, cw_ref, cb_ref):
        return (cb_ref[...] + cw_ref[0:1, :] * hs[off - hist:off - hist + tm, :]
                + cw_ref[1:2, :] * hs[off - shift:off - shift + tm, :]
                + cw_ref[2:3, :] * hs[off:off + tm, :])

    gate = conv(hg, cwg_ref, cbg_ref)
    val = conv(hv, cwv_ref, cbv_ref)
    act = (gate * jax.nn.sigmoid(gate) * val).astype(act_ref.dtype)
    if t_cols:
        tn = act.shape[1]
        for t in range(t_cols):
            act_ref[:, t * tn:(t + 1) * tn] = act[t * shift:(t + 1) * shift]
    else:
        act_ref[...] = act
    tail_g = hg[off + tm - hist:off + tm, :]
    tail_v = hv[off + tm - hist:off + tm, :]
    hg[off - hist:off, :] = tail_g
    hv[off - hist:off, :] = tail_v

    @pl.when(mstep == pl.num_programs(1) - 1)
    def _():
        cnga_ref[...] = tail_g[0:shift]
        cngb_ref[...] = tail_g[shift:hist]
        cnva_ref[...] = tail_v[0:shift]
        cnvb_ref[...] = tail_v[shift:hist]


def ffn_up(h, w_up, conv_w, conv_b, c0, *, time_major, row0=0, m=None, tm=None, tn=512, layer=None,
           out_rows=None):
    n_seq = c0.shape[0]
    nj = D_FF // tn
    if time_major:
        _, t_cols, k = h.shape
        shift, steps, tm = n_seq, 1, n_seq * t_cols
        h = h.reshape(n_seq, t_cols * k)
        h_spec = pl.BlockSpec((n_seq, t_cols * k), lambda j, i: (0, 0))
        act_spec = pl.BlockSpec((n_seq, t_cols * tn), lambda j, i: (0, j))
        act_shape = jax.ShapeDtypeStruct((n_seq, nj * t_cols * tn), BF16)
    else:
        k = h.shape[1]
        assert n_seq == 1 and row0 % tm == 0 and m % tm == 0
        shift, steps, b0, t_cols = 1, m // tm, row0 // tm, 0
        h_spec = pl.BlockSpec((tm, k), lambda j, i: (b0 + i, 0))
        act_spec = pl.BlockSpec((tm, tn), lambda j, i: (i, j))
        act_shape = jax.ShapeDtypeStruct((m if out_rows is None else out_rows, D_FF), BF16)
    hist = 2 * shift
    off = max(SUBLANES, hist)
    kern = functools.partial(_ffn_up_kernel, shift=shift, tm=tm, off=off, t_cols=t_cols)
    cw = jnp.concatenate([conv_w, jnp.zeros((SUBLANES - conv_w.shape[0], conv_w.shape[1]), F32)], axis=0)
    cb = conv_b.reshape(1, -1)

    c0 = c0.reshape(n_seq, 4 * D_FF)

    def c0_spec(tap, half):
        return pl.BlockSpec((n_seq, tn), lambda j, i: (0, (2 * tap + half) * nj + j))

    tap_out = pl.BlockSpec((shift, tn), lambda j, i: (0, j))
    tap_shape = jax.ShapeDtypeStruct((shift, D_FF), F32)
    outs = pl.pallas_call(
        kern,
        grid=(nj, steps),
        in_specs=[h_spec,
                  _weight_spec(w_up, layer, k, tn),
                  _weight_spec(w_up, layer, k, tn, nj),
                  pl.BlockSpec((SUBLANES, tn), lambda j, i: (0, j)),
                  pl.BlockSpec((SUBLANES, tn), lambda j, i: (0, j + nj)),
                  pl.BlockSpec((1, tn), lambda j, i: (0, j)),
                  pl.BlockSpec((1, tn), lambda j, i: (0, j + nj)),
                  c0_spec(0, 0), c0_spec(1, 0), c0_spec(0, 1), c0_spec(1, 1)],
        out_specs=[act_spec, tap_out, tap_out, tap_out, tap_out],
        out_shape=[act_shape, tap_shape, tap_shape, tap_shape, tap_shape],
        scratch_shapes=[pltpu.VMEM((k, tn), BF16), pltpu.VMEM((k, tn), BF16),
                        pltpu.VMEM((off + tm, tn), F32), pltpu.VMEM((off + tm, tn), F32)],
        compiler_params=_cparams(("parallel", "arbitrary")),
        name="ffn_up",
    )(h, w_up, w_up, cw, cw, cb, cb, c0, c0, c0, c0)
    act = outs[0]
    if time_major:
        act = jnp.transpose(act.reshape(n_seq, nj, t_cols, tn), (0, 2, 1, 3)).reshape(n_seq * t_cols, D_FF)
    return act, outs[1:]


def conv_state_from_taps(taps):
    ga, gb, va, vb = taps
    return jnp.stack([jnp.concatenate([ga, va], axis=1), jnp.concatenate([gb, vb], axis=1)], axis=1)


def _ret_rope_tables(pos):
    half = RET_HD // 2
    inv = jnp.power(RET_ROPE_BASE, -jnp.arange(half, dtype=F32) / half)
    ang = pos.astype(F32)[:, None] * inv[None, :]
    cos, sin = jnp.cos(ang), jnp.sin(ang)
    return jnp.concatenate([cos, cos], axis=1), jnp.concatenate([-sin, sin], axis=1)


def _diff_rope_tables(pos):
    half = ROPE_DIM // 2
    inv = jnp.power(ROPE_THETA, -jnp.arange(half, dtype=F32) / half)
    ang = pos.astype(F32)[:, None] * inv[None, :]
    cos, sin = jnp.cos(ang), jnp.sin(ang)
    n = pos.shape[0]
    rest = DIFF_QD - ROPE_DIM
    c = jnp.concatenate([cos, cos, jnp.ones((n, rest), F32)], axis=1)
    s1 = jnp.concatenate([-sin, jnp.zeros((n, half + rest), F32)], axis=1)
    s2 = jnp.concatenate([jnp.zeros((n, half), F32), sin, jnp.zeros((n, rest), F32)], axis=1)
    rep = LANES // DIFF_QD
    return jnp.tile(c, (1, rep)), jnp.tile(s1, (1, rep)), jnp.tile(s2, (1, rep))


def _rwkv_state_to_pairs(s):
    b = s.shape[0]
    s = s.reshape(b, RWKV_HEADS // 2, 2, RWKV_HD, RWKV_HD)
    return jnp.transpose(s, (0, 1, 4, 2, 3)).reshape(b, RWKV_HEADS // 2, RWKV_HD, LANES)


def _rwkv_state_from_pairs(s):
    b = s.shape[0]
    s = s.reshape(b, RWKV_HEADS // 2, RWKV_HD, 2, RWKV_HD)
    return jnp.transpose(s, (0, 1, 3, 4, 2)).reshape(b, RWKV_HEADS, RWKV_HD, RWKV_HD)


def kernel(x_prompt, x_sample, p_prompt, p_sample, cache_k, cache_v, page_table, state_ret, state_rwkv, state_rwkv_shift, state_s5_re, state_s5_im, state_ffn_conv, norm_mix, w_in, w_out, ret_norm_w, ret_norm_b, diff_lq1, diff_lk1, diff_lq2, diff_lk2, diff_subln, rwkv_mu, rwkv_w0, rwkv_w2, rwkv_a0, rwkv_a2, rwkv_g2, rwkv_kk, rwkv_ka, rwkv_rk, rwkv_ln_w, rwkv_ln_b, s5_lam_re, s5_lam_im, s5_log_step, s5_b_re, s5_b_im, s5_c_re, s5_c_im, s5_d, s5_w_glu, s5_b_glu, s5_norm, norm_ffn, ffn_w_up, ffn_conv_w, ffn_conv_b, ffn_w_down, norm_ple, ple_w_proj, ple_norm_e, ple_w_gate, norm_final):
    depth = w_in.shape[0]
    bp, tp, d = x_prompt.shape
    nb, ts, _ = x_sample.shape
    assert bp == 1
    mp, ms = bp * tp, nb * ts
    past_len = page_table.shape[1] * PAGE_SIZE
    nch = S5_GROUPS * S5_N

    x = jnp.concatenate([x_prompt.reshape(mp, d), x_sample.reshape(ms, d)], axis=0)
    p_all = jnp.concatenate([p_prompt.reshape(depth, mp, -1), p_sample.reshape(depth, ms, -1)], axis=1)
    pos_p = jnp.arange(tp, dtype=jnp.int32)
    pos_s = past_len + jnp.arange(ts, dtype=jnp.int32)
    ret_cos_p, ret_sin_p = _ret_rope_tables(pos_p)
    ret_cos_s, ret_sin_s = _ret_rope_tables(pos_s)
    pos_all = jnp.concatenate([pos_p, jnp.tile(pos_s, nb)])
    dc, ds1, ds2 = _diff_rope_tables(pos_all)
    cache_k2 = cache_k.reshape(cache_k.shape[0], cache_k.shape[1], PAGE_SIZE * DIFF_HEADS, DIFF_VD)
    cache_v2 = cache_v.reshape(cache_v.shape[0], cache_v.shape[1], PAGE_SIZE * DIFF_HEADS, DIFF_VD)
    head_of = jnp.arange(GROUP_W) // RWKV_HD
    e512 = (head_of[:, None] == head_of[None, :]).astype(BF16)
    e128 = e512[:LANES, :LANES]

    outs = {k: [] for k in ("ret_p", "ret_s", "rwkv_p", "rwkv_s", "sh_p", "sh_s",
                            "s5r_p", "s5i_p", "s5r_s", "s5i_s", "conv_p", "conv_s")}
    kv_out = None
    for i in range(depth):
        h = rmsnorm_rows(x, norm_mix[i], BF16)
        c_ret = matmul(h, w_in, layer=i, tn=1024, tm=1024, col0=0, ncols=RET_COLS, name="proj_ret")
        c_diff = matmul(h, w_in, layer=i, tn=512, tm=1024, col0=RET_COLS, ncols=DIFF_COLS, name="proj_diff")
        c_rwkv = matmul(h, w_in, layer=i, tn=896, tm=1024, col0=RET_COLS + DIFF_COLS, ncols=RWKV_COLS,
                        name="proj_rwkv")
        c_s5 = matmul(h, w_in, layer=i, tn=256, tm=1024, col0=RET_COLS + DIFF_COLS + RWKV_COLS, ncols=S5_COLS,
                      name="proj_s5")

        o_ret_p, s_ret_p = retention(c_ret, 0, bp, tp, ret_cos_p, ret_sin_p,
                                     jnp.zeros((bp, RET_HEADS, RET_HD, RET_HD), F32),
                                     ret_norm_w[i], ret_norm_b[i], chunk=RET_CHUNK, chunks_per_step=4)
        o_ret_s, s_ret_s = retention(c_ret, mp, nb, ts, ret_cos_s, ret_sin_s, state_ret,
                                     ret_norm_w[i], ret_norm_b[i], chunk=ts, chunks_per_step=1, layer=i)

        lam_init = 0.8 - 0.6 * math.exp(-0.3 * i)
        lam = (jnp.exp(jnp.sum(diff_lq1[i] * diff_lk1[i])) - jnp.exp(jnp.sum(diff_lq2[i] * diff_lk2[i])) + lam_init)
        (q0, q1, k_new, kb, vt), kv_out = diff_prep(c_diff, dc, ds1, ds2, layer=i, depth=depth, mp=mp,
                                                    kv_prev=kv_out)
        o_diff_p = diff_attention_prompt(q0, q1, kb, vt, lam, diff_subln[i], lam_init, t=mp)
        qs = (q0[mp:].astype(F32) + q1[mp:].astype(F32)).reshape(nb, ts, GROUP_W)
        o_diff_s = diff_attention_sample(qs, k_new, c_diff, mp, cache_k2, cache_v2, i, page_table,
                                         lam, diff_subln[i], lam_init)

        cr_s = c_rwkv[mp:].reshape(nb, ts, RWKV_COLS)
        prev_s = jnp.concatenate([state_rwkv_shift[i][:, None], cr_s[:, :-1]], axis=1).reshape(ms, RWKV_COLS)
        rp = dict(mu=rwkv_mu[i], w0=rwkv_w0[i], w2=rwkv_w2[i], a0=rwkv_a0[i], a2=rwkv_a2[i], g2=rwkv_g2[i],
                  kk=rwkv_kk[i], ka=rwkv_ka[i], rk=rwkv_rk[i].reshape(-1))
        lnw, lnb = rwkv_ln_w[i].reshape(-1), rwkv_ln_b[i].reshape(-1)
        o_rwkv_p, s_rwkv_p = rwkv_scan(*rwkv_prep(c_rwkv, None, rp, e512, row0=0, m=mp), lnw, lnb, e128,
                                       jnp.zeros((bp, RWKV_HEADS // 2, RWKV_HD, LANES), F32),
                                       t_seq=tp, seq_per_step=1, t_step=256, pairs=4)
        o_rwkv_s, s_rwkv_s = rwkv_scan(*rwkv_prep(c_rwkv, prev_s, rp, e512, row0=mp, m=ms), lnw, lnb, e128,
                                       _rwkv_state_to_pairs(state_rwkv[i]),
                                       t_seq=ts, seq_per_step=16, t_step=ts, pairs=4)

        sp = dict(lam_re=s5_lam_re[i], lam_im=s5_lam_im[i], log_step=s5_log_step[i], b_re=s5_b_re[i], b_im=s5_b_im[i],
                  c_re=s5_c_re[i], c_im=s5_c_im[i], d=s5_d[i], w_glu=s5_w_glu[i], b_glu=s5_b_glu[i], norm=s5_norm[i])
        o_s5_p, s5r_p, s5i_p = s5_mixer(c_s5, sp, jnp.zeros((1, nch), F32), jnp.zeros((1, nch), F32),
                                        per_group_state=False, row0=0, m=mp)
        o_s5_s, s5r_s, s5i_s = s5_mixer(c_s5, sp, state_s5_re[i].reshape(nb, nch), state_s5_im[i].reshape(nb, nch),
                                        per_group_state=True, row0=mp, m=ms)

        x = matmul_split([o_ret_p, o_diff_p, o_rwkv_p, o_s5_p], [o_ret_s, o_diff_s, o_rwkv_s, o_s5_s], w_out,
                         layer=i, tn=512, tm=1024, tiles=(x,), epilogue=lambda acc, res: res + acc, name="w_out")

        h2 = rmsnorm_rows(x, norm_ffn[i], BF16)
        act_p, taps_p = ffn_up(h2, ffn_w_up, ffn_conv_w[i], ffn_conv_b[i], jnp.zeros((bp, 2, 2 * D_FF), F32),
                               layer=i, time_major=False, row0=0, m=mp, tm=1024, out_rows=mp + ms)
        act_s, taps_s = ffn_up(h2[mp:].reshape(nb, ts, d), ffn_w_up, ffn_conv_w[i], ffn_conv_b[i],
                               state_ffn_conv[i], layer=i, time_major=True)
        act = lax.dynamic_update_slice(act_p, act_s, (mp, 0))
        x = matmul(act, ffn_w_down, layer=i, tn=512, tm=512, tiles=(x,),
                   epilogue=lambda acc, res: res + acc, name="ffn_down")

        e = matmul(p_all[i], ple_w_proj, layer=i, tn=d, tm=256, rows=(ple_norm_e[i],),
                   epilogue=lambda acc, g: acc * lax.rsqrt(jnp.mean(acc * acc, axis=-1, keepdims=True) + EPS) * g,
                   name="ple_proj")
        h3 = rmsnorm_rows(x, norm_ple[i], BF16)
        x = matmul(h3, ple_w_gate, layer=i, tn=512, tm=1024, tiles=(x, e),
                   epilogue=lambda acc, res, ee: res + ee * jax.nn.sigmoid(acc), name="ple_gate")

        outs["ret_p"].append(s_ret_p)
        outs["ret_s"].append(s_ret_s)
        outs["rwkv_p"].append(_rwkv_state_from_pairs(s_rwkv_p))
        outs["rwkv_s"].append(_rwkv_state_from_pairs(s_rwkv_s))
        outs["sh_p"].append(c_rwkv[mp - 1:mp].reshape(bp, RWKV_COLS))
        outs["sh_s"].append(cr_s[:, -1])
        outs["s5r_p"].append(s5r_p.reshape(bp, S5_GROUPS, S5_N))
        outs["s5i_p"].append(s5i_p.reshape(bp, S5_GROUPS, S5_N))
        outs["s5r_s"].append(s5r_s.reshape(nb, S5_GROUPS, S5_N))
        outs["s5i_s"].append(s5i_s.reshape(nb, S5_GROUPS, S5_N))
        outs["conv_p"].append(conv_state_from_taps(taps_p))
        outs["conv_s"].append(conv_state_from_taps(taps_s))

    y_p, y_s = rmsnorm_rows_split(x, norm_final, mp)
    st = lambda k: jnp.stack(outs[k])
    k_p, k_s, v_p, v_s = kv_out
    return (y_p.reshape(bp, tp, d), y_s.reshape(nb, ts, d),
            k_p.reshape(depth, bp, tp, DIFF_HEADS, DIFF_VD), v_p.reshape(depth, bp, tp, DIFF_HEADS, DIFF_VD),
            k_s.reshape(depth, nb, ts, DIFF_HEADS, DIFF_VD), v_s.reshape(depth, nb, ts, DIFF_HEADS, DIFF_VD),
            st("ret_p"), st("ret_s"), st("rwkv_p"), st("rwkv_s"),
            st("sh_p"), st("sh_s"), st("s5r_p"), st("s5i_p"), st("s5r_s"), st("s5i_s"), st("conv_p"), st("conv_s"))
```

```python
import functools
import math

import jax
import jax.numpy as jnp
import numpy as np
from jax import lax
from jax.experimental import pallas as pl
from jax.experimental.pallas import tpu as pltpu

F32 = jnp.float32
BF16 = jnp.bfloat16

D_MODEL = 2048
GROUP_W = 512
RET_HEADS = 4
RET_HD = 128
RET_CHUNK = 128
RET_ROPE_BASE = 10000.0
DIFF_HEADS = 4
DIFF_VD = 128
DIFF_QD = 64
ROPE_THETA = 500000.0
ROPE_DIM = 16
PAGE_SIZE = 128
RWKV_HD = 64
RWKV_HEADS = 8
RWKV_LN_EPS = 64e-5
S5_CH = 16
S5_GROUPS = 32
S5_N = 64
D_FF = 5632
EPS = 1e-6
NEG_INF = -1e30
LOG2_E = math.log2(math.e)

RET_COLS = 4 * GROUP_W
DIFF_COLS = 3 * GROUP_W
RWKV_COLS = 3 * GROUP_W + 64 + 64 + 128
S5_COLS = GROUP_W

LANES = 128
SUBLANES = 8
VMEM_LIMIT_BYTES = 52 * 1024 * 1024


def _cparams(sem, vmem=VMEM_LIMIT_BYTES):
    return pltpu.CompilerParams(dimension_semantics=sem, vmem_limit_bytes=vmem)


def _split_bf16(x):
    hi = x.astype(BF16)
    lo = (x - hi.astype(F32)).astype(BF16)
    return hi, lo


def _dot_hilo(x, w_bf16):
    hi, lo = _split_bf16(x)
    return (jnp.dot(hi, w_bf16, preferred_element_type=F32)
            + jnp.dot(lo, w_bf16, preferred_element_type=F32))


def _dot3(x, w_hi, w_lo):
    hi, lo = _split_bf16(x)
    return (jnp.dot(hi, w_hi, preferred_element_type=F32)
            + jnp.dot(hi, w_lo, preferred_element_type=F32)
            + jnp.dot(lo, w_hi, preferred_element_type=F32))


def _rmsnorm_kernel(x_ref, g_ref, o_ref):
    x = x_ref[...]
    y = x * lax.rsqrt(jnp.mean(x * x, axis=-1, keepdims=True) + EPS)
    o_ref[...] = (y * g_ref[...]).astype(o_ref.dtype)


def rmsnorm_rows(x, g, out_dtype, tm=512):
    m, d = x.shape
    return pl.pallas_call(
        _rmsnorm_kernel,
        grid=(m // tm,),
        in_specs=[pl.BlockSpec((tm, d), lambda i: (i, 0)),
                  pl.BlockSpec((1, d), lambda i: (0, 0))],
        out_specs=pl.BlockSpec((tm, d), lambda i: (i, 0)),
        out_shape=jax.ShapeDtypeStruct((m, d), out_dtype),
        compiler_params=_cparams(("parallel",)),
        name="rmsnorm_rows",
    )(x, g.reshape(1, d))


def _rmsnorm_split_kernel(x_ref, g_ref, op_ref, os_ref, *, n_prompt_tiles):
    x = x_ref[...]
    y = (x * lax.rsqrt(jnp.mean(x * x, axis=-1, keepdims=True) + EPS)) * g_ref[...]

    @pl.when(pl.program_id(0) < n_prompt_tiles)
    def _():
        op_ref[...] = y

    @pl.when(pl.program_id(0) >= n_prompt_tiles)
    def _():
        os_ref[...] = y


def rmsnorm_rows_split(x, g, mp, tm=512):
    m, d = x.shape
    ms = m - mp
    assert mp % tm == 0 and ms % tm == 0
    npt = mp // tm
    return pl.pallas_call(
        functools.partial(_rmsnorm_split_kernel, n_prompt_tiles=npt),
        grid=(m // tm,),
        in_specs=[pl.BlockSpec((tm, d), lambda i: (i, 0)),
                  pl.BlockSpec((1, d), lambda i: (0, 0))],
        out_specs=[pl.BlockSpec((tm, d), lambda i: (jnp.minimum(i, npt - 1), 0)),
                   pl.BlockSpec((tm, d), lambda i: (jnp.maximum(i - npt, 0), 0))],
        out_shape=[jax.ShapeDtypeStruct((mp, d), F32), jax.ShapeDtypeStruct((ms, d), F32)],
        compiler_params=_cparams(("arbitrary",)),
        name="rmsnorm_final",
    )(x, g.reshape(1, d))


def _mm_kernel(x_ref, w_ref, *rest, epilogue, n_extra):
    extra = rest[:n_extra]
    o_ref, wb_ref = rest[n_extra], rest[n_extra + 1]

    @pl.when(pl.program_id(1) == 0)
    def _():
        wb_ref[...] = w_ref[...].astype(BF16)

    acc = jnp.dot(x_ref[...].astype(BF16), wb_ref[...], preferred_element_type=F32)
    if epilogue is not None:
        acc = epilogue(acc, *[e[...] for e in extra])
    o_ref[...] = acc.astype(o_ref.dtype)


def _weight_spec(w, layer, k, tn, joff=0):
    if w.ndim == 3:
        return pl.BlockSpec((None, k, tn), lambda j, i: (layer, 0, j + joff))
    return pl.BlockSpec((k, tn), lambda j, i: (0, j + joff))


def matmul(x, w, *, tn, tm=512, col0=0, ncols=None, epilogue=None, tiles=(), rows=(),
           out_dtype=F32, layer=None, name="matmul"):
    m, k = x.shape
    n = w.shape[-1] - col0 if ncols is None else ncols
    assert col0 % tn == 0 and n % tn == 0 and m % tm == 0
    joff = col0 // tn
    in_specs = [pl.BlockSpec((tm, k), lambda j, i: (i, 0)), _weight_spec(w, layer, k, tn, joff)]
    in_specs += [pl.BlockSpec((tm, tn), lambda j, i: (i, j)) for _ in tiles]
    in_specs += [pl.BlockSpec((1, tn), lambda j, i: (0, j)) for _ in rows]
    kern = functools.partial(_mm_kernel, epilogue=epilogue, n_extra=len(tiles) + len(rows))
    return pl.pallas_call(
        kern,
        grid=(n // tn, m // tm),
        in_specs=in_specs,
        out_specs=pl.BlockSpec((tm, tn), lambda j, i: (i, j)),
        out_shape=jax.ShapeDtypeStruct((m, n), out_dtype),
        scratch_shapes=[pltpu.VMEM((k, tn), BF16)],
        compiler_params=_cparams(("parallel", "arbitrary")),
        name=name,
    )(x, w, *tiles, *[r.reshape(1, -1) for r in rows])


def _mm_split_kernel(*refs, n_parts, part_k, n_prompt_tiles, epilogue, n_extra):
    xp = refs[:n_parts]
    xs = refs[n_parts:2 * n_parts]
    w_ref = refs[2 * n_parts]
    extra = refs[2 * n_parts + 1:2 * n_parts + 1 + n_extra]
    o_ref, wb_ref = refs[2 * n_parts + 1 + n_extra], refs[2 * n_parts + 2 + n_extra]
    i = pl.program_id(1)

    @pl.when(i == 0)
    def _():
        wb_ref[...] = w_ref[...].astype(BF16)

    def body(parts):
        acc = None
        for g, r in enumerate(parts):
            d = jnp.dot(r[...], wb_ref[g * part_k:(g + 1) * part_k, :], preferred_element_type=F32)
            acc = d if acc is None else acc + d
        o_ref[...] = epilogue(acc, *[e[...] for e in extra]).astype(o_ref.dtype)

    @pl.when(i < n_prompt_tiles)
    def _():
        body(xp)

    @pl.when(i >= n_prompt_tiles)
    def _():
        body(xs)


def matmul_split(xs_prompt, xs_sample, w, *, tn, tm, epilogue, tiles=(), out_dtype=F32, layer=None,
                 name="matmul_split"):
    n_parts = len(xs_prompt)
    mp, part_k = xs_prompt[0].shape
    ms = xs_sample[0].shape[0]
    k, n = w.shape[-2:]
    assert k == n_parts * part_k and mp % tm == 0 and ms % tm == 0 and n % tn == 0
    npt, nst = mp // tm, ms // tm
    in_specs = [pl.BlockSpec((tm, part_k), lambda j, i: (jnp.minimum(i, npt - 1), 0)) for _ in xs_prompt]
    in_specs += [pl.BlockSpec((tm, part_k), lambda j, i: (jnp.maximum(i - npt, 0), 0)) for _ in xs_sample]
    in_specs += [_weight_spec(w, layer, k, tn)]
    in_specs += [pl.BlockSpec((tm, tn), lambda j, i: (i, j)) for _ in tiles]
    kern = functools.partial(_mm_split_kernel, n_parts=n_parts, part_k=part_k, n_prompt_tiles=npt,
                             epilogue=epilogue, n_extra=len(tiles))
    return pl.pallas_call(
        kern,
        grid=(n // tn, npt + nst),
        in_specs=in_specs,
        out_specs=pl.BlockSpec((tm, tn), lambda j, i: (i, j)),
        out_shape=jax.ShapeDtypeStruct((mp + ms, n), out_dtype),
        scratch_shapes=[pltpu.VMEM((k, tn), BF16)],
        compiler_params=_cparams(("parallel", "arbitrary")),
        name=name,
    )(*xs_prompt, *xs_sample, w, *tiles)


def _ret_kernel(q_ref, k_ref, v_ref, g_ref, cos_ref, sin_ref, dmask_ref, qdec_ref, kdec_ref,
                cdec_ref, nw_ref, nb_ref, s0_ref, o_ref, sn_ref, s_scr, *, chunk, n_chunks):
    c = pl.program_id(1)

    @pl.when(c == 0)
    def _():
        s_scr[...] = s0_ref[...]

    cos = cos_ref[...]
    sin = sin_ref[...]
    for h in range(RET_HEADS):
        hs = slice(h * RET_HD, (h + 1) * RET_HD)
        qh = q_ref[:, hs]
        kh = k_ref[:, hs]
        qr = qh * cos + pltpu.roll(qh, RET_HD // 2, axis=1) * sin
        kr = (kh * cos + pltpu.roll(kh, RET_HD // 2, axis=1) * sin) * (RET_HD ** -0.5)
        vh = v_ref[:, hs]
        s = s_scr[h]
        outs = []
        for ci in range(n_chunks):
            rs = slice(ci * chunk, (ci + 1) * chunk)
            qc, kc, vc = qr[rs], kr[rs], vh[rs].astype(BF16)
            att = lax.dot_general(qc.astype(BF16), kc.astype(BF16), (((1,), (1,)), ((), ())),
                                  preferred_element_type=F32) * dmask_ref[h]
            o = jnp.dot(att.astype(BF16), vc, preferred_element_type=F32)
            o += jnp.dot((qc * qdec_ref[h]).astype(BF16), s.astype(BF16), preferred_element_type=F32)
            kd = (kc * kdec_ref[h]).astype(BF16)
            s = s * cdec_ref[h, 0:1, :] + lax.dot_general(kd, vc, (((0,), (0,)), ((), ())),
                                                  preferred_element_type=F32)
            outs.append(o)
        s_scr[h] = s
        o = outs[0] if n_chunks == 1 else jnp.concatenate(outs, axis=0)
        mu = jnp.mean(o, axis=-1, keepdims=True)
        var = jnp.mean(jnp.square(o - mu), axis=-1, keepdims=True)
        o = (o - mu) * lax.rsqrt(var + EPS) * nw_ref[h:h + 1, :] + nb_ref[h:h + 1, :]
        gh = g_ref[:, hs]
        o_ref[:, hs] = (o * (gh * jax.nn.sigmoid(gh))).astype(o_ref.dtype)

    @pl.when(c == pl.num_programs(1) - 1)
    def _():
        sn_ref[...] = s_scr[...]


def retention(proj, row0, n_seq, t_seq, cos, sin, s0, norm_w, norm_b, *, chunk, chunks_per_step, layer=None):
    rb = chunk * chunks_per_step
    steps = t_seq // rb
    assert row0 % rb == 0 and t_seq % rb == 0
    b0 = row0 // rb
    log_g = jnp.log1p(-jnp.exp2(-5.0 - jnp.arange(RET_HEADS, dtype=F32)))
    idx = jnp.arange(chunk, dtype=F32)
    rel = idx[:, None] - idx[None, :]
    dmask = jnp.where(rel >= 0, jnp.exp(log_g[:, None, None] * jnp.maximum(rel, 0.0)), 0.0)
    ones = jnp.ones((1, 1, RET_HD), F32)
    qdec = jnp.exp(log_g[:, None] * (idx + 1.0))[:, :, None] * ones
    kdec = jnp.exp(log_g[:, None] * (chunk - 1.0 - idx))[:, :, None] * ones
    cdec = jnp.exp(log_g * chunk)[:, None, None] * jnp.ones((1, SUBLANES, RET_HD), F32)

    def col(j):
        return pl.BlockSpec((rb, GROUP_W), lambda s, c: (b0 + s * steps + c, j))

    full = lambda shape: pl.BlockSpec(shape, lambda s, c: (0,) * len(shape))
    if s0.ndim == 5:
        s0_spec = pl.BlockSpec((None, None, RET_HEADS, RET_HD, RET_HD), lambda s, c: (layer, s, 0, 0, 0))
    else:
        s0_spec = pl.BlockSpec((None, RET_HEADS, RET_HD, RET_HD), lambda s, c: (s, 0, 0, 0))
    kern = functools.partial(_ret_kernel, chunk=chunk, n_chunks=chunks_per_step)
    return pl.pallas_call(
        kern,
        grid=(n_seq, steps),
        in_specs=[col(0), col(1), col(2), col(3),
                  pl.BlockSpec((rb, RET_HD), lambda s, c: (c, 0)),
                  pl.BlockSpec((rb, RET_HD), lambda s, c: (c, 0)),
                  full((RET_HEADS, chunk, chunk)),
                  full((RET_HEADS, chunk, RET_HD)),
                  full((RET_HEADS, chunk, RET_HD)),
                  full((RET_HEADS, SUBLANES, RET_HD)),
                  full((RET_HEADS, RET_HD)),
                  full((RET_HEADS, RET_HD)),
                  s0_spec],
        out_specs=[pl.BlockSpec((rb, GROUP_W), lambda s, c: (s * steps + c, 0)),
                   pl.BlockSpec((None, RET_HEADS, RET_HD, RET_HD), lambda s, c: (s, 0, 0, 0))],
        out_shape=[jax.ShapeDtypeStruct((n_seq * t_seq, GROUP_W), BF16),
                   jax.ShapeDtypeStruct((n_seq, RET_HEADS, RET_HD, RET_HD), F32)],
        scratch_shapes=[pltpu.VMEM((RET_HEADS, RET_HD, RET_HD), F32)],
        compiler_params=_cparams(("arbitrary", "arbitrary")),
        name="retention",
    )(proj, proj, proj, proj, cos, sin, dmask, qdec, kdec, cdec, norm_w, norm_b, s0)


def _diff_prep_kernel(q_ref, k_ref, v_ref, c_ref, s1_ref, s2_ref, *rest, n_prompt_tiles, n_alias):
    q0_ref, q1_ref, kn_ref, kb_ref, vt_ref, kp_ref, ks_ref, vp_ref, vs_ref = rest[n_alias:]
    i = pl.program_id(0)
    tm = q_ref.shape[0]
    c, s1, s2 = c_ref[...], s1_ref[...], s2_ref[...]
    lane = lax.broadcasted_iota(jnp.int32, c.shape, 1)
    lo = lane < DIFF_QD
    krs = []
    for j in range(GROUP_W // LANES):
        cs = slice(j * LANES, (j + 1) * LANES)
        q = q_ref[:, cs]
        k = k_ref[:, cs]
        qr = q * c + pltpu.roll(q, LANES - ROPE_DIM // 2, axis=1) * s1 + pltpu.roll(q, ROPE_DIM // 2, axis=1) * s2
        kr = k * c + pltpu.roll(k, LANES - ROPE_DIM // 2, axis=1) * s1 + pltpu.roll(k, ROPE_DIM // 2, axis=1) * s2
        qr = qr * (DIFF_QD ** -0.5 * LOG2_E)
        q0_ref[:, cs] = jnp.where(lo, qr, 0.0).astype(BF16)
        q1_ref[:, cs] = jnp.where(lo, 0.0, qr).astype(BF16)
        kn_ref[:, cs] = kr
        kb_ref[:, cs] = kr.astype(BF16)
        krs.append(kr)
    v = v_ref[...]
    vt_ref[...] = v.T.astype(BF16)

    def emit(k_out, v_out):
        for j in range(DIFF_HEADS):
            k_out[pl.ds(j, tm, stride=DIFF_HEADS), :] = krs[j]
            v_out[pl.ds(j, tm, stride=DIFF_HEADS), :] = v[:, j * DIFF_VD:(j + 1) * DIFF_VD]

    @pl.when(i < n_prompt_tiles)
    def _():
        emit(kp_ref, vp_ref)

    @pl.when(i >= n_prompt_tiles)
    def _():
        emit(ks_ref, vs_ref)


def diff_prep(proj, c, s1, s2, *, layer, depth, mp, kv_prev=None, tm=512):
    m = proj.shape[0]
    ms = m - mp
    assert mp % tm == 0 and ms % tm == 0
    npt = mp // tm
    col = lambda j: pl.BlockSpec((tm, GROUP_W), lambda i: (i, j))
    tab = pl.BlockSpec((tm, LANES), lambda i: (i, 0))
    out = pl.BlockSpec((tm, GROUP_W), lambda i: (i, 0))
    kv_p = pl.BlockSpec((None, tm * DIFF_HEADS, DIFF_VD), lambda i: (layer, jnp.minimum(i, npt - 1), 0))
    kv_s = pl.BlockSpec((None, tm * DIFF_HEADS, DIFF_VD), lambda i: (layer, jnp.maximum(i - npt, 0), 0))
    shape_p = jax.ShapeDtypeStruct((depth, mp * DIFF_HEADS, DIFF_VD), F32)
    shape_s = jax.ShapeDtypeStruct((depth, ms * DIFF_HEADS, DIFF_VD), F32)
    prev = () if kv_prev is None else tuple(kv_prev)
    n_alias = len(prev)
    kern = functools.partial(_diff_prep_kernel, n_prompt_tiles=npt, n_alias=n_alias)
    outs = pl.pallas_call(
        kern,
        grid=(m // tm,),
        in_specs=[col(0), col(1), col(2), tab, tab, tab] + [pl.BlockSpec(memory_space=pl.ANY)] * n_alias,
        out_specs=[out] * 4 + [pl.BlockSpec((GROUP_W, tm), lambda i: (0, i)), kv_p, kv_s, kv_p, kv_s],
        out_shape=[jax.ShapeDtypeStruct((m, GROUP_W), BF16), jax.ShapeDtypeStruct((m, GROUP_W), BF16),
                   jax.ShapeDtypeStruct((m, GROUP_W), F32), jax.ShapeDtypeStruct((m, GROUP_W), BF16),
                   jax.ShapeDtypeStruct((GROUP_W, m), BF16), shape_p, shape_s, shape_p, shape_s],
        input_output_aliases={6 + a: 5 + a for a in range(n_alias)},
        compiler_params=_cparams(("arbitrary",)),
        name="diff_prep",
    )(proj, proj, proj, c, s1, s2, *prev)
    return outs[:5], outs[5:]


def _diff_flash_kernel(qi_ref, ki_ref, lam_ref, q0_ref, q1_ref, k_ref, vt_ref, sub_ref, o_ref,
                       m0, l0, a0, m1, l1, a1, *, bq, bk, out_scale):
    step = pl.program_id(1)
    qi = qi_ref[step]
    ki = ki_ref[step]
    ratio = bq // bk

    @pl.when(ki == 0)
    def _():
        m0[...] = jnp.full(m0.shape, NEG_INF, F32)
        m1[...] = jnp.full(m1.shape, NEG_INF, F32)
        l0[...] = jnp.zeros(l0.shape, F32)
        l1[...] = jnp.zeros(l1.shape, F32)
        a0[...] = jnp.zeros(a0.shape, F32)
        a1[...] = jnp.zeros(a1.shape, F32)

    def update(q_ref, m_ref, l_ref, a_ref, lo, diagonal):
        st = lax.dot_general(k_ref[...], q_ref[lo:, :], (((1,), (1,)), ((), ())), preferred_element_type=F32)
        if diagonal:
            kpos = lax.broadcasted_iota(jnp.int32, (bk, bq - lo), 0)
            qpos = lax.broadcasted_iota(jnp.int32, (bk, bq - lo), 1)
            st = jnp.where(kpos <= qpos, st, NEG_INF)
        m_prev = m_ref[:, lo:]
        m_new = jnp.maximum(m_prev, jnp.max(st, axis=0, keepdims=True))
        alpha = jnp.exp2(m_prev - m_new)
        pt = jnp.exp2(st - m_new)
        l_ref[:, lo:] = alpha * l_ref[:, lo:] + jnp.sum(pt, axis=0, keepdims=True)
        a_ref[:, lo:] = alpha * a_ref[:, lo:] + jnp.dot(vt_ref[...], pt.astype(BF16), preferred_element_type=F32)
        m_ref[:, lo:] = m_new

    @pl.when(ki < ratio * qi)
    def _():
        update(q0_ref, m0, l0, a0, 0, False)
        update(q1_ref, m1, l1, a1, 0, False)

    for d in range(ratio):
        @pl.when(ki == ratio * qi + d)
        def _(d=d):
            update(q0_ref, m0, l0, a0, d * bk, True)
            update(q1_ref, m1, l1, a1, d * bk, True)

    @pl.when(ki == ratio * qi + ratio - 1)
    def _():
        ot = a0[...] / l0[...] - lam_ref[0, 0] * (a1[...] / l1[...])
        yt = ot * lax.rsqrt(jnp.mean(ot * ot, axis=0, keepdims=True) + EPS) * (sub_ref[...] * out_scale)
        o_ref[...] = yt.T.astype(o_ref.dtype)


def diff_attention_prompt(q0, q1, kb, vt, lam, subln, lam_init, *, t, bq=1024, bk=512):
    assert bq % bk == 0 and t % bq == 0
    nq, ratio = t // bq, bq // bk
    qi_tbl = np.concatenate([np.full(ratio * (i + 1), i, np.int32) for i in range(nq)])
    ki_tbl = np.concatenate([np.arange(ratio * (i + 1), dtype=np.int32) for i in range(nq)])
    kern = functools.partial(_diff_flash_kernel, bq=bq, bk=bk, out_scale=1.0 - lam_init)
    grid_spec = pltpu.PrefetchScalarGridSpec(
        num_scalar_prefetch=2,
        grid=(DIFF_HEADS, len(qi_tbl)),
        in_specs=[pl.BlockSpec(memory_space=pltpu.SMEM),
                  pl.BlockSpec((bq, LANES), lambda h, s, qi, ki: (qi[s], h)),
                  pl.BlockSpec((bq, LANES), lambda h, s, qi, ki: (qi[s], h)),
                  pl.BlockSpec((bk, LANES), lambda h, s, qi, ki: (ki[s], h)),
                  pl.BlockSpec((LANES, bk), lambda h, s, qi, ki: (h, ki[s])),
                  pl.BlockSpec((DIFF_VD, 1), lambda h, s, qi, ki: (0, 0))],
        out_specs=pl.BlockSpec((bq, LANES), lambda h, s, qi, ki: (qi[s], h)),
        scratch_shapes=[pltpu.VMEM((1, bq), F32), pltpu.VMEM((1, bq), F32), pltpu.VMEM((DIFF_VD, bq), F32),
                        pltpu.VMEM((1, bq), F32), pltpu.VMEM((1, bq), F32), pltpu.VMEM((DIFF_VD, bq), F32)],
    )
    return pl.pallas_call(
        kern,
        grid_spec=grid_spec,
        out_shape=jax.ShapeDtypeStruct((t, GROUP_W), BF16),
        compiler_params=_cparams(("arbitrary", "arbitrary")),
        name="diff_attention_prompt",
    )(jnp.asarray(qi_tbl), jnp.asarray(ki_tbl), lam.reshape(1, 1), q0, q1, kb, vt, subln.reshape(DIFF_VD, 1))


def _diff_paged_kernel(pt_ref, lam_ref, q_ref, kn_ref, vn_ref, sub_ref, sel_ref, qmask_ref, *rest,
                       n_pages, t_new, out_scale):
    k_pages = rest[:n_pages]
    v_pages = rest[n_pages:2 * n_pages]
    o_ref = rest[2 * n_pages]
    kpad, vpad, qpad, s_scr = rest[2 * n_pages + 1:]
    half = DIFF_HEADS * t_new
    b = pl.program_id(0)

    @pl.when(b == 0)
    def _():
        kpad[...] = jnp.zeros(kpad.shape, kpad.dtype)
        vpad[...] = jnp.zeros(vpad.shape, vpad.dtype)
        qpad[...] = jnp.zeros(qpad.shape, qpad.dtype)

    kpad[0:t_new, :] = kn_ref[...]
    vpad[0:t_new, :] = vn_ref[...]
    qpad[0:t_new, :] = q_ref[...]
    qbd = (lax.dot_general(qpad[...].astype(BF16), sel_ref[...], (((0,), (0,)), ((), ())),
                           preferred_element_type=F32) * qmask_ref[...]).astype(BF16)

    def heads_to_lanes(ref):
        return jnp.concatenate([ref[pl.ds(h, PAGE_SIZE, stride=DIFF_HEADS), :] for h in range(DIFF_HEADS)],
                               axis=1).astype(BF16)

    key =lax.broadcasted_iota(jnp.int32, (PAGE_SIZE, LANES), 0)
    qpos = lax.broadcasted_iota(jnp.int32, (PAGE_SIZE, LANES), 1) & (t_new - 1)
    m = jnp.full((1, LANES), NEG_INF, F32)
    for p in range(n_pages + 1):
        if p < n_pages:
            s = jnp.dot(heads_to_lanes(k_pages[p]), qbd, preferred_element_type=F32)
        else:
            s = jnp.dot(kpad[...].astype(BF16), qbd, preferred_element_type=F32)
            s = jnp.where(key <= qpos, s, NEG_INF)
        s_scr[p] = s
        m = jnp.maximum(m, jnp.max(s, axis=0, keepdims=True))
    l = jnp.zeros((1, LANES), F32)
    acc = jnp.zeros((LANES, GROUP_W), F32)
    for p in range(n_pages + 1):
        e = jnp.exp2(s_scr[p] - m)
        l = l + jnp.sum(e, axis=0, keepdims=True)
        vsrc = heads_to_lanes(v_pages[p]) if p < n_pages else vpad[...].astype(BF16)
        acc += lax.dot_general(e.astype(BF16), vsrc, (((0,), (0,)), ((), ())), preferred_element_type=F32)
    inv_col = jnp.broadcast_to(1.0 / l, (LANES, LANES)).T
    lam = lam_ref[0, 0]
    outs = []
    for h in range(DIFF_HEADS):
        r0, r1 = h * t_new, half + h * t_new
        cs = slice(h * DIFF_VD, (h + 1) * DIFF_VD)
        o = acc[r0:r0 + t_new, cs] * inv_col[r0:r0 + t_new, :] - lam * (acc[r1:r1 + t_new, cs] * inv_col[r1:r1 + t_new, :])
        y = o * lax.rsqrt(jnp.mean(o * o, axis=-1, keepdims=True) + EPS)
        outs.append(y * sub_ref[...] * out_scale)
    o_ref[...] = jnp.concatenate(outs, axis=1).astype(o_ref.dtype)


def diff_attention_sample(qs, kn, proj, row0, cache_k, cache_v, layer, page_table, lam, subln, lam_init):
    nb, t_new, _ = qs.shape
    assert row0 % t_new == 0
    b0 = row0 // t_new
    n_pages = page_table.shape[1]
    rows = PAGE_SIZE * DIFF_HEADS
    assert t_new & (t_new - 1) == 0
    half = DIFF_HEADS * t_new
    col = np.arange(LANES)
    valid = col < 2 * half
    sel = ((np.arange(PAGE_SIZE)[:, None] == (col % t_new)[None, :]) & valid[None, :])
    blk_of_col = 2 * ((col % half) // t_new) + col // half
    qmask = ((np.arange(GROUP_W)[:, None] // DIFF_QD) == blk_of_col[None, :]) & valid[None, :]
    sel = jnp.asarray(sel, BF16)
    qmask = jnp.asarray(qmask, F32)

    def page_spec(p):
        return pl.BlockSpec((None, None, rows, DIFF_VD), lambda b, pt: (layer, pt[b, p], 0, 0))

    kern = functools.partial(_diff_paged_kernel, n_pages=n_pages, t_new=t_new, out_scale=1.0 - lam_init)
    grid_spec = pltpu.PrefetchScalarGridSpec(
        num_scalar_prefetch=1,
        grid=(nb,),
        in_specs=[pl.BlockSpec(memory_space=pltpu.SMEM),
                  pl.BlockSpec((t_new, GROUP_W), lambda b, pt: (b, 0)),
                  pl.BlockSpec((t_new, GROUP_W), lambda b, pt: (b0 + b, 0)),
                  pl.BlockSpec((t_new, GROUP_W), lambda b, pt: (b0 + b, 2)),
                  pl.BlockSpec((1, LANES), lambda b, pt: (0, 0)),
                  pl.BlockSpec((PAGE_SIZE, LANES), lambda b, pt: (0, 0)),
                  pl.BlockSpec((GROUP_W, LANES), lambda b, pt: (0, 0))]
                 + [page_spec(p) for p in range(n_pages)] * 2,
        out_specs=pl.BlockSpec((t_new, GROUP_W), lambda b, pt: (b, 0)),
        scratch_shapes=[pltpu.VMEM((PAGE_SIZE, GROUP_W), F32), pltpu.VMEM((PAGE_SIZE, GROUP_W), F32),
                        pltpu.VMEM((PAGE_SIZE, GROUP_W), F32),
                        pltpu.VMEM((n_pages + 1, PAGE_SIZE, LANES), F32)],
    )
    return pl.pallas_call(
        kern,
        grid_spec=grid_spec,
        out_shape=jax.ShapeDtypeStruct((nb * t_new, GROUP_W), BF16),
        compiler_params=_cparams(("arbitrary",)),
        name="diff_attention_sample",
    )(page_table, lam.reshape(1, 1), qs.reshape(nb * t_new, GROUP_W), kn, proj, subln.reshape(1, LANES), sel, qmask,
      *([cache_k] * n_pages), *([cache_v] * n_pages))


def _interleave64(x, y):
    lane = lax.broadcasted_iota(jnp.int32, (x.shape[0], LANES), 1)
    lo = lane < RWKV_HD
    blocks = []
    for c in range(GROUP_W // LANES):
        xc = x[:, c * LANES:(c + 1) * LANES]
        yc = y[:, c * LANES:(c + 1) * LANES]
        rx = pltpu.roll(xc, RWKV_HD, axis=1)
        ry = pltpu.roll(yc, RWKV_HD, axis=1)
        blocks.append(jnp.where(lo, xc, ry))
        blocks.append(jnp.where(lo, rx, yc))
    return jnp.concatenate(blocks, axis=1)


def _rwkv_prep_kernel(cols_ref, prev_ref, mu_ref, w0_ref, a0_ref, kk_ref, ka_ref, rk_ref,
                      w2h_ref, w2l_ref, a2h_ref, a2l_ref, g2h_ref, g2l_ref, e_ref,
                      ar_ref, bt_ref, kt_ref, g8_ref, v_ref, br_ref, kr_ref, bon_ref, g_ref, *, shifted):
    cols = cols_ref[...]
    if shifted:
        before = jnp.where(pl.program_id(0) == 0, 0.0, prev_ref[SUBLANES - 1:SUBLANES, :])
        first = lax.broadcasted_iota(jnp.int32, cols.shape, 0) == 0
        prev = jnp.where(first, before, pltpu.roll(cols, 1, axis=0))
    else:
        prev = prev_ref[...]
    xm = cols + (prev - cols) * mu_ref[...]
    o1 = GROUP_W
    r, k, v = xm[:, 0:o1], xm[:, o1:2 * o1], xm[:, 2 * o1:3 * o1]
    lora = xm[:, 3 * o1:3 * o1 + LANES]
    gl = xm[:, 3 * o1 + LANES:3 * o1 + 2 * LANES]
    wterm = _dot3(jnp.tanh(lora), w2h_ref[...], w2l_ref[...])
    aterm = _dot3(lora, a2h_ref[...], a2l_ref[...])
    z = -(w0_ref[...] + wterm)
    softplus = jnp.maximum(z, 0.0) + jnp.log1p(jnp.exp(-jnp.abs(z)))
    w = -softplus - 0.5
    log_decay = -jnp.exp(w)
    a = jax.nn.sigmoid(a0_ref[...] + aterm)
    g = _dot3(jax.nn.sigmoid(gl), g2h_ref[...], g2l_ref[...])
    e = e_ref[...]
    kk = k * kk_ref[...]
    kk = kk * lax.rsqrt(jnp.maximum(_dot_hilo(kk * kk, e), 1e-24))
    k2 = k * (1.0 + (a - 1.0) * ka_ref[...])
    bv = kk * a
    n = cols.shape[0]
    sub = lax.broadcasted_iota(jnp.int32, log_decay.shape, 0) & (SUBLANES - 1)
    csum = log_decay
    rsum = log_decay
    for d in (1, 2, 4):
        csum = csum + jnp.where(sub >= d, pltpu.roll(csum, d, axis=0), 0.0)
        rsum = rsum + jnp.where(sub < SUBLANES - d, pltpu.roll(rsum, n - d, axis=0), 0.0)
    gamma = jnp.exp(csum)
    inv_gamma = jnp.exp(-csum)
    rnd = lambda t: t.astype(BF16).astype(F32)
    ar_ref[...] = _interleave64(rnd(jnp.exp(csum - log_decay) * (-kk)), rnd(r * gamma))
    bt_ref[...] = rnd(bv * inv_gamma)
    kt_ref[...] = rnd(k2 * inv_gamma)
    g8_ref[...] = jnp.exp(csum + rsum - log_decay)
    v_ref[...] = v
    br_ref[...] = _dot_hilo(bv * r, e)
    kr_ref[...] = _dot_hilo(k2 * r, e)
    bon_ref[...] = _dot_hilo(r * k2 * rk_ref[...], e)
    g_ref[...] = g


def rwkv_prep(cols, prev, p, e512, *, row0, m, tm=256):
    assert row0 % tm == 0 and m % tm == 0
    b0 = row0 // tm
    shifted = prev is None
    if shifted:
        prev_arr = cols
        per8 = tm // SUBLANES
        prev_spec = pl.BlockSpec((SUBLANES, RWKV_COLS), lambda i: (jnp.maximum((b0 + i) * per8 - 1, 0), 0))
    else:
        prev_arr = prev
        prev_spec = pl.BlockSpec((tm, RWKV_COLS), lambda i: (i, 0))
    row = lambda n: pl.BlockSpec((1, n), lambda i: (0, 0))
    full = lambda a: pl.BlockSpec(a.shape, lambda i: (0, 0))
    wide = pl.BlockSpec((tm, 2 * GROUP_W), lambda i: (i, 0))
    nar = pl.BlockSpec((tm, GROUP_W), lambda i: (i, 0))
    z64 = jnp.zeros((64, GROUP_W), F32)
    w2p = jnp.concatenate([p["w2"], z64], axis=0)
    a2p = jnp.concatenate([z64, p["a2"]], axis=0)
    w2h, w2l = _split_bf16(w2p)
    a2h, a2l = _split_bf16(a2p)
    g2h, g2l = _split_bf16(p["g2"])
    mats = [w2h, w2l, a2h, a2l, g2h, g2l, e512]
    return pl.pallas_call(
        functools.partial(_rwkv_prep_kernel, shifted=shifted),
        grid=(m // tm,),
        in_specs=[pl.BlockSpec((tm, RWKV_COLS), lambda i: (b0 + i, 0)),
                  prev_spec,
                  row(RWKV_COLS), row(GROUP_W), row(GROUP_W), row(GROUP_W), row(GROUP_W), row(GROUP_W)]
                 + [full(a) for a in mats],
        out_specs=[wide, nar, nar, nar, nar, nar, nar, nar, nar],
        out_shape=[jax.ShapeDtypeStruct((m, 2 * GROUP_W), F32)]
                  + [jax.ShapeDtypeStruct((m, GROUP_W), F32)] * 8,
        compiler_params=_cparams(("parallel",)),
        name="rwkv_prep",
    )(cols, prev_arr, p["mu"].reshape(1, -1), p["w0"].reshape(1, -1), p["a0"].reshape(1, -1),
      p["kk"].reshape(1, -1), p["ka"].reshape(1, -1), p["rk"].reshape(1, -1), *mats)


def _rwkv_scan_kernel(ar_ref, bt_ref, kt_ref, g8_ref, v_ref, br_ref, kr_ref, bon_ref, g_ref, lnw_ref, lnb_ref,
                      e_ref, s0_ref, o_ref, sn_ref, s_scr, y_scr, *, n_seq, t_len, pairs):
    c = pl.program_id(2)
    half = RWKV_HD
    lane_lo = lax.broadcasted_iota(jnp.int32, (half, LANES), 1) < half
    diag = (lax.broadcasted_iota(jnp.int32, (half, LANES), 1) & (half - 1)) == \
        lax.broadcasted_iota(jnp.int32, (half, LANES), 0)
    ones_blocks = e_ref[...]

    def tokens8(g, states):
        r0 = pl.multiple_of(g * SUBLANES, SUBLANES)
        rows = pl.ds(r0, SUBLANES)
        ar8, bt8, kt8, g8 = ar_ref[rows, :], bt_ref[rows, :], kt_ref[rows, :], g8_ref[rows, :]
        v8, br8, kr8 = v_ref[rows, :], br_ref[rows, :], kr_ref[rows, :]
        states = list(states)
        ys = [[] for _ in range(pairs)]

        def by_transpose(tile, k, pp):
            top = jnp.broadcast_to(tile[k:k + 1, 2 * pp * LANES:(2 * pp + 1) * LANES], (half, LANES))
            bot = jnp.broadcast_to(tile[k:k + 1, (2 * pp + 1) * LANES:(2 * pp + 2) * LANES], (half, LANES))
            t = jnp.concatenate([top, bot], axis=0).astype(BF16).T
            return t[0:half].astype(F32), t[half:2 * half].astype(F32)

        def by_matmul(x8, y8, k, ls):
            zx = jnp.where(diag, jnp.broadcast_to(x8[k:k + 1, ls], (half, LANES)), 0.0)
            zy = jnp.where(diag, jnp.broadcast_to(y8[k:k + 1, ls], (half, LANES)), 0.0)
            t = jnp.dot(jnp.concatenate([zx, zy], axis=0).astype(BF16), ones_blocks, preferred_element_type=F32)
            return t[0:half], t[half:2 * half]

        for k in range(SUBLANES):
            for pp in range(pairs):
                s = states[pp]
                ls = slice(pp * LANES, (pp + 1) * LANES)
                ac, rc = by_transpose(ar8, k, pp)
                bc, kc = by_matmul(bt8, kt8, k, ls)
                u = jnp.sum(s * ac, axis=0, keepdims=True)
                yp = jnp.sum(s * rc, axis=0, keepdims=True)
                vrow = v8[k:k + 1, ls]
                ys[pp].append(yp + u * br8[k:k + 1, ls] + vrow * kr8[k:k + 1, ls])
                states[pp] = s + bc * u + kc * vrow
        for pp in range(pairs):
            ls = slice(pp * LANES, (pp + 1) * LANES)
            y_scr[rows, ls] = jnp.concatenate(ys[pp], axis=0)
            gt = jnp.broadcast_to(g8[0:1, ls], (LANES, LANES)).T
            states[pp] = states[pp] * jnp.where(lane_lo, gt[0:half], gt[half:2 * half])
        return tuple(states)

    @pl.when(c == 0)
    def _():
        s_scr[...] = s0_ref[...]

    groups = t_len // SUBLANES

    def seq(si, carry):
        states = tuple(s_scr[si, pp] for pp in range(pairs))
        states = lax.fori_loop(0, groups, lambda g, st: tokens8(si * groups + g, st), states)
        for pp in range(pairs):
            s_scr[si, pp] = states[pp]
        return carry

    lax.fori_loop(0, n_seq, seq, 0)

    e = e_ref[...]
    for pp in range(pairs):
        ls = slice(pp * LANES, (pp + 1) * LANES)
        y = y_scr[:, ls]
        mu = _dot_hilo(y, e) * (1.0 / RWKV_HD)
        d = y - mu
        var = _dot_hilo(d * d, e) * (1.0 / RWKV_HD)
        yn = d * lax.rsqrt(var + RWKV_LN_EPS) * lnw_ref[:, ls] + lnb_ref[:, ls]
        o_ref[:, ls] = ((yn + bon_ref[:, ls] * v_ref[:, ls]) * g_ref[:, ls]).astype(o_ref.dtype)

    @pl.when(c == pl.num_programs(2) - 1)
    def _():
        sn_ref[...] = s_scr[...]


def rwkv_scan(ar, bt, kt, g8, v, br, kr, bon, g, ln_w, ln_b, e128, s0, *, t_seq, seq_per_step, t_step, pairs=2):
    n_seq = s0.shape[0]
    chunks = t_seq // t_step
    rb = seq_per_step * t_step
    assert seq_per_step == 1 or chunks == 1
    n_pairs = RWKV_HEADS // 2
    assert n_pairs % pairs == 0
    w = pairs * LANES

    def rows(width):
        return pl.BlockSpec((rb, width), lambda p, s, c: (s * chunks + c, p))

    kern = functools.partial(_rwkv_scan_kernel, n_seq=seq_per_step, t_len=t_step, pairs=pairs)
    st = pl.BlockSpec((seq_per_step, pairs, RWKV_HD, LANES), lambda p, s, c: (s, p, 0, 0))
    return pl.pallas_call(
        kern,
        grid=(n_pairs // pairs, n_seq // seq_per_step, chunks),
        in_specs=[rows(2 * w),
                  rows(w), rows(w), rows(w), rows(w), rows(w), rows(w), rows(w), rows(w),
                  pl.BlockSpec((1, w), lambda p, s, c: (0, p)),
                  pl.BlockSpec((1, w), lambda p, s, c: (0, p)),
                  pl.BlockSpec((LANES, LANES), lambda p, s, c: (0, 0)),
                  st],
        out_specs=[rows(w), st],
        out_shape=[jax.ShapeDtypeStruct((n_seq * t_seq, GROUP_W), BF16),
                   jax.ShapeDtypeStruct(s0.shape, F32)],
        scratch_shapes=[pltpu.VMEM((seq_per_step, pairs, RWKV_HD, LANES), F32),
                        pltpu.VMEM((rb, w), F32)],
        compiler_params=_cparams(("arbitrary", "arbitrary", "arbitrary")),
        name="rwkv_scan",
    )(ar, bt, kt, g8, v, br, kr, bon, g, ln_w.reshape(1, -1), ln_b.reshape(1, -1), e128, s0)


def _s5_kernel(u_ref, bb_ref, cc_ref, pw_ref, ad_ref, d_ref, wg_ref, bg_ref, nw_ref, sr0_ref, si0_ref,
               o_ref, srn_ref, sin_ref, bur, bui, cr, ci, wgb, *, tm, per_group_state):
    i = pl.program_id(0)
    nch = S5_GROUPS * S5_N
    ngrp = tm // SUBLANES

    @pl.when(i == 0)
    def _():
        wgb[...] = wg_ref[...].astype(BF16)
        if not per_group_state:
            cr[...] = sr0_ref[...]
            ci[...] = si0_ref[...]

    u = u_ref[...]
    bu = jnp.dot(u.astype(BF16), bb_ref[...], preferred_element_type=F32)
    bur[...] = bu[:, 0:nch]
    bui[...] = bu[:, nch:2 * nch]
    sub = lax.broadcasted_iota(jnp.int32, (SUBLANES, nch), 0)

    def group(r0, c_r, c_i):
        xr = bur[pl.ds(r0, SUBLANES), :]
        xi = bui[pl.ds(r0, SUBLANES), :]
        for di, dsh in enumerate((1, 2, 4)):
            ar = ad_ref[2 * di:2 * di + 1, :]
            ai = ad_ref[2 * di + 1:2 * di + 2, :]
            keep = sub >= dsh
            sr = jnp.where(keep, pltpu.roll(xr, dsh, axis=0), 0.0)
            si = jnp.where(keep, pltpu.roll(xi, dsh, axis=0), 0.0)
            xr, xi = xr + ar * sr - ai * si, xi + ar * si + ai * sr
        pr = pw_ref[0:SUBLANES, :]
        pi = pw_ref[SUBLANES:2 * SUBLANES, :]
        xr, xi = xr + pr * c_r - pi * c_i, xi + pr * c_i + pi * c_r
        bur[pl.ds(r0, SUBLANES), :] = xr
        bui[pl.ds(r0, SUBLANES), :] = xi
        return xr[SUBLANES - 1:SUBLANES, :], xi[SUBLANES - 1:SUBLANES, :]

    def block(bi, carry):
        g0 = pl.multiple_of(bi * SUBLANES, SUBLANES)
        if per_group_state:
            st_r = sr0_ref[pl.ds(g0, SUBLANES), :]
            st_i = si0_ref[pl.ds(g0, SUBLANES), :]
            lasts_r, lasts_i = [], []
        else:
            c_r, c_i = cr[...], ci[...]
        for k in range(SUBLANES):
            r0 = pl.multiple_of((g0 + k) * SUBLANES, SUBLANES)
            if per_group_state:
                l_r, l_i = group(r0, st_r[k:k + 1, :], st_i[k:k + 1, :])
                lasts_r.append(l_r)
                lasts_i.append(l_i)
            else:
                c_r, c_i = group(r0, c_r, c_i)
        if per_group_state:
            srn_ref[pl.ds(g0, SUBLANES), :] = jnp.concatenate(lasts_r, axis=0)
            sin_ref[pl.ds(g0, SUBLANES), :] = jnp.concatenate(lasts_i, axis=0)
        else:
            cr[...] = c_r
            ci[...] = c_i
        return carry

    lax.fori_loop(0, ngrp // SUBLANES, block, 0)

    if not per_group_state:
        srn_ref[...] = cr[...]
        sin_ref[...] = ci[...]

    ccv = cc_ref[...]
    y = (jnp.dot(bur[...].astype(BF16), ccv[0:nch], preferred_element_type=F32)
         + jnp.dot(bui[...].astype(BF16), ccv[nch:2 * nch], preferred_element_type=F32))
    y = y + d_ref[...] * u
    y = 0.5 * y * (1.0 + jnp.tanh(math.sqrt(2.0 / math.pi) * (y + 0.044715 * (y * y * y))))
    gate = jnp.dot(y.astype(BF16), wgb[...], preferred_element_type=F32) + bg_ref[...]
    y = y * jax.nn.sigmoid(gate)
    y = y * lax.rsqrt(jnp.mean(y * y, axis=-1, keepdims=True) + EPS)
    o_ref[...] = (y * nw_ref[...]).astype(o_ref.dtype)


def s5_mixer(u, p, sr0, si0, *, per_group_state, row0, m, tm=512):
    assert row0 % tm == 0 and m % tm == 0
    b0 = row0 // tm
    nch = S5_GROUPS * S5_N
    lr, li = p["lam_re"], p["lam_im"]
    dt = jnp.exp(p["log_step"])[:, None]
    mag = jnp.exp(lr * dt)
    ab_re, ab_im = mag * jnp.cos(li * dt), mag * jnp.sin(li * dt)
    den = lr * lr + li * li
    cf_re = ((ab_re - 1.0) * lr + ab_im * li) / den
    cf_im = (ab_im * lr - (ab_re - 1.0) * li) / den
    bb_re = cf_re[..., None] * p["b_re"] - cf_im[..., None] * p["b_im"]
    bb_im = cf_re[..., None] * p["b_im"] + cf_im[..., None] * p["b_re"]
    eye = jnp.eye(S5_GROUPS, dtype=F32)

    def bd_in(b):
        return jnp.einsum("gnc,gh->gchn", b, eye).reshape(GROUP_W, nch)

    def bd_out(cm):
        return jnp.einsum("gcn,gh->gnhc", cm, eye).reshape(nch, GROUP_W)

    bb = jnp.concatenate([bd_in(bb_re), bd_in(bb_im)], axis=1).astype(BF16)
    cc = jnp.concatenate([bd_out(p["c_re"]), -bd_out(p["c_im"])], axis=0).astype(BF16)
    ar, ai = ab_re.reshape(1, nch), ab_im.reshape(1, nch)
    pows = [(ar, ai)]
    for _ in range(SUBLANES - 1):
        pr, pi = pows[-1]
        pows.append((pr * ar - pi * ai, pr * ai + pi * ar))
    pw = jnp.concatenate([jnp.concatenate([q[0] for q in pows], axis=0),
                          jnp.concatenate([q[1] for q in pows], axis=0)], axis=0)
    ad = jnp.concatenate([pows[0][0], pows[0][1], pows[1][0], pows[1][1], pows[3][0], pows[3][1],
                          jnp.zeros((2, nch), F32)], axis=0)
    n_state = sr0.shape[0]
    full = lambda a: pl.BlockSpec(a.shape, lambda i: (0,) * a.ndim)
    row = lambda n: pl.BlockSpec((1, n), lambda i: (0, 0))
    if per_group_state:
        st = pl.BlockSpec((tm // SUBLANES, nch), lambda i: (i, 0))
    else:
        st = pl.BlockSpec((1, nch), lambda i: (0, 0))
    kern = functools.partial(_s5_kernel, tm=tm, per_group_state=per_group_state)
    return pl.pallas_call(
        kern,
        grid=(m // tm,),
        in_specs=[pl.BlockSpec((tm, GROUP_W), lambda i: (b0 + i, 0)), full(bb), full(cc), full(pw), full(ad),
                  row(GROUP_W), full(p["w_glu"]), row(GROUP_W), row(GROUP_W), st, st],
        out_specs=[pl.BlockSpec((tm, GROUP_W), lambda i: (i, 0)), st, st],
        out_shape=[jax.ShapeDtypeStruct((m, GROUP_W), BF16),
                   jax.ShapeDtypeStruct((n_state, nch), F32), jax.ShapeDtypeStruct((n_state, nch), F32)],
        scratch_shapes=[pltpu.VMEM((tm, nch), F32), pltpu.VMEM((tm, nch), F32),
                        pltpu.VMEM((1, nch), F32), pltpu.VMEM((1, nch), F32),
                        pltpu.VMEM((GROUP_W, GROUP_W), BF16)],
        compiler_params=_cparams(("arbitrary",)),
        name="s5_mixer",
    )(u, bb, cc, pw, ad, p["d"].reshape(1, -1), p["w_glu"], p["b_glu"].reshape(1, -1),
      p["norm"].reshape(1, -1), sr0, si0)


def _ffn_up_kernel(h_ref, wg_ref, wv_ref, cwg_ref, cwv_ref, cbg_ref, cbv_ref, c0ga_ref, c0gb_ref, c0va_ref,
                   c0vb_ref, act_ref, cnga_ref, cngb_ref, cnva_ref, cnvb_ref, wgb, wvb, hg, hv, *, shift, tm, off,
                   t_cols):
    mstep = pl.program_id(1)
    hist = 2 * shift

    @pl.when(mstep == 0)
    def _():
        wgb[...] = wg_ref[...].astype(BF16)
        wvb[...] = wv_ref[...].astype(BF16)
        hg[off - hist:off - shift, :] = c0ga_ref[...]
        hg[off - shift:off, :] = c0gb_ref[...]
        hv[off - hist:off - shift, :] = c0va_ref[...]
        hv[off - shift:off, :] = c0vb_ref[...]

    if t_cols:
        k = h_ref.shape[1] // t_cols
        hb = jnp.concatenate([h_ref[:, t * k:(t + 1) * k] for t in range(t_cols)], axis=0)
    else:
        hb = h_ref[...]
    hg[off:off + tm, :] = jnp.dot(hb, wgb[...], preferred_element_type=F32)
    hv[off:off + tm, :] = jnp.dot(hb, wvb[...], preferred_element_type=F32)

    def conv(hs, cw_ref, cb_ref):
        return (cb_ref[...] + cw_ref[0:1, :] * hs[off - hist:off - hist + tm, :]
                + cw_ref[1:2, :] * hs[off - shift:off - shift + tm, :]
                + cw_ref[2:3, :] * hs[off:off + tm, :])

    gate = conv(hg, cwg_ref, cbg_ref)
    val = conv(hv, cwv_ref, cbv_ref)
    act = (gate * jax.nn.sigmoid(gate) * val).astype(act_ref.dtype)
    if t_cols:
        tn = act.shape[1]
        for t in range(t_cols):
            act_ref[:, t * tn:(t + 1) * tn] = act[t * shift:(t + 1) * shift]
    else:
        act_ref[...] = act
    tail_g = hg[off + tm - hist:off + tm, :]
    tail_v = hv[off + tm - hist:off + tm, :]
    hg[off - hist:off, :] = tail_g
    hv[off - hist:off, :] = tail_v

    @pl.when(mstep == pl.num_programs(1) - 1)
    def _():
        cnga_ref[...] = tail_g[0:shift]
        cngb_ref[...] = tail_g[shift:hist]
        cnva_ref[...] = tail_v[0:shift]
        cnvb_ref[...] = tail_v[shift:hist]


def ffn_up(h, w_up, conv_w, conv_b, c0, *, time_major, row0=0, m=None, tm=None, tn=512, layer=None,
           out_rows=None):
    n_seq = c0.shape[0]
    nj = D_FF // tn
    if time_major:
        _, t_cols, k = h.shape
        shift, steps, tm = n_seq, 1, n_seq * t_cols
        h = h.reshape(n_seq, t_cols * k)
        h_spec = pl.BlockSpec((n_seq, t_cols * k), lambda j, i: (0, 0))
        act_spec = pl.BlockSpec((n_seq, t_cols * tn), lambda j, i: (0, j))
        act_shape = jax.ShapeDtypeStruct((n_seq, nj * t_cols * tn), BF16)
    else:
        k = h.shape[1]
        assert n_seq == 1 and row0 % tm == 0 and m % tm == 0
        shift, steps, b0, t_cols = 1, m // tm, row0 // tm, 0
        h_spec = pl.BlockSpec((tm, k), lambda j, i: (b0 + i, 0))
        act_spec = pl.BlockSpec((tm, tn), lambda j, i: (i, j))
        act_shape = jax.ShapeDtypeStruct((m if out_rows is None else out_rows, D_FF), BF16)
    hist = 2 * shift
    off = max(SUBLANES, hist)
    kern = functools.partial(_ffn_up_kernel, shift=shift, tm=tm, off=off, t_cols=t_cols)
    cw = jnp.concatenate([conv_w, jnp.zeros((SUBLANES - conv_w.shape[0], conv_w.shape[1]), F32)], axis=0)
    cb = conv_b.reshape(1, -1)

    c0 = c0.reshape(n_seq, 4 * D_FF)

    def c0_spec(tap, half):
        return pl.BlockSpec((n_seq, tn), lambda j, i: (0, (2 * tap + half) * nj + j))

    tap_out = pl.BlockSpec((shift, tn), lambda j, i: (0, j))
    tap_shape = jax.ShapeDtypeStruct((shift, D_FF), F32)
    outs = pl.pallas_call(
        kern,
        grid=(nj, steps),
        in_specs=[h_spec,
                  _weight_spec(w_up, layer, k, tn),
                  _weight_spec(w_up, layer, k, tn, nj),
                  pl.BlockSpec((SUBLANES, tn), lambda j, i: (0, j)),
                  pl.BlockSpec((SUBLANES, tn), lambda j, i: (0, j + nj)),
                  pl.BlockSpec((1, tn), lambda j, i: (0, j)),
                  pl.BlockSpec((1, tn), lambda j, i: (0, j + nj)),
                  c0_spec(0, 0), c0_spec(1, 0), c0_spec(0, 1), c0_spec(1, 1)],
        out_specs=[act_spec, tap_out, tap_out, tap_out, tap_out],
        out_shape=[act_shape, tap_shape, tap_shape, tap_shape, tap_shape],
        scratch_shapes=[pltpu.VMEM((k, tn), BF16), pltpu.VMEM((k, tn), BF16),
                        pltpu.VMEM((off + tm, tn), F32), pltpu.VMEM((off + tm, tn), F32)],
        compiler_params=_cparams(("parallel", "arbitrary")),
        name="ffn_up",
    )(h, w_up, w_up, cw, cw, cb, cb, c0, c0, c0, c0)
    act = outs[0]
    if time_major:
        act = jnp.transpose(act.reshape(n_seq, nj, t_cols, tn), (0, 2, 1, 3)).reshape(n_seq * t_cols, D_FF)
    return act, outs[1:]


def conv_state_from_taps(taps):
    ga, gb, va, vb = taps
    return jnp.stack([jnp.concatenate([ga, va], axis=1), jnp.concatenate([gb, vb], axis=1)], axis=1)


def _ret_rope_tables(pos):
    half = RET_HD // 2
    inv = jnp.power(RET_ROPE_BASE, -jnp.arange(half, dtype=F32) / half)
    ang = pos.astype(F32)[:, None] * inv[None, :]
    cos, sin = jnp.cos(ang), jnp.sin(ang)
    return jnp.concatenate([cos, cos], axis=1), jnp.concatenate([-sin, sin], axis=1)


def _diff_rope_tables(pos):
    half = ROPE_DIM // 2
    inv = jnp.power(ROPE_THETA, -jnp.arange(half, dtype=F32) / half)
    ang = pos.astype(F32)[:, None] * inv[None, :]
    cos, sin = jnp.cos(ang), jnp.sin(ang)
    n = pos.shape[0]
    rest = DIFF_QD - ROPE_DIM
    c = jnp.concatenate([cos, cos, jnp.ones((n, rest), F32)], axis=1)
    s1 = jnp.concatenate([-sin, jnp.zeros((n, half + rest), F32)], axis=1)
    s2 = jnp.concatenate([jnp.zeros((n, half), F32), sin, jnp.zeros((n, rest), F32)], axis=1)
    rep = LANES // DIFF_QD
    return jnp.tile(c, (1, rep)), jnp.tile(s1, (1, rep)), jnp.tile(s2, (1, rep))


def _rwkv_state_to_pairs(s):
    b = s.shape[0]
    s = s.reshape(b, RWKV_HEADS // 2, 2, RWKV_HD, RWKV_HD)
    return jnp.transpose(s, (0, 1, 4, 2, 3)).reshape(b, RWKV_HEADS // 2, RWKV_HD, LANES)


def _rwkv_state_from_pairs(s):
    b = s.shape[0]
    s = s.reshape(b, RWKV_HEADS // 2, RWKV_HD, 2, RWKV_HD)
    return jnp.transpose(s, (0, 1, 3, 4, 2)).reshape(b, RWKV_HEADS, RWKV_HD, RWKV_HD)


def kernel(x_prompt, x_sample, p_prompt, p_sample, cache_k, cache_v, page_table, state_ret, state_rwkv, state_rwkv_shift, state_s5_re, state_s5_im, state_ffn_conv, norm_mix, w_in, w_out, ret_norm_w, ret_norm_b, diff_lq1, diff_lk1, diff_lq2, diff_lk2, diff_subln, rwkv_mu, rwkv_w0, rwkv_w2, rwkv_a0, rwkv_a2, rwkv_g2, rwkv_kk, rwkv_ka, rwkv_rk, rwkv_ln_w, rwkv_ln_b, s5_lam_re, s5_lam_im, s5_log_step, s5_b_re, s5_b_im, s5_c_re, s5_c_im, s5_d, s5_w_glu, s5_b_glu, s5_norm, norm_ffn, ffn_w_up, ffn_conv_w, ffn_conv_b, ffn_w_down, norm_ple, ple_w_proj, ple_norm_e, ple_w_gate, norm_final):
    depth = w_in.shape[0]
    bp, tp, d = x_prompt.shape
    nb, ts, _ = x_sample.shape
    assert bp == 1
    mp, ms = bp * tp, nb * ts
    past_len = page_table.shape[1] * PAGE_SIZE
    nch = S5_GROUPS * S5_N

    x = jnp.concatenate([x_prompt.reshape(mp, d), x_sample.reshape(ms, d)], axis=0)
    p_all = jnp.concatenate([p_prompt.reshape(depth, mp, -1), p_sample.reshape(depth, ms, -1)], axis=1)
    pos_p = jnp.arange(tp, dtype=jnp.int32)
    pos_s = past_len + jnp.arange(ts, dtype=jnp.int32)
    ret_cos_p, ret_sin_p = _ret_rope_tables(pos_p)
    ret_cos_s, ret_sin_s = _ret_rope_tables(pos_s)
    pos_all = jnp.concatenate([pos_p, jnp.tile(pos_s, nb)])
    dc, ds1, ds2 = _diff_rope_tables(pos_all)
    cache_k2 = cache_k.reshape(cache_k.shape[0], cache_k.shape[1], PAGE_SIZE * DIFF_HEADS, DIFF_VD)
    cache_v2 = cache_v.reshape(cache_v.shape[0], cache_v.shape[1], PAGE_SIZE * DIFF_HEADS, DIFF_VD)
    head_of = jnp.arange(GROUP_W) // RWKV_HD
    e512 = (head_of[:, None] == head_of[None, :]).astype(BF16)
    e128 = e512[:LANES, :LANES]

    outs = {k: [] for k in ("ret_p", "ret_s", "rwkv_p", "rwkv_s", "sh_p", "sh_s",
                            "s5r_p", "s5i_p", "s5r_s", "s5i_s", "conv_p", "conv_s")}
    kv_out = None
    for i in range(depth):
        h = rmsnorm_rows(x, norm_mix[i], BF16)
        c_ret = matmul(h, w_in, layer=i, tn=1024, tm=1024, col0=0, ncols=RET_COLS, name="proj_ret")
        c_diff = matmul(h, w_in, layer=i, tn=512, tm=1024, col0=RET_COLS, ncols=DIFF_COLS, name="proj_diff")
        c_rwkv = matmul(h, w_in, layer=i, tn=896, tm=1024, col0=RET_COLS + DIFF_COLS, ncols=RWKV_COLS,
                        name="proj_rwkv")
        c_s5 = matmul(h, w_in, layer=i, tn=256, tm=1024, col0=RET_COLS + DIFF_COLS + RWKV_COLS, ncols=S5_COLS,
                      name="proj_s5")

        o_ret_p, s_ret_p = retention(c_ret, 0, bp, tp, ret_cos_p, ret_sin_p,
                                     jnp.zeros((bp, RET_HEADS, RET_HD, RET_HD), F32),
                                     ret_norm_w[i], ret_norm_b[i], chunk=RET_CHUNK, chunks_per_step=4)
        o_ret_s, s_ret_s = retention(c_ret, mp, nb, ts, ret_cos_s, ret_sin_s, state_ret,
                                     ret_norm_w[i], ret_norm_b[i], chunk=ts, chunks_per_step=1, layer=i)

        lam_init = 0.8 - 0.6 * math.exp(-0.3 * i)
        lam = (jnp.exp(jnp.sum(diff_lq1[i] * diff_lk1[i])) - jnp.exp(jnp.sum(diff_lq2[i] * diff_lk2[i])) + lam_init)
        (q0, q1, k_new, kb, vt), kv_out = diff_prep(c_diff, dc, ds1, ds2, layer=i, depth=depth, mp=mp,
                                                    kv_prev=kv_out)
        o_diff_p = diff_attention_prompt(q0, q1, kb, vt, lam, diff_subln[i], lam_init, t=mp)
        qs = (q0[mp:].astype(F32) + q1[mp:].astype(F32)).reshape(nb, ts, GROUP_W)
        o_diff_s = diff_attention_sample(qs, k_new, c_diff, mp, cache_k2, cache_v2, i, page_table,
                                         lam, diff_subln[i], lam_init)

        cr_s = c_rwkv[mp:].reshape(nb, ts, RWKV_COLS)
        prev_s = jnp.concatenate([state_rwkv_shift[i][:, None], cr_s[:, :-1]], axis=1).reshape(ms, RWKV_COLS)
        rp = dict(mu=rwkv_mu[i], w0=rwkv_w0[i], w2=rwkv_w2[i], a0=rwkv_a0[i], a2=rwkv_a2[i], g2=rwkv_g2[i],
                  kk=rwkv_kk[i], ka=rwkv_ka[i], rk=rwkv_rk[i].reshape(-1))
        lnw, lnb = rwkv_ln_w[i].reshape(-1), rwkv_ln_b[i].reshape(-1)
        o_rwkv_p, s_rwkv_p = rwkv_scan(*rwkv_prep(c_rwkv, None, rp, e512, row0=0, m=mp), lnw, lnb, e128,
                                       jnp.zeros((bp, RWKV_HEADS // 2, RWKV_HD, LANES), F32),
                                       t_seq=tp, seq_per_step=1, t_step=256, pairs=4)
        o_rwkv_s, s_rwkv_s = rwkv_scan(*rwkv_prep(c_rwkv, prev_s, rp, e512, row0=mp, m=ms), lnw, lnb, e128,
                                       _rwkv_state_to_pairs(state_rwkv[i]),
                                       t_seq=ts, seq_per_step=16, t_step=ts, pairs=4)

        sp = dict(lam_re=s5_lam_re[i], lam_im=s5_lam_im[i], log_step=s5_log_step[i], b_re=s5_b_re[i], b_im=s5_b_im[i],
                  c_re=s5_c_re[i], c_im=s5_c_im[i], d=s5_d[i], w_glu=s5_w_glu[i], b_glu=s5_b_glu[i], norm=s5_norm[i])
        o_s5_p, s5r_p, s5i_p = s5_mixer(c_s5, sp, jnp.zeros((1, nch), F32), jnp.zeros((1, nch), F32),
                                        per_group_state=False, row0=0, m=mp)
        o_s5_s, s5r_s, s5i_s = s5_mixer(c_s5, sp, state_s5_re[i].reshape(nb, nch), state_s5_im[i].reshape(nb, nch),
                                        per_group_state=True, row0=mp, m=ms)

        x = matmul_split([o_ret_p, o_diff_p, o_rwkv_p, o_s5_p], [o_ret_s, o_diff_s, o_rwkv_s, o_s5_s], w_out,
                         layer=i, tn=512, tm=1024, tiles=(x,), epilogue=lambda acc, res: res + acc, name="w_out")

        h2 = rmsnorm_rows(x, norm_ffn[i], BF16)
        act_p, taps_p = ffn_up(h2, ffn_w_up, ffn_conv_w[i], ffn_conv_b[i], jnp.zeros((bp, 2, 2 * D_FF), F32),
                               layer=i, time_major=False, row0=0, m=mp, tm=1024, out_rows=mp + ms)
        act_s, taps_s = ffn_up(h2[mp:].reshape(nb, ts, d), ffn_w_up, ffn_conv_w[i], ffn_conv_b[i],
                               state_ffn_conv[i], layer=i, time_major=True)
        act = lax.dynamic_update_slice(act_p, act_s, (mp, 0))
        x = matmul(act, ffn_w_down, layer=i, tn=512, tm=512, tiles=(x,),
                   epilogue=lambda acc, res: res + acc, name="ffn_down")

        e = matmul(p_all[i], ple_w_proj, layer=i, tn=d, tm=256, rows=(ple_norm_e[i],),
                   epilogue=lambda acc, g: acc * lax.rsqrt(jnp.mean(acc * acc, axis=-1, keepdims=True) + EPS) * g,
                   name="ple_proj")
        h3 = rmsnorm_rows(x, norm_ple[i], BF16)
        x = matmul(h3, ple_w_gate, layer=i, tn=512, tm=1024, tiles=(x, e),
                   epilogue=lambda acc, res, ee: res + ee * jax.nn.sigmoid(acc), name="ple_gate")

        outs["ret_p"].append(s_ret_p)
        outs["ret_s"].append(s_ret_s)
        outs["rwkv_p"].append(_rwkv_state_from_pairs(s_rwkv_p))
        outs["rwkv_s"].append(_rwkv_state_from_pairs(s_rwkv_s))
        outs["sh_p"].append(c_rwkv[mp - 1:mp].reshape(bp, RWKV_COLS))
        outs["sh_s"].append(cr_s[:, -1])
        outs["s5r_p"].append(s5r_p.reshape(bp, S5_GROUPS, S5_N))
        outs["s5i_p"].append(s5i_p.reshape(bp, S5_GROUPS, S5_N))
        outs["s5r_s"].append(s5r_s.reshape(nb, S5_GROUPS, S5_N))
        outs["s5i_s"].append(s5i_s.reshape(nb, S5_GROUPS, S5_N))
        outs["conv_p"].append(conv_state_from_taps(taps_p))
        outs["conv_s"].append(conv_state_from_taps(taps_s))

    y_p, y_s = rmsnorm_rows_split(x, norm_final, mp)
    st = lambda k: jnp.stack(outs[k])
    k_p, k_s, v_p, v_s = kv_out
    return (y_p.reshape(bp, tp, d), y_s.reshape(nb, ts, d),
            k_p.reshape(depth, bp, tp, DIFF_HEADS, DIFF_VD), v_p.reshape(depth, bp, tp, DIFF_HEADS, DIFF_VD),
            k_s.reshape(depth, nb, ts, DIFF_HEADS, DIFF_VD), v_s.reshape(depth, nb, ts, DIFF_HEADS, DIFF_VD),
            st("ret_p"), st("ret_s"), st("rwkv_p"), st("rwkv_s"),
            st("sh_p"), st("sh_s"), st("s5r_p"), st("s5i_p"), st("s5r_s"), st("s5i_s"), st("conv_p"), st("conv_s"))
```

```python
import functools
import math

import jax
import jax.numpy as jnp
import numpy as np
from jax import lax
from jax.experimental import pallas as pl
from jax.experimental.pallas import tpu as pltpu

F32 = jnp.float32
BF16 = jnp.bfloat16

D_MODEL = 2048
GROUP_W = 512
RET_HEADS = 4
RET_HD = 128
RET_CHUNK = 128
RET_ROPE_BASE = 10000.0
DIFF_HEADS = 4
DIFF_VD = 128
DIFF_QD = 64
ROPE_THETA = 500000.0
ROPE_DIM = 16
PAGE_SIZE = 128
RWKV_HD = 64
RWKV_HEADS = 8
RWKV_LN_EPS = 64e-5
S5_CH = 16
S5_GROUPS = 32
S5_N = 64
D_FF = 5632
EPS = 1e-6
NEG_INF = -1e30
LOG2_E = math.log2(math.e)

RET_COLS = 4 * GROUP_W
DIFF_COLS = 3 * GROUP_W
RWKV_COLS = 3 * GROUP_W + 64 + 64 + 128
S5_COLS = GROUP_W

LANES = 128
SUBLANES = 8
VMEM_LIMIT_BYTES = 52 * 1024 * 1024


def _cparams(sem, vmem=VMEM_LIMIT_BYTES):
    return pltpu.CompilerParams(dimension_semantics=sem, vmem_limit_bytes=vmem)


def _split_bf16(x):
    hi = x.astype(BF16)
    lo = (x - hi.astype(F32)).astype(BF16)
    return hi, lo


def _dot_hilo(x, w_bf16):
    hi, lo = _split_bf16(x)
    return (jnp.dot(hi, w_bf16, preferred_element_type=F32)
            + jnp.dot(lo, w_bf16, preferred_element_type=F32))


def _dot3(x, w_hi, w_lo):
    hi, lo = _split_bf16(x)
    return (jnp.dot(hi, w_hi, preferred_element_type=F32)
            + jnp.dot(hi, w_lo, preferred_element_type=F32)
            + jnp.dot(lo, w_hi, preferred_element_type=F32))


def _rmsnorm_kernel(x_ref, g_ref, o_ref):
    x = x_ref[...]
    y = x * lax.rsqrt(jnp.mean(x * x, axis=-1, keepdims=True) + EPS)
    o_ref[...] = (y * g_ref[...]).astype(o_ref.dtype)


def rmsnorm_rows(x, g, out_dtype, tm=512):
    m, d = x.shape
    return pl.pallas_call(
        _rmsnorm_kernel,
        grid=(m // tm,),
        in_specs=[pl.BlockSpec((tm, d), lambda i: (i, 0)),
                  pl.BlockSpec((1, d), lambda i: (0, 0))],
        out_specs=pl.BlockSpec((tm, d), lambda i: (i, 0)),
        out_shape=jax.ShapeDtypeStruct((m, d), out_dtype),
        compiler_params=_cparams(("parallel",)),
        name="rmsnorm_rows",
    )(x, g.reshape(1, d))


def _rmsnorm_split_kernel(x_ref, g_ref, op_ref, os_ref, *, n_prompt_tiles):
    x = x_ref[...]
    y = (x * lax.rsqrt(jnp.mean(x * x, axis=-1, keepdims=True) + EPS)) * g_ref[...]

    @pl.when(pl.program_id(0) < n_prompt_tiles)
    def _():
        op_ref[...] = y

    @pl.when(pl.program_id(0) >= n_prompt_tiles)
    def _():
        os_ref[...] = y


def rmsnorm_rows_split(x, g, mp, tm=512):
    m, d = x.shape
    ms = m - mp
    assert mp % tm == 0 and ms % tm == 0
    npt = mp // tm
    return pl.pallas_call(
        functools.partial(_rmsnorm_split_kernel, n_prompt_tiles=npt),
        grid=(m // tm,),
        in_specs=[pl.BlockSpec((tm, d), lambda i: (i, 0)),
                  pl.BlockSpec((1, d), lambda i: (0, 0))],
        out_specs=[pl.BlockSpec((tm, d), lambda i: (jnp.minimum(i, npt - 1), 0)),
                   pl.BlockSpec((tm, d), lambda i: (jnp.maximum(i - npt, 0), 0))],
        out_shape=[jax.ShapeDtypeStruct((mp, d), F32), jax.ShapeDtypeStruct((ms, d), F32)],
        compiler_params=_cparams(("arbitrary",)),
        name="rmsnorm_final",
    )(x, g.reshape(1, d))


def _mm_kernel(x_ref, w_ref, *rest, epilogue, n_extra):
    extra = rest[:n_extra]
    o_ref, wb_ref = rest[n_extra], rest[n_extra + 1]

    @pl.when(pl.program_id(1) == 0)
    def _():
        wb_ref[...] = w_ref[...].astype(BF16)

    acc = jnp.dot(x_ref[...].astype(BF16), wb_ref[...], preferred_element_type=F32)
    if epilogue is not None:
        acc = epilogue(acc, *[e[...] for e in extra])
    o_ref[...] = acc.astype(o_ref.dtype)


def _weight_spec(w, layer, k, tn, joff=0):
    if w.ndim == 3:
        return pl.BlockSpec((None, k, tn), lambda j, i: (layer, 0, j + joff))
    return pl.BlockSpec((k, tn), lambda j, i: (0, j + joff))


def matmul(x, w, *, tn, tm=512, col0=0, ncols=None, epilogue=None, tiles=(), rows=(),
           out_dtype=F32, layer=None, name="matmul"):
    m, k = x.shape
    n = w.shape[-1] - col0 if ncols is None else ncols
    assert col0 % tn == 0 and n % tn == 0 and m % tm == 0
    joff = col0 // tn
    in_specs = [pl.BlockSpec((tm, k), lambda j, i: (i, 0)), _weight_spec(w, layer, k, tn, joff)]
    in_specs += [pl.BlockSpec((tm, tn), lambda j, i: (i, j)) for _ in tiles]
    in_specs += [pl.BlockSpec((1, tn), lambda j, i: (0, j)) for _ in rows]
    kern = functools.partial(_mm_kernel, epilogue=epilogue, n_extra=len(tiles) + len(rows))
    return pl.pallas_call(
        kern,
        grid=(n // tn, m // tm),
        in_specs=in_specs,
        out_specs=pl.BlockSpec((tm, tn), lambda j, i: (i, j)),
        out_shape=jax.ShapeDtypeStruct((m, n), out_dtype),
        scratch_shapes=[pltpu.VMEM((k, tn), BF16)],
        compiler_params=_cparams(("parallel", "arbitrary")),
        name=name,
    )(x, w, *tiles, *[r.reshape(1, -1) for r in rows])


def _mm_split_kernel(*refs, n_parts, part_k, n_prompt_tiles, epilogue, n_extra):
    xp = refs[:n_parts]
    xs = refs[n_parts:2 * n_parts]
    w_ref = refs[2 * n_parts]
    extra = refs[2 * n_parts + 1:2 * n_parts + 1 + n_extra]
    o_ref, wb_ref = refs[2 * n_parts + 1 + n_extra], refs[2 * n_parts + 2 + n_extra]
    i = pl.program_id(1)

    @pl.when(i == 0)
    def _():
        wb_ref[...] = w_ref[...].astype(BF16)

    def body(parts):
        acc = None
        for g, r in enumerate(parts):
            d = jnp.dot(r[...], wb_ref[g * part_k:(g + 1) * part_k, :], preferred_element_type=F32)
            acc = d if acc is None else acc + d
        o_ref[...] = epilogue(acc, *[e[...] for e in extra]).astype(o_ref.dtype)

    @pl.when(i < n_prompt_tiles)
    def _():
        body(xp)

    @pl.when(i >= n_prompt_tiles)
    def _():
        body(xs)


def matmul_split(xs_prompt, xs_sample, w, *, tn, tm, epilogue, tiles=(), out_dtype=F32, layer=None,
                 name="matmul_split"):
    n_parts = len(xs_prompt)
    mp, part_k = xs_prompt[0].shape
    ms = xs_sample[0].shape[0]
    k, n = w.shape[-2:]
    assert k == n_parts * part_k and mp % tm == 0 and ms % tm == 0 and n % tn == 0
    npt, nst = mp // tm, ms // tm
    in_specs = [pl.BlockSpec((tm, part_k), lambda j, i: (jnp.minimum(i, npt - 1), 0)) for _ in xs_prompt]
    in_specs += [pl.BlockSpec((tm, part_k), lambda j, i: (jnp.maximum(i - npt, 0), 0)) for _ in xs_sample]
    in_specs += [_weight_spec(w, layer, k, tn)]
    in_specs += [pl.BlockSpec((tm, tn), lambda j, i: (i, j)) for _ in tiles]
    kern = functools.partial(_mm_split_kernel, n_parts=n_parts, part_k=part_k, n_prompt_tiles=npt,
                             epilogue=epilogue, n_extra=len(tiles))
    return pl.pallas_call(
        kern,
        grid=(n // tn, npt + nst),
        in_specs=in_specs,
        out_specs=pl.BlockSpec((tm, tn), lambda j, i: (i, j)),
        out_shape=jax.ShapeDtypeStruct((mp + ms, n), out_dtype),
        scratch_shapes=[pltpu.VMEM((k, tn), BF16)],
        compiler_params=_cparams(("parallel", "arbitrary")),
        name=name,
    )(*xs_prompt, *xs_sample, w, *tiles)


def _ret_kernel(q_ref, k_ref, v_ref, g_ref, cos_ref, sin_ref, dmask_ref, qdec_ref, kdec_ref,
                cdec_ref, nw_ref, nb_ref, s0_ref, o_ref, sn_ref, s_scr, *, chunk, n_chunks):
    c = pl.program_id(1)

    @pl.when(c == 0)
    def _():
        s_scr[...] = s0_ref[...]

    cos = cos_ref[...]
    sin = sin_ref[...]
    for h in range(RET_HEADS):
        hs = slice(h * RET_HD, (h + 1) * RET_HD)
        qh = q_ref[:, hs]
        kh = k_ref[:, hs]
        qr = qh * cos + pltpu.roll(qh, RET_HD // 2, axis=1) * sin
        kr = (kh * cos + pltpu.roll(kh, RET_HD // 2, axis=1) * sin) * (RET_HD ** -0.5)
        vh = v_ref[:, hs]
        s = s_scr[h]
        outs = []
        for ci in range(n_chunks):
            rs = slice(ci * chunk, (ci + 1) * chunk)
            qc, kc, vc = qr[rs], kr[rs], vh[rs].astype(BF16)
            att = lax.dot_general(qc.astype(BF16), kc.astype(BF16), (((1,), (1,)), ((), ())),
                                  preferred_element_type=F32) * dmask_ref[h]
            o = jnp.dot(att.astype(BF16), vc, preferred_element_type=F32)
            o += jnp.dot((qc * qdec_ref[h]).astype(BF16), s.astype(BF16), preferred_element_type=F32)
            kd = (kc * kdec_ref[h]).astype(BF16)
            s = s * cdec_ref[h, 0:1, :] + lax.dot_general(kd, vc, (((0,), (0,)), ((), ())),
                                                  preferred_element_type=F32)
            outs.append(o)
        s_scr[h] = s
        o = outs[0] if n_chunks == 1 else jnp.concatenate(outs, axis=0)
        mu = jnp.mean(o, axis=-1, keepdims=True)
        var = jnp.mean(jnp.square(o - mu), axis=-1, keepdims=True)
        o = (o - mu) * lax.rsqrt(var + EPS) * nw_ref[h:h + 1, :] + nb_ref[h:h + 1, :]
        gh = g_ref[:, hs]
        o_ref[:, hs] = (o * (gh * jax.nn.sigmoid(gh))).astype(o_ref.dtype)

    @pl.when(c == pl.num_programs(1) - 1)
    def _():
        sn_ref[...] = s_scr[...]


def retention(proj, row0, n_seq, t_seq, cos, sin, s0, norm_w, norm_b, *, chunk, chunks_per_step, layer=None):
    rb = chunk * chunks_per_step
    steps = t_seq // rb
    assert row0 % rb == 0 and t_seq % rb == 0
    b0 = row0 // rb
    log_g = jnp.log1p(-jnp.exp2(-5.0 - jnp.arange(RET_HEADS, dtype=F32)))
    idx = jnp.arange(chunk, dtype=F32)
    rel = idx[:, None] - idx[None, :]
    dmask = jnp.where(rel >= 0, jnp.exp(log_g[:, None, None] * jnp.maximum(rel, 0.0)), 0.0)
    ones = jnp.ones((1, 1, RET_HD), F32)
    qdec = jnp.exp(log_g[:, None] * (idx + 1.0))[:, :, None] * ones
    kdec = jnp.exp(log_g[:, None] * (chunk - 1.0 - idx))[:, :, None] * ones
    cdec = jnp.exp(log_g * chunk)[:, None, None] * jnp.ones((1, SUBLANES, RET_HD), F32)

    def col(j):
        return pl.BlockSpec((rb, GROUP_W), lambda s, c: (b0 + s * steps + c, j))

    full = lambda shape: pl.BlockSpec(shape, lambda s, c: (0,) * len(shape))
    if s0.ndim == 5:
        s0_spec = pl.BlockSpec((None, None, RET_HEADS, RET_HD, RET_HD), lambda s, c: (layer, s, 0, 0, 0))
    else:
        s0_spec = pl.BlockSpec((None, RET_HEADS, RET_HD, RET_HD), lambda s, c: (s, 0, 0, 0))
    kern = functools.partial(_ret_kernel, chunk=chunk, n_chunks=chunks_per_step)
    return pl.pallas_call(
        kern,
        grid=(n_seq, steps),
        in_specs=[col(0), col(1), col(2), col(3),
                  pl.BlockSpec((rb, RET_HD), lambda s, c: (c, 0)),
                  pl.BlockSpec((rb, RET_HD), lambda s, c: (c, 0)),
                  full((RET_HEADS, chunk, chunk)),
                  full((RET_HEADS, chunk, RET_HD)),
                  full((RET_HEADS, chunk, RET_HD)),
                  full((RET_HEADS, SUBLANES, RET_HD)),
                  full((RET_HEADS, RET_HD)),
                  full((RET_HEADS, RET_HD)),
                  s0_spec],
        out_specs=[pl.BlockSpec((rb, GROUP_W), lambda s, c: (s * steps + c, 0)),
                   pl.BlockSpec((None, RET_HEADS, RET_HD, RET_HD), lambda s, c: (s, 0, 0, 0))],
        out_shape=[jax.ShapeDtypeStruct((n_seq * t_seq, GROUP_W), BF16),
                   jax.ShapeDtypeStruct((n_seq, RET_HEADS, RET_HD, RET_HD), F32)],
        scratch_shapes=[pltpu.VMEM((RET_HEADS, RET_HD, RET_HD), F32)],
        compiler_params=_cparams(("arbitrary", "arbitrary")),
        name="retention",
    )(proj, proj, proj, proj, cos, sin, dmask, qdec, kdec, cdec, norm_w, norm_b, s0)


def _diff_prep_kernel(q_ref, k_ref, v_ref, c_ref, s1_ref, s2_ref, *rest, n_prompt_tiles, n_alias):
    q0_ref, q1_ref, kn_ref, kb_ref, vt_ref, kp_ref, ks_ref, vp_ref, vs_ref = rest[n_alias:]
    i = pl.program_id(0)
    tm = q_ref.shape[0]
    c, s1, s2 = c_ref[...], s1_ref[...], s2_ref[...]
    lane = lax.broadcasted_iota(jnp.int32, c.shape, 1)
    lo = lane < DIFF_QD
    krs = []
    for j in range(GROUP_W // LANES):
        cs = slice(j * LANES, (j + 1) * LANES)
        q = q_ref[:, cs]
        k = k_ref[:, cs]
        qr = q * c + pltpu.roll(q, LANES - ROPE_DIM // 2, axis=1) * s1 + pltpu.roll(q, ROPE_DIM // 2, axis=1) * s2
        kr = k * c + pltpu.roll(k, LANES - ROPE_DIM // 2, axis=1) * s1 + pltpu.roll(k, ROPE_DIM // 2, axis=1) * s2
        qr = qr * (DIFF_QD ** -0.5 * LOG2_E)
        q0_ref[:, cs] = jnp.where(lo, qr, 0.0).astype(BF16)
        q1_ref[:, cs] = jnp.where(lo, 0.0, qr).astype(BF16)
        kn_ref[:, cs] = kr
        kb_ref[:, cs] = kr.astype(BF16)
        krs.append(kr)
    v = v_ref[...]
    vt_ref[...] = v.T.astype(BF16)

    def emit(k_out, v_out):
        for j in range(DIFF_HEADS):
            k_out[pl.ds(j, tm, stride=DIFF_HEADS), :] = krs[j]
            v_out[pl.ds(j, tm, stride=DIFF_HEADS), :] = v[:, j * DIFF_VD:(j + 1) * DIFF_VD]

    @pl.when(i < n_prompt_tiles)
    def _():
        emit(kp_ref, vp_ref)

    @pl.when(i >= n_prompt_tiles)
    def _():
        emit(ks_ref, vs_ref)


def diff_prep(proj, c, s1, s2, *, layer, depth, mp, kv_prev=None, tm=512):
    m = proj.shape[0]
    ms = m - mp
    assert mp % tm == 0 and ms % tm == 0
    npt = mp // tm
    col = lambda j: pl.BlockSpec((tm, GROUP_W), lambda i: (i, j))
    tab = pl.BlockSpec((tm, LANES), lambda i: (i, 0))
    out = pl.BlockSpec((tm, GROUP_W), lambda i: (i, 0))
    kv_p = pl.BlockSpec((None, tm * DIFF_HEADS, DIFF_VD), lambda i: (layer, jnp.minimum(i, npt - 1), 0))
    kv_s = pl.BlockSpec((None, tm * DIFF_HEADS, DIFF_VD), lambda i: (layer, jnp.maximum(i - npt, 0), 0))
    shape_p = jax.ShapeDtypeStruct((depth, mp * DIFF_HEADS, DIFF_VD), F32)
    shape_s = jax.ShapeDtypeStruct((depth, ms * DIFF_HEADS, DIFF_VD), F32)
    prev = () if kv_prev is None else tuple(kv_prev)
    n_alias = len(prev)
    kern = functools.partial(_diff_prep_kernel, n_prompt_tiles=npt, n_alias=n_alias)
    outs = pl.pallas_call(
        kern,
        grid=(m // tm,),
        in_specs=[col(0), col(1), col(2), tab, tab, tab] + [pl.BlockSpec(memory_space=pl.ANY)] * n_alias,
        out_specs=[out] * 4 + [pl.BlockSpec((GROUP_W, tm), lambda i: (0, i)), kv_p, kv_s, kv_p, kv_s],
        out_shape=[jax.ShapeDtypeStruct((m, GROUP_W), BF16), jax.ShapeDtypeStruct((m, GROUP_W), BF16),
                   jax.ShapeDtypeStruct((m, GROUP_W), F32), jax.ShapeDtypeStruct((m, GROUP_W), BF16),
                   jax.ShapeDtypeStruct((GROUP_W, m), BF16), shape_p, shape_s, shape_p, shape_s],
        input_output_aliases={6 + a: 5 + a for a in range(n_alias)},
        compiler_params=_cparams(("arbitrary",)),
        name="diff_prep",
    )(proj, proj, proj, c, s1, s2, *prev)
    return outs[:5], outs[5:]


def _diff_flash_kernel(qi_ref, ki_ref, lam_ref, q0_ref, q1_ref, k_ref, vt_ref, sub_ref, o_ref,
                       m0, l0, a0, m1, l1, a1, *, bq, bk, out_scale):
    step = pl.program_id(1)
    qi = qi_ref[step]
    ki = ki_ref[step]
    ratio = bq // bk

    @pl.when(ki == 0)
    def _():
        m0[...] = jnp.full(m0.shape, NEG_INF, F32)
        m1[...] = jnp.full(m1.shape, NEG_INF, F32)
        l0[...] = jnp.zeros(l0.shape, F32)
        l1[...] = jnp.zeros(l1.shape, F32)
        a0[...] = jnp.zeros(a0.shape, F32)
        a1[...] = jnp.zeros(a1.shape, F32)

    def update(q_ref, m_ref, l_ref, a_ref, lo, diagonal):
        st = lax.dot_general(k_ref[...], q_ref[lo:, :], (((1,), (1,)), ((), ())), preferred_element_type=F32)
        if diagonal:
            kpos = lax.broadcasted_iota(jnp.int32, (bk, bq - lo), 0)
            qpos = lax.broadcasted_iota(jnp.int32, (bk, bq - lo), 1)
            st = jnp.where(kpos <= qpos, st, NEG_INF)
        m_prev = m_ref[:, lo:]
        m_new = jnp.maximum(m_prev, jnp.max(st, axis=0, keepdims=True))
        alpha = jnp.exp2(m_prev - m_new)
        pt = jnp.exp2(st - m_new)
        l_ref[:, lo:] = alpha * l_ref[:, lo:] + jnp.sum(pt, axis=0, keepdims=True)
        a_ref[:, lo:] = alpha * a_ref[:, lo:] + jnp.dot(vt_ref[...], pt.astype(BF16), preferred_element_type=F32)
        m_ref[:, lo:] = m_new

    @pl.when(ki < ratio * qi)
    def _():
        update(q0_ref, m0, l0, a0, 0, False)
        update(q1_ref, m1, l1, a1, 0, False)

    for d in range(ratio):
        @pl.when(ki == ratio * qi + d)
        def _(d=d):
            update(q0_ref, m0, l0, a0, d * bk, True)
            update(q1_ref, m1, l1, a1, d * bk, True)

    @pl.when(ki == ratio * qi + ratio - 1)
    def _():
        ot = a0[...] / l0[...] - lam_ref[0, 0] * (a1[...] / l1[...])
        yt = ot * lax.rsqrt(jnp.mean(ot * ot, axis=0, keepdims=True) + EPS) * (sub_ref[...] * out_scale)
        o_ref[...] = yt.T.astype(o_ref.dtype)


def diff_attention_prompt(q0, q1, kb, vt, lam, subln, lam_init, *, t, bq=1024, bk=512):
    assert bq % bk == 0 and t % bq == 0
    nq, ratio = t // bq, bq // bk
    qi_tbl = np.concatenate([np.full(ratio * (i + 1), i, np.int32) for i in range(nq)])
    ki_tbl = np.concatenate([np.arange(ratio * (i + 1), dtype=np.int32) for i in range(nq)])
    kern = functools.partial(_diff_flash_kernel, bq=bq, bk=bk, out_scale=1.0 - lam_init)
    grid_spec = pltpu.PrefetchScalarGridSpec(
        num_scalar_prefetch=2,
        grid=(DIFF_HEADS, len(qi_tbl)),
        in_specs=[pl.BlockSpec(memory_space=pltpu.SMEM),
                  pl.BlockSpec((bq, LANES), lambda h, s, qi, ki: (qi[s], h)),
                  pl.BlockSpec((bq, LANES), lambda h, s, qi, ki: (qi[s], h)),
                  pl.BlockSpec((bk, LANES), lambda h, s, qi, ki: (ki[s], h)),
                  pl.BlockSpec((LANES, bk), lambda h, s, qi, ki: (h, ki[s])),
                  pl.BlockSpec((DIFF_VD, 1), lambda h, s, qi, ki: (0, 0))],
        out_specs=pl.BlockSpec((bq, LANES), lambda h, s, qi, ki: (qi[s], h)),
        scratch_shapes=[pltpu.VMEM((1, bq), F32), pltpu.VMEM((1, bq), F32), pltpu.VMEM((DIFF_VD, bq), F32),
                        pltpu.VMEM((1, bq), F32), pltpu.VMEM((1, bq), F32), pltpu.VMEM((DIFF_VD, bq), F32)],
    )
    return pl.pallas_call(
        kern,
        grid_spec=grid_spec,
        out_shape=jax.ShapeDtypeStruct((t, GROUP_W), BF16),
        compiler_params=_cparams(("arbitrary", "arbitrary")),
        name="diff_attention_prompt",
    )(jnp.asarray(qi_tbl), jnp.asarray(ki_tbl), lam.reshape(1, 1), q0, q1, kb, vt, subln.reshape(DIFF_VD, 1))


def _diff_paged_kernel(pt_ref, lam_ref, q_ref, kn_ref, vn_ref, sub_ref, sel_ref, qmask_ref, *rest,
                       n_pages, t_new, out_scale):
    k_pages = rest[:n_pages]
    v_pages = rest[n_pages:2 * n_pages]
    o_ref = rest[2 * n_pages]
    kpad, vpad, qpad, s_scr = rest[2 * n_pages + 1:]
    half = DIFF_HEADS * t_new
    b = pl.program_id(0)

    @pl.when(b == 0)
    def _():
        kpad[...] = jnp.zeros(kpad.shape, kpad.dtype)
        vpad[...] = jnp.zeros(vpad.shape, vpad.dtype)
        qpad[...] = jnp.zeros(qpad.shape, qpad.dtype)

    kpad[0:t_new, :] = kn_ref[...]
    vpad[0:t_new, :] = vn_ref[...]
    qpad[0:t_new, :] = q_ref[...]
    qbd = (lax.dot_general(qpad[...].astype(BF16), sel_ref[...], (((0,), (0,)), ((), ())),
                           preferred_element_type=F32) * qmask_ref[...]).astype(BF16)

    def heads_to_lanes(ref):
        return jnp.concatenate([ref[pl.ds(h, PAGE_SIZE, stride=DIFF_HEADS), :] for h in range(DIFF_HEADS)],
                               axis=1).astype(BF16)

    key =lax.broadcasted_iota(jnp.int32, (PAGE_SIZE, LANES), 0)
    qpos = lax.broadcasted_iota(jnp.int32, (PAGE_SIZE, LANES), 1) & (t_new - 1)
    m = jnp.full((1, LANES), NEG_INF, F32)
    for p in range(n_pages + 1):
        if p < n_pages:
            s = jnp.dot(heads_to_lanes(k_pages[p]), qbd, preferred_element_type=F32)
        else:
            s = jnp.dot(kpad[...].astype(BF16), qbd, preferred_element_type=F32)
            s = jnp.where(key <= qpos, s, NEG_INF)
        s_scr[p] = s
        m = jnp.maximum(m, jnp.max(s, axis=0, keepdims=True))
    l = jnp.zeros((1, LANES), F32)
    acc = jnp.zeros((LANES, GROUP_W), F32)
    for p in range(n_pages + 1):
        e = jnp.exp2(s_scr[p] - m)
        l = l + jnp.sum(e, axis=0, keepdims=True)
        vsrc = heads_to_lanes(v_pages[p]) if p < n_pages else vpad[...].astype(BF16)
        acc += lax.dot_general(e.astype(BF16), vsrc, (((0,), (0,)), ((), ())), preferred_element_type=F32)
    inv_col = jnp.broadcast_to(1.0 / l, (LANES, LANES)).T
    lam = lam_ref[0, 0]
    outs = []
    for h in range(DIFF_HEADS):
        r0, r1 = h * t_new, half + h * t_new
        cs = slice(h * DIFF_VD, (h + 1) * DIFF_VD)
        o = acc[r0:r0 + t_new, cs] * inv_col[r0:r0 + t_new, :] - lam * (acc[r1:r1 + t_new, cs] * inv_col[r1:r1 + t_new, :])
        y = o * lax.rsqrt(jnp.mean(o * o, axis=-1, keepdims=True) + EPS)
        outs.append(y * sub_ref[...] * out_scale)
    o_ref[...] = jnp.concatenate(outs, axis=1).astype(o_ref.dtype)


def diff_attention_sample(qs, kn, proj, row0, cache_k, cache_v, layer, page_table, lam, subln, lam_init):
    nb, t_new, _ = qs.shape
    assert row0 % t_new == 0
    b0 = row0 // t_new
    n_pages = page_table.shape[1]
    rows = PAGE_SIZE * DIFF_HEADS
    assert t_new & (t_new - 1) == 0
    half = DIFF_HEADS * t_new
    col = np.arange(LANES)
    valid = col < 2 * half
    sel = ((np.arange(PAGE_SIZE)[:, None] == (col % t_new)[None, :]) & valid[None, :])
    blk_of_col = 2 * ((col % half) // t_new) + col // half
    qmask = ((np.arange(GROUP_W)[:, None] // DIFF_QD) == blk_of_col[None, :]) & valid[None, :]
    sel = jnp.asarray(sel, BF16)
    qmask = jnp.asarray(qmask, F32)

    def page_spec(p):
        return pl.BlockSpec((None, None, rows, DIFF_VD), lambda b, pt: (layer, pt[b, p], 0, 0))

    kern = functools.partial(_diff_paged_kernel, n_pages=n_pages, t_new=t_new, out_scale=1.0 - lam_init)
    grid_spec = pltpu.PrefetchScalarGridSpec(
        num_scalar_prefetch=1,
        grid=(nb,),
        in_specs=[pl.BlockSpec(memory_space=pltpu.SMEM),
                  pl.BlockSpec((t_new, GROUP_W), lambda b, pt: (b, 0)),
                  pl.BlockSpec((t_new, GROUP_W), lambda b, pt: (b0 + b, 0)),
                  pl.BlockSpec((t_new, GROUP_W), lambda b, pt: (b0 + b, 2)),
                  pl.BlockSpec((1, LANES), lambda b, pt: (0, 0)),
                  pl.BlockSpec((PAGE_SIZE, LANES), lambda b, pt: (0, 0)),
                  pl.BlockSpec((GROUP_W, LANES), lambda b, pt: (0, 0))]
                 + [page_spec(p) for p in range(n_pages)] * 2,
        out_specs=pl.BlockSpec((t_new, GROUP_W), lambda b, pt: (b, 0)),
        scratch_shapes=[pltpu.VMEM((PAGE_SIZE, GROUP_W), F32), pltpu.VMEM((PAGE_SIZE, GROUP_W), F32),
                        pltpu.VMEM((PAGE_SIZE, GROUP_W), F32),
                        pltpu.VMEM((n_pages + 1, PAGE_SIZE, LANES), F32)],
    )
    return pl.pallas_call(
        kern,
        grid_spec=grid_spec,
        out_shape=jax.ShapeDtypeStruct((nb * t_new, GROUP_W), BF16),
        compiler_params=_cparams(("arbitrary",)),
        name="diff_attention_sample",
    )(page_table, lam.reshape(1, 1), qs.reshape(nb * t_new, GROUP_W), kn, proj, subln.reshape(1, LANES), sel, qmask,
      *([cache_k] * n_pages), *([cache_v] * n_pages))


def _interleave64(x, y):
    lane = lax.broadcasted_iota(jnp.int32, (x.shape[0], LANES), 1)
    lo = lane < RWKV_HD
    blocks = []
    for c in range(GROUP_W // LANES):
        xc = x[:, c * LANES:(c + 1) * LANES]
        yc = y[:, c * LANES:(c + 1) * LANES]
        rx = pltpu.roll(xc, RWKV_HD, axis=1)
        ry = pltpu.roll(yc, RWKV_HD, axis=1)
        blocks.append(jnp.where(lo, xc, ry))
        blocks.append(jnp.where(lo, rx, yc))
    return jnp.concatenate(blocks, axis=1)


def _rwkv_prep_kernel(cols_ref, prev_ref, mu_ref, w0_ref, a0_ref, kk_ref, ka_ref, rk_ref,
                      w2h_ref, w2l_ref, a2h_ref, a2l_ref, g2h_ref, g2l_ref, e_ref,
                      ar_ref, bt_ref, kt_ref, g8_ref, v_ref, br_ref, kr_ref, bon_ref, g_ref, *, shifted):
    cols = cols_ref[...]
    if shifted:
        before = jnp.where(pl.program_id(0) == 0, 0.0, prev_ref[SUBLANES - 1:SUBLANES, :])
        first = lax.broadcasted_iota(jnp.int32, cols.shape, 0) == 0
        prev = jnp.where(first, before, pltpu.roll(cols, 1, axis=0))
    else:
        prev = prev_ref[...]
    xm = cols + (prev - cols) * mu_ref[...]
    o1 = GROUP_W
    r, k, v = xm[:, 0:o1], xm[:, o1:2 * o1], xm[:, 2 * o1:3 * o1]
    lora = xm[:, 3 * o1:3 * o1 + LANES]
    gl = xm[:, 3 * o1 + LANES:3 * o1 + 2 * LANES]
    wterm = _dot3(jnp.tanh(lora), w2h_ref[...], w2l_ref[...])
    aterm = _dot3(lora, a2h_ref[...], a2l_ref[...])
    z = -(w0_ref[...] + wterm)
    softplus = jnp.maximum(z, 0.0) + jnp.log1p(jnp.exp(-jnp.abs(z)))
    w = -softplus - 0.5
    log_decay = -jnp.exp(w)
    a = jax.nn.sigmoid(a0_ref[...] + aterm)
    g = _dot3(jax.nn.sigmoid(gl), g2h_ref[...], g2l_ref[...])
    e = e_ref[...]
    kk = k * kk_ref[...]
    kk = kk * lax.rsqrt(jnp.maximum(_dot_hilo(kk * kk, e), 1e-24))
    k2 = k * (1.0 + (a - 1.0) * ka_ref[...])
    bv = kk * a
    n = cols.shape[0]
    sub = lax.broadcasted_iota(jnp.int32, log_decay.shape, 0) & (SUBLANES - 1)
    csum = log_decay
    rsum = log_decay
    for d in (1, 2, 4):
        csum = csum + jnp.where(sub >= d, pltpu.roll(csum, d, axis=0), 0.0)
        rsum = rsum + jnp.where(sub < SUBLANES - d, pltpu.roll(rsum, n - d, axis=0), 0.0)
    gamma = jnp.exp(csum)
    inv_gamma = jnp.exp(-csum)
    rnd = lambda t: t.astype(BF16).astype(F32)
    ar_ref[...] = _interleave64(rnd(jnp.exp(csum - log_decay) * (-kk)), rnd(r * gamma))
    bt_ref[...] = rnd(bv * inv_gamma)
    kt_ref[...] = rnd(k2 * inv_gamma)
    g8_ref[...] = jnp.exp(csum + rsum - log_decay)
    v_ref[...] = v
    br_ref[...] = _dot_hilo(bv * r, e)
    kr_ref[...] = _dot_hilo(k2 * r, e)
    bon_ref[...] = _dot_hilo(r * k2 * rk_ref[...], e)
    g_ref[...] = g


def rwkv_prep(cols, prev, p, e512, *, row0, m, tm=256):
    assert row0 % tm == 0 and m % tm == 0
    b0 = row0 // tm
    shifted = prev is None
    if shifted:
        prev_arr = cols
        per8 = tm // SUBLANES
        prev_spec = pl.BlockSpec((SUBLANES, RWKV_COLS), lambda i: (jnp.maximum((b0 + i) * per8 - 1, 0), 0))
    else:
        prev_arr = prev
        prev_spec = pl.BlockSpec((tm, RWKV_COLS), lambda i: (i, 0))
    row = lambda n: pl.BlockSpec((1, n), lambda i: (0, 0))
    full = lambda a: pl.BlockSpec(a.shape, lambda i: (0, 0))
    wide = pl.BlockSpec((tm, 2 * GROUP_W), lambda i: (i, 0))
    nar = pl.BlockSpec((tm, GROUP_W), lambda i: (i, 0))
    z64 = jnp.zeros((64, GROUP_W), F32)
    w2p = jnp.concatenate([p["w2"], z64], axis=0)
    a2p = jnp.concatenate([z64, p["a2"]], axis=0)
    w2h, w2l = _split_bf16(w2p)
    a2h, a2l = _split_bf16(a2p)
    g2h, g2l = _split_bf16(p["g2"])
    mats = [w2h, w2l, a2h, a2l, g2h, g2l, e512]
    return pl.pallas_call(
        functools.partial(_rwkv_prep_kernel, shifted=shifted),
        grid=(m // tm,),
        in_specs=[pl.BlockSpec((tm, RWKV_COLS), lambda i: (b0 + i, 0)),
                  prev_spec,
                  row(RWKV_COLS), row(GROUP_W), row(GROUP_W), row(GROUP_W), row(GROUP_W), row(GROUP_W)]
                 + [full(a) for a in mats],
        out_specs=[wide, nar, nar, nar, nar, nar, nar, nar, nar],
        out_shape=[jax.ShapeDtypeStruct((m, 2 * GROUP_W), F32)]
                  + [jax.ShapeDtypeStruct((m, GROUP_W), F32)] * 8,
        compiler_params=_cparams(("parallel",)),
        name="rwkv_prep",
    )(cols, prev_arr, p["mu"].reshape(1, -1), p["w0"].reshape(1, -1), p["a0"].reshape(1, -1),
      p["kk"].reshape(1, -1), p["ka"].reshape(1, -1), p["rk"].reshape(1, -1), *mats)


def _rwkv_scan_kernel(ar_ref, bt_ref, kt_ref, g8_ref, v_ref, br_ref, kr_ref, bon_ref, g_ref, lnw_ref, lnb_ref,
                      e_ref, s0_ref, o_ref, sn_ref, s_scr, y_scr, *, n_seq, t_len, pairs):
    c = pl.program_id(2)
    half = RWKV_HD
    lane_lo = lax.broadcasted_iota(jnp.int32, (half, LANES), 1) < half
    diag = (lax.broadcasted_iota(jnp.int32, (half, LANES), 1) & (half - 1)) == \
        lax.broadcasted_iota(jnp.int32, (half, LANES), 0)
    ones_blocks = e_ref[...]

    def tokens8(g, states):
        r0 = pl.multiple_of(g * SUBLANES, SUBLANES)
        rows = pl.ds(r0, SUBLANES)
        ar8, bt8, kt8, g8 = ar_ref[rows, :], bt_ref[rows, :], kt_ref[rows, :], g8_ref[rows, :]
        v8, br8, kr8 = v_ref[rows, :], br_ref[rows, :], kr_ref[rows, :]
        states = list(states)
        ys = [[] for _ in range(pairs)]

        def by_transpose(tile, k, pp):
            top = jnp.broadcast_to(tile[k:k + 1, 2 * pp * LANES:(2 * pp + 1) * LANES], (half, LANES))
            bot = jnp.broadcast_to(tile[k:k + 1, (2 * pp + 1) * LANES:(2 * pp + 2) * LANES], (half, LANES))
            t = jnp.concatenate([top, bot], axis=0).astype(BF16).T
            return t[0:half].astype(F32), t[half:2 * half].astype(F32)

        def by_matmul(x8, y8, k, ls):
            zx = jnp.where(diag, jnp.broadcast_to(x8[k:k + 1, ls], (half, LANES)), 0.0)
            zy = jnp.where(diag, jnp.broadcast_to(y8[k:k + 1, ls], (half, LANES)), 0.0)
            t = jnp.dot(jnp.concatenate([zx, zy], axis=0).astype(BF16), ones_blocks, preferred_element_type=F32)
            return t[0:half], t[half:2 * half]

        for k in range(SUBLANES):
            for pp in range(pairs):
                s = states[pp]
                ls = slice(pp * LANES, (pp + 1) * LANES)
                ac, rc = by_transpose(ar8, k, pp)
                bc, kc = by_matmul(bt8, kt8, k, ls)
                u = jnp.sum(s * ac, axis=0, keepdims=True)
                yp = jnp.sum(s * rc, axis=0, keepdims=True)
                vrow = v8[k:k + 1, ls]
                ys[pp].append(yp + u * br8[k:k + 1, ls] + vrow * kr8[k:k + 1, ls])
                states[pp] = s + bc * u + kc * vrow
        for pp in range(pairs):
            ls = slice(pp * LANES, (pp + 1) * LANES)
            y_scr[rows, ls] = jnp.concatenate(ys[pp], axis=0)
            gt = jnp.broadcast_to(g8[0:1, ls], (LANES, LANES)).T
            states[pp] = states[pp] * jnp.where(lane_lo, gt[0:half], gt[half:2 * half])
        return tuple(states)

    @pl.when(c == 0)
    def _():
        s_scr[...] = s0_ref[...]

    groups = t_len // SUBLANES

    def seq(si, carry):
        states = tuple(s_scr[si, pp] for pp in range(pairs))
        states = lax.fori_loop(0, groups, lambda g, st: tokens8(si * groups + g, st), states)
        for pp in range(pairs):
            s_scr[si, pp] = states[pp]
        return carry

    lax.fori_loop(0, n_seq, seq, 0)

    e = e_ref[...]
    for pp in range(pairs):
        ls = slice(pp * LANES, (pp + 1) * LANES)
        y = y_scr[:, ls]
        mu = _dot_hilo(y, e) * (1.0 / RWKV_HD)
        d = y - mu
        var = _dot_hilo(d * d, e) * (1.0 / RWKV_HD)
        yn = d * lax.rsqrt(var + RWKV_LN_EPS) * lnw_ref[:, ls] + lnb_ref[:, ls]
        o_ref[:, ls] = ((yn + bon_ref[:, ls] * v_ref[:, ls]) * g_ref[:, ls]).astype(o_ref.dtype)

    @pl.when(c == pl.num_programs(2) - 1)
    def _():
        sn_ref[...] = s_scr[...]


def rwkv_scan(ar, bt, kt, g8, v, br, kr, bon, g, ln_w, ln_b, e128, s0, *, t_seq, seq_per_step, t_step, pairs=2):
    n_seq = s0.shape[0]
    chunks = t_seq // t_step
    rb = seq_per_step * t_step
    assert seq_per_step == 1 or chunks == 1
    n_pairs = RWKV_HEADS // 2
    assert n_pairs % pairs == 0
    w = pairs * LANES

    def rows(width):
        return pl.BlockSpec((rb, width), lambda p, s, c: (s * chunks + c, p))

    kern = functools.partial(_rwkv_scan_kernel, n_seq=seq_per_step, t_len=t_step, pairs=pairs)
    st = pl.BlockSpec((seq_per_step, pairs, RWKV_HD, LANES), lambda p, s, c: (s, p, 0, 0))
    return pl.pallas_call(
        kern,
        grid=(n_pairs // pairs, n_seq // seq_per_step, chunks),
        in_specs=[rows(2 * w),
                  rows(w), rows(w), rows(w), rows(w), rows(w), rows(w), rows(w), rows(w),
                  pl.BlockSpec((1, w), lambda p, s, c: (0, p)),
                  pl.BlockSpec((1, w), lambda p, s, c: (0, p)),
                  pl.BlockSpec((LANES, LANES), lambda p, s, c: (0, 0)),
                  st],
        out_specs=[rows(w), st],
        out_shape=[jax.ShapeDtypeStruct((n_seq * t_seq, GROUP_W), BF16),
                   jax.ShapeDtypeStruct(s0.shape, F32)],
        scratch_shapes=[pltpu.VMEM((seq_per_step, pairs, RWKV_HD, LANES), F32),
                        pltpu.VMEM((rb, w), F32)],
        compiler_params=_cparams(("arbitrary", "arbitrary", "arbitrary")),
        name="rwkv_scan",
    )(ar, bt, kt, g8, v, br, kr, bon, g, ln_w.reshape(1, -1), ln_b.reshape(1, -1), e128, s0)


def _s5_kernel(u_ref, bb_ref, cc_ref, pw_ref, ad_ref, d_ref, wg_ref, bg_ref, nw_ref, sr0_ref, si0_ref,
               o_ref, srn_ref, sin_ref, bur, bui, cr, ci, wgb, *, tm, per_group_state):
    i = pl.program_id(0)
    nch = S5_GROUPS * S5_N
    ngrp = tm // SUBLANES

    @pl.when(i == 0)
    def _():
        wgb[...] = wg_ref[...].astype(BF16)
        if not per_group_state:
            cr[...] = sr0_ref[...]
            ci[...] = si0_ref[...]

    u = u_ref[...]
    bu = jnp.dot(u.astype(BF16), bb_ref[...], preferred_element_type=F32)
    bur[...] = bu[:, 0:nch]
    bui[...] = bu[:, nch:2 * nch]

    def group(r0, c_r, c_i):
        xr = bur[pl.ds(r0, SUBLANES), :]
        xi = bui[pl.ds(r0, SUBLANES), :]
        for di, dsh in enumerate((1, 2, 4)):
            ar = ad_ref[2 * SUBLANES * di:2 * SUBLANES * di + SUBLANES, :]
            ai = ad_ref[2 * SUBLANES * di + SUBLANES:2 * SUBLANES * (di + 1), :]
            sr = pltpu.roll(xr, dsh, axis=0)
            si = pltpu.roll(xi, dsh, axis=0)
            xr, xi = xr + ar * sr - ai * si, xi + ar * si + ai * sr
        pr = pw_ref[0:SUBLANES, :]
        pi = pw_ref[SUBLANES:2 * SUBLANES, :]
        xr, xi = xr + pr * c_r - pi * c_i, xi + pr * c_i + pi * c_r
        bur[pl.ds(r0, SUBLANES), :] = xr
        bui[pl.ds(r0, SUBLANES), :] = xi
        return xr[SUBLANES - 1:SUBLANES, :], xi[SUBLANES - 1:SUBLANES, :]

    def block(bi, carry):
        g0 = pl.multiple_of(bi * SUBLANES, SUBLANES)
        if per_group_state:
            st_r = sr0_ref[pl.ds(g0, SUBLANES), :]
            st_i = si0_ref[pl.ds(g0, SUBLANES), :]
            lasts_r, lasts_i = [], []
        else:
            c_r, c_i = cr[...], ci[...]
        for k in range(SUBLANES):
            r0 = pl.multiple_of((g0 + k) * SUBLANES, SUBLANES)
            if per_group_state:
                l_r, l_i = group(r0, st_r[k:k + 1, :], st_i[k:k + 1, :])
                lasts_r.append(l_r)
                lasts_i.append(l_i)
            else:
                c_r, c_i = group(r0, c_r, c_i)
        if per_group_state:
            srn_ref[pl.ds(g0, SUBLANES), :] = jnp.concatenate(lasts_r, axis=0)
            sin_ref[pl.ds(g0, SUBLANES), :] = jnp.concatenate(lasts_i, axis=0)
        else:
            cr[...] = c_r
            ci[...] = c_i
        return carry

    lax.fori_loop(0, ngrp // SUBLANES, block, 0)

    if not per_group_state:
        srn_ref[...] = cr[...]
        sin_ref[...] = ci[...]

    ccv = cc_ref[...]
    y = (jnp.dot(bur[...].astype(BF16), ccv[0:nch], preferred_element_type=F32)
         + jnp.dot(bui[...].astype(BF16), ccv[nch:2 * nch], preferred_element_type=F32))
    y = y + d_ref[...] * u
    y = 0.5 * y * (1.0 + jnp.tanh(math.sqrt(2.0 / math.pi) * (y + 0.044715 * (y * y * y))))
    gate = jnp.dot(y.astype(BF16), wgb[...], preferred_element_type=F32) + bg_ref[...]
    y = y * jax.nn.sigmoid(gate)
    y = y * lax.rsqrt(jnp.mean(y * y, axis=-1, keepdims=True) + EPS)
    o_ref[...] = (y * nw_ref[...]).astype(o_ref.dtype)


def s5_mixer(u, p, sr0, si0, *, per_group_state, row0, m, tm=512):
    assert row0 % tm == 0 and m % tm == 0
    b0 = row0 // tm
    nch = S5_GROUPS * S5_N
    lr, li = p["lam_re"], p["lam_im"]
    dt = jnp.exp(p["log_step"])[:, None]
    mag = jnp.exp(lr * dt)
    ab_re, ab_im = mag * jnp.cos(li * dt), mag * jnp.sin(li * dt)
    den = lr * lr + li * li
    cf_re = ((ab_re - 1.0) * lr + ab_im * li) / den
    cf_im = (ab_im * lr - (ab_re - 1.0) * li) / den
    bb_re = cf_re[..., None] * p["b_re"] - cf_im[..., None] * p["b_im"]
    bb_im = cf_re[..., None] * p["b_im"] + cf_im[..., None] * p["b_re"]
    eye = jnp.eye(S5_GROUPS, dtype=F32)

    def bd_in(b):
        return jnp.einsum("gnc,gh->gchn", b, eye).reshape(GROUP_W, nch)

    def bd_out(cm):
        return jnp.einsum("gcn,gh->gnhc", cm, eye).reshape(nch, GROUP_W)

    bb = jnp.concatenate([bd_in(bb_re), bd_in(bb_im)], axis=1).astype(BF16)
    cc = jnp.concatenate([bd_out(p["c_re"]), -bd_out(p["c_im"])], axis=0).astype(BF16)
    ar, ai = ab_re.reshape(1, nch), ab_im.reshape(1, nch)
    pows = [(ar, ai)]
    for _ in range(SUBLANES - 1):
        pr, pi = pows[-1]
        pows.append((pr * ar - pi * ai, pr * ai + pi * ar))
    pw = jnp.concatenate([jnp.concatenate([q[0] for q in pows], axis=0),
                          jnp.concatenate([q[1] for q in pows], axis=0)], axis=0)
    srow = jnp.arange(SUBLANES)[:, None]
    ad = jnp.concatenate([jnp.where(srow >= dsh, pows[dsh - 1][part], 0.0)
                          for dsh in (1, 2, 4) for part in (0, 1)], axis=0)
    n_state = sr0.shape[0]
    full = lambda a: pl.BlockSpec(a.shape, lambda i: (0,) * a.ndim)
    row = lambda n: pl.BlockSpec((1, n), lambda i: (0, 0))
    if per_group_state:
        st = pl.BlockSpec((tm // SUBLANES, nch), lambda i: (i, 0))
    else:
        st = pl.BlockSpec((1, nch), lambda i: (0, 0))
    kern = functools.partial(_s5_kernel, tm=tm, per_group_state=per_group_state)
    return pl.pallas_call(
        kern,
        grid=(m // tm,),
        in_specs=[pl.BlockSpec((tm, GROUP_W), lambda i: (b0 + i, 0)), full(bb), full(cc), full(pw), full(ad),
                  row(GROUP_W), full(p["w_glu"]), row(GROUP_W), row(GROUP_W), st, st],
        out_specs=[pl.BlockSpec((tm, GROUP_W), lambda i: (i, 0)), st, st],
        out_shape=[jax.ShapeDtypeStruct((m, GROUP_W), BF16),
                   jax.ShapeDtypeStruct((n_state, nch), F32), jax.ShapeDtypeStruct((n_state, nch), F32)],
        scratch_shapes=[pltpu.VMEM((tm, nch), F32), pltpu.VMEM((tm, nch), F32),
                        pltpu.VMEM((1, nch), F32), pltpu.VMEM((1, nch), F32),
                        pltpu.VMEM((GROUP_W, GROUP_W), BF16)],
        compiler_params=_cparams(("arbitrary",)),
        name="s5_mixer",
    )(u, bb, cc, pw, ad, p["d"].reshape(1, -1), p["w_glu"], p["b_glu"].reshape(1, -1),
      p["norm"].reshape(1, -1), sr0, si0)


def _ffn_up_kernel(h_ref, wg_ref, wv_ref, cwg_ref, cwv_ref, cbg_ref, cbv_ref, c0ga_ref, c0gb_ref, c0va_ref,
                   c0vb_ref, act_ref, cnga_ref, cngb_ref, cnva_ref, cnvb_ref, wgb, wvb, hg, hv, *, shift, tm, off,
                   t_cols):
    mstep = pl.program_id(1)
    hist = 2 * shift

    @pl.when(mstep == 0)
    def _():
        wgb[...] = wg_ref[...].astype(BF16)
        wvb[...] = wv_ref[...].astype(BF16)
        hg[off - hist:off - shift, :] = c0ga_ref[...]
        hg[off - shift:off, :] = c0gb_ref[...]
        hv[off - hist:off - shift, :] = c0va_ref[...]
        hv[off - shift:off, :] = c0vb_ref[...]

    if t_cols:
        k = h_ref.shape[1] // t_cols
        hb = jnp.concatenate([h_ref[:, t * k:(t + 1) * k] for t in range(t_cols)], axis=0)
    else:
        hb = h_ref[...]
    hg[off:off + tm, :] = jnp.dot(hb, wgb[...], preferred_element_type=F32)
    hv[off:off + tm, :] = jnp.dot(hb, wvb[...], preferred_element_type=F32)

    def conv(hs, cw_ref, cb_ref):
        return (cb_ref[...] + cw_ref[0:1, :] * hs[off - hist:off - hist + tm, :]
                + cw_ref[1:2, :] * hs[off - shift:off - shift + tm, :]
                + cw_ref[2:3, :] * hs[off:off + tm, :])

    gate = conv(hg, cwg_ref, cbg_ref)
    val = conv(hv, cwv_ref, cbv_ref)
    act = (gate * jax.nn.sigmoid(gate) * val).astype(act_ref.dtype)
    if t_cols:
        tn = act.shape[1]
        for t in range(t_cols):
            act_ref[:, t * tn:(t + 1) * tn] = act[t * shift:(t + 1) * shift]
    else:
        act_ref[...] = act
    tail_g = hg[off + tm - hist:off + tm, :]
    tail_v = hv[off + tm - hist:off + tm, :]
    hg[off - hist:off, :] = tail_g
    hv[off - hist:off, :] = tail_v

    @pl.when(mstep == pl.num_programs(1) - 1)
    def _():
        cnga_ref[...] = tail_g[0:shift]
        cngb_ref[...] = tail_g[shift:hist]
        cnva_ref[...] = tail_v[0:shift]
        cnvb_ref[...] = tail_v[shift:hist]


def ffn_up(h, w_up, conv_w, conv_b, c0, *, time_major, row0=0, m=None, tm=None, tn=512, layer=None,
           out_rows=None):
    n_seq = c0.shape[0]
    nj = D_FF // tn
    if time_major:
        _, t_cols, k = h.shape
        shift, steps, tm = n_seq, 1, n_seq * t_cols
        h = h.reshape(n_seq, t_cols * k)
        h_spec = pl.BlockSpec((n_seq, t_cols * k), lambda j, i: (0, 0))
        act_spec = pl.BlockSpec((n_seq, t_cols * tn), lambda j, i: (0, j))
        act_shape = jax.ShapeDtypeStruct((n_seq, nj * t_cols * tn), BF16)
    else:
        k = h.shape[1]
        assert n_seq == 1 and row0 % tm == 0 and m % tm == 0
        shift, steps, b0, t_cols = 1, m // tm, row0 // tm, 0
        h_spec = pl.BlockSpec((tm, k), lambda j, i: (b0 + i, 0))
        act_spec = pl.BlockSpec((tm, tn), lambda j, i: (i, j))
        act_shape = jax.ShapeDtypeStruct((m if out_rows is None else out_rows, D_FF), BF16)
    hist = 2 * shift
    off = max(SUBLANES, hist)
    kern = functools.partial(_ffn_up_kernel, shift=shift, tm=tm, off=off, t_cols=t_cols)
    cw = jnp.concatenate([conv_w, jnp.zeros((SUBLANES - conv_w.shape[0], conv_w.shape[1]), F32)], axis=0)
    cb = conv_b.reshape(1, -1)

    c0 = c0.reshape(n_seq, 4 * D_FF)

    def c0_spec(tap, half):
        return pl.BlockSpec((n_seq, tn), lambda j, i: (0, (2 * tap + half) * nj + j))

    tap_out = pl.BlockSpec((shift, tn), lambda j, i: (0, j))
    tap_shape = jax.ShapeDtypeStruct((shift, D_FF), F32)
    outs = pl.pallas_call(
        kern,
        grid=(nj, steps),
        in_specs=[h_spec,
                  _weight_spec(w_up, layer, k, tn),
                  _weight_spec(w_up, layer, k, tn, nj),
                  pl.BlockSpec((SUBLANES, tn), lambda j, i: (0, j)),
                  pl.BlockSpec((SUBLANES, tn), lambda j, i: (0, j + nj)),
                  pl.BlockSpec((1, tn), lambda j, i: (0, j)),
                  pl.BlockSpec((1, tn), lambda j, i: (0, j + nj)),
                  c0_spec(0, 0), c0_spec(1, 0), c0_spec(0, 1), c0_spec(1, 1)],
        out_specs=[act_spec, tap_out, tap_out, tap_out, tap_out],
        out_shape=[act_shape, tap_shape, tap_shape, tap_shape, tap_shape],
        scratch_shapes=[pltpu.VMEM((k, tn), BF16), pltpu.VMEM((k, tn), BF16),
                        pltpu.VMEM((off + tm, tn), F32), pltpu.VMEM((off + tm, tn), F32)],
        compiler_params=_cparams(("parallel", "arbitrary")),
        name="ffn_up",
    )(h, w_up, w_up, cw, cw, cb, cb, c0, c0, c0, c0)
    act = outs[0]
    if time_major:
        act = jnp.transpose(act.reshape(n_seq, nj, t_cols, tn), (0, 2, 1, 3)).reshape(n_seq * t_cols, D_FF)
    return act, outs[1:]


def conv_state_from_taps(taps):
    ga, gb, va, vb = taps
    return jnp.stack([jnp.concatenate([ga, va], axis=1), jnp.concatenate([gb, vb], axis=1)], axis=1)


def _ret_rope_tables(pos):
    half = RET_HD // 2
    inv = jnp.power(RET_ROPE_BASE, -jnp.arange(half, dtype=F32) / half)
    ang = pos.astype(F32)[:, None] * inv[None, :]
    cos, sin = jnp.cos(ang), jnp.sin(ang)
    return jnp.concatenate([cos, cos], axis=1), jnp.concatenate([-sin, sin], axis=1)


def _diff_rope_tables(pos):
    half = ROPE_DIM // 2
    inv = jnp.power(ROPE_THETA, -jnp.arange(half, dtype=F32) / half)
    ang = pos.astype(F32)[:, None] * inv[None, :]
    cos, sin = jnp.cos(ang), jnp.sin(ang)
    n = pos.shape[0]
    rest = DIFF_QD - ROPE_DIM
    c = jnp.concatenate([cos, cos, jnp.ones((n, rest), F32)], axis=1)
    s1 = jnp.concatenate([-sin, jnp.zeros((n, half + rest), F32)], axis=1)
    s2 = jnp.concatenate([jnp.zeros((n, half), F32), sin, jnp.zeros((n, rest), F32)], axis=1)
    rep = LANES // DIFF_QD
    return jnp.tile(c, (1, rep)), jnp.tile(s1, (1, rep)), jnp.tile(s2, (1, rep))


def _rwkv_state_to_pairs(s):
    b = s.shape[0]
    s = s.reshape(b, RWKV_HEADS // 2, 2, RWKV_HD, RWKV_HD)
    return jnp.transpose(s, (0, 1, 4, 2, 3)).reshape(b, RWKV_HEADS // 2, RWKV_HD, LANES)


def _rwkv_state_from_pairs(s):
    b = s.shape[0]
    s = s.reshape(b, RWKV_HEADS // 2, RWKV_HD, 2, RWKV_HD)
    return jnp.transpose(s, (0, 1, 3, 4, 2)).reshape(b, RWKV_HEADS, RWKV_HD, RWKV_HD)


def kernel(x_prompt, x_sample, p_prompt, p_sample, cache_k, cache_v, page_table, state_ret, state_rwkv, state_rwkv_shift, state_s5_re, state_s5_im, state_ffn_conv, norm_mix, w_in, w_out, ret_norm_w, ret_norm_b, diff_lq1, diff_lk1, diff_lq2, diff_lk2, diff_subln, rwkv_mu, rwkv_w0, rwkv_w2, rwkv_a0, rwkv_a2, rwkv_g2, rwkv_kk, rwkv_ka, rwkv_rk, rwkv_ln_w, rwkv_ln_b, s5_lam_re, s5_lam_im, s5_log_step, s5_b_re, s5_b_im, s5_c_re, s5_c_im, s5_d, s5_w_glu, s5_b_glu, s5_norm, norm_ffn, ffn_w_up, ffn_conv_w, ffn_conv_b, ffn_w_down, norm_ple, ple_w_proj, ple_norm_e, ple_w_gate, norm_final):
    depth = w_in.shape[0]
    bp, tp, d = x_prompt.shape
    nb, ts, _ = x_sample.shape
    assert bp == 1
    mp, ms = bp * tp, nb * ts
    past_len = page_table.shape[1] * PAGE_SIZE
    nch = S5_GROUPS * S5_N

    x = jnp.concatenate([x_prompt.reshape(mp, d), x_sample.reshape(ms, d)], axis=0)
    p_all = jnp.concatenate([p_prompt.reshape(depth, mp, -1), p_sample.reshape(depth, ms, -1)], axis=1)
    pos_p = jnp.arange(tp, dtype=jnp.int32)
    pos_s = past_len + jnp.arange(ts, dtype=jnp.int32)
    ret_cos_p, ret_sin_p = _ret_rope_tables(pos_p)
    ret_cos_s, ret_sin_s = _ret_rope_tables(pos_s)
    pos_all = jnp.concatenate([pos_p, jnp.tile(pos_s, nb)])
    dc, ds1, ds2 = _diff_rope_tables(pos_all)
    cache_k2 = cache_k.reshape(cache_k.shape[0], cache_k.shape[1], PAGE_SIZE * DIFF_HEADS, DIFF_VD)
    cache_v2 = cache_v.reshape(cache_v.shape[0], cache_v.shape[1], PAGE_SIZE * DIFF_HEADS, DIFF_VD)
    head_of = jnp.arange(GROUP_W) // RWKV_HD
    e512 = (head_of[:, None] == head_of[None, :]).astype(BF16)
    e128 = e512[:LANES, :LANES]

    outs = {k: [] for k in ("ret_p", "ret_s", "rwkv_p", "rwkv_s", "sh_p", "sh_s",
                            "s5r_p", "s5i_p", "s5r_s", "s5i_s", "conv_p", "conv_s")}
    kv_out = None
    for i in range(depth):
        h = rmsnorm_rows(x, norm_mix[i], BF16)
        c_ret = matmul(h, w_in, layer=i, tn=1024, tm=1024, col0=0, ncols=RET_COLS, name="proj_ret")
        c_diff = matmul(h, w_in, layer=i, tn=512, tm=1024, col0=RET_COLS, ncols=DIFF_COLS, name="proj_diff")
        c_rwkv = matmul(h, w_in, layer=i, tn=896, tm=1024, col0=RET_COLS + DIFF_COLS, ncols=RWKV_COLS,
                        name="proj_rwkv")
        c_s5 = matmul(h, w_in, layer=i, tn=256, tm=1024, col0=RET_COLS + DIFF_COLS + RWKV_COLS, ncols=S5_COLS,
                      name="proj_s5")

        o_ret_p, s_ret_p = retention(c_ret, 0, bp, tp, ret_cos_p, ret_sin_p,
                                     jnp.zeros((bp, RET_HEADS, RET_HD, RET_HD), F32),
                                     ret_norm_w[i], ret_norm_b[i], chunk=RET_CHUNK, chunks_per_step=4)
        o_ret_s, s_ret_s = retention(c_ret, mp, nb, ts, ret_cos_s, ret_sin_s, state_ret,
                                     ret_norm_w[i], ret_norm_b[i], chunk=ts, chunks_per_step=1, layer=i)

        lam_init = 0.8 - 0.6 * math.exp(-0.3 * i)
        lam = (jnp.exp(jnp.sum(diff_lq1[i] * diff_lk1[i])) - jnp.exp(jnp.sum(diff_lq2[i] * diff_lk2[i])) + lam_init)
        (q0, q1, k_new, kb, vt), kv_out = diff_prep(c_diff, dc, ds1, ds2, layer=i, depth=depth, mp=mp,
                                                    kv_prev=kv_out)
        o_diff_p = diff_attention_prompt(q0, q1, kb, vt, lam, diff_subln[i], lam_init, t=mp)
        qs = (q0[mp:].astype(F32) + q1[mp:].astype(F32)).reshape(nb, ts, GROUP_W)
        o_diff_s = diff_attention_sample(qs, k_new, c_diff, mp, cache_k2, cache_v2, i, page_table,
                                         lam, diff_subln[i], lam_init)

        cr_s = c_rwkv[mp:].reshape(nb, ts, RWKV_COLS)
        prev_s = jnp.concatenate([state_rwkv_shift[i][:, None], cr_s[:, :-1]], axis=1).reshape(ms, RWKV_COLS)
        rp = dict(mu=rwkv_mu[i], w0=rwkv_w0[i], w2=rwkv_w2[i], a0=rwkv_a0[i], a2=rwkv_a2[i], g2=rwkv_g2[i],
                  kk=rwkv_kk[i], ka=rwkv_ka[i], rk=rwkv_rk[i].reshape(-1))
        lnw, lnb = rwkv_ln_w[i].reshape(-1), rwkv_ln_b[i].reshape(-1)
        o_rwkv_p, s_rwkv_p = rwkv_scan(*rwkv_prep(c_rwkv, None, rp, e512, row0=0, m=mp), lnw, lnb, e128,
                                       jnp.zeros((bp, RWKV_HEADS // 2, RWKV_HD, LANES), F32),
                                       t_seq=tp, seq_per_step=1, t_step=256, pairs=4)
        o_rwkv_s, s_rwkv_s = rwkv_scan(*rwkv_prep(c_rwkv, prev_s, rp, e512, row0=mp, m=ms), lnw, lnb, e128,
                                       _rwkv_state_to_pairs(state_rwkv[i]),
                                       t_seq=ts, seq_per_step=16, t_step=ts, pairs=4)

        sp = dict(lam_re=s5_lam_re[i], lam_im=s5_lam_im[i], log_step=s5_log_step[i], b_re=s5_b_re[i], b_im=s5_b_im[i],
                  c_re=s5_c_re[i], c_im=s5_c_im[i], d=s5_d[i], w_glu=s5_w_glu[i], b_glu=s5_b_glu[i], norm=s5_norm[i])
        o_s5_p, s5r_p, s5i_p = s5_mixer(c_s5, sp, jnp.zeros((1, nch), F32), jnp.zeros((1, nch), F32),
                                        per_group_state=False, row0=0, m=mp)
        o_s5_s, s5r_s, s5i_s = s5_mixer(c_s5, sp, state_s5_re[i].reshape(nb, nch), state_s5_im[i].reshape(nb, nch),
                                        per_group_state=True, row0=mp, m=ms)

        x = matmul_split([o_ret_p, o_diff_p, o_rwkv_p, o_s5_p], [o_ret_s, o_diff_s, o_rwkv_s, o_s5_s], w_out,
                         layer=i, tn=512, tm=1024, tiles=(x,), epilogue=lambda acc, res: res + acc, name="w_out")

        h2 = rmsnorm_rows(x, norm_ffn[i], BF16)
        act_p, taps_p = ffn_up(h2, ffn_w_up, ffn_conv_w[i], ffn_conv_b[i], jnp.zeros((bp, 2, 2 * D_FF), F32),
                               layer=i, time_major=False, row0=0, m=mp, tm=1024, out_rows=mp + ms)
        act_s, taps_s = ffn_up(h2[mp:].reshape(nb, ts, d), ffn_w_up, ffn_conv_w[i], ffn_conv_b[i],
                               state_ffn_conv[i], layer=i, time_major=True)
        act = lax.dynamic_update_slice(act_p, act_s, (mp, 0))
        x = matmul(act, ffn_w_down, layer=i, tn=512, tm=512, tiles=(x,),
                   epilogue=lambda acc, res: res + acc, name="ffn_down")

        e = matmul(p_all[i], ple_w_proj, layer=i, tn=d, tm=256, rows=(ple_norm_e[i],),
                   epilogue=lambda acc, g: acc * lax.rsqrt(jnp.mean(acc * acc, axis=-1, keepdims=True) + EPS) * g,
                   out_dtype=BF16, name="ple_proj")
        h3 = rmsnorm_rows(x, norm_ple[i], BF16)
        x = matmul(h3, ple_w_gate, layer=i, tn=512, tm=1024, tiles=(x, e),
                   epilogue=lambda acc, res, ee: res + ee.astype(F32) * jax.nn.sigmoid(acc), name="ple_gate")

        outs["ret_p"].append(s_ret_p)
        outs["ret_s"].append(s_ret_s)
        outs["rwkv_p"].append(_rwkv_state_from_pairs(s_rwkv_p))
        outs["rwkv_s"].append(_rwkv_state_from_pairs(s_rwkv_s))
        outs["sh_p"].append(c_rwkv[mp - 1:mp].reshape(bp, RWKV_COLS))
        outs["sh_s"].append(cr_s[:, -1])
        outs["s5r_p"].append(s5r_p.reshape(bp, S5_GROUPS, S5_N))
        outs["s5i_p"].append(s5i_p.reshape(bp, S5_GROUPS, S5_N))
        outs["s5r_s"].append(s5r_s.reshape(nb, S5_GROUPS, S5_N))
        outs["s5i_s"].append(s5i_s.reshape(nb, S5_GROUPS, S5_N))
        outs["conv_p"].append(conv_state_from_taps(taps_p))
        outs["conv_s"].append(conv_state_from_taps(taps_s))

    y_p, y_s = rmsnorm_rows_split(x, norm_final, mp)
    st = lambda k: jnp.stack(outs[k])
    k_p, k_s, v_p, v_s = kv_out
    return (y_p.reshape(bp, tp, d), y_s.reshape(nb, ts, d),
            k_p.reshape(depth, bp, tp, DIFF_HEADS, DIFF_VD), v_p.reshape(depth, bp, tp, DIFF_HEADS, DIFF_VD),
            k_s.reshape(depth, nb, ts, DIFF_HEADS, DIFF_VD), v_s.reshape(depth, nb, ts, DIFF_HEADS, DIFF_VD),
            st("ret_p"), st("ret_s"), st("rwkv_p"), st("rwkv_s"),
            st("sh_p"), st("sh_s"), st("s5r_p"), st("s5i_p"), st("s5r_s"), st("s5i_s"), st("conv_p"), st("conv_s"))
```
